```python
import math
import jax
import jax.numpy as jnp
from jax import lax
import numpy as np

D_MODEL = 4096
BATCH = 4
SEQ = 2048
DEPTH = 2
DEC_BATCH = 8
DEC_SEQ = 1
PAST_LEN = 16384
PAGE_SIZE = 128

MLSTM_HEADS = 4
MLSTM_DV = D_MODEL // (4 * MLSTM_HEADS)
MLSTM_DQK = MLSTM_DV // 2
MLSTM_CHUNK = 64
DIFF_HEADS = 8
DIFF_DV = D_MODEL // (2 * DIFF_HEADS)
DIFF_DQK = DIFF_DV // 2
Q_BLOCK = 128
POOL_GROUPS = 4
POOL_WINDOWS = (2, 4, 8, 16)
POOL_GC = D_MODEL // (4 * POOL_GROUPS)
POOL_CH = POOL_GROUPS * POOL_GC
POOL_BUF = 15
N_BRANCH = 3
D_FF = 11008
CONV_W = 3
REL_BUCKETS = 32
REL_MAX_DIST = 128
RMS_EPS = 1e-6

MLSTM_QK_W = MLSTM_HEADS * MLSTM_DQK
MLSTM_V_W = MLSTM_HEADS * MLSTM_DV
DIFF_QK_W = DIFF_HEADS * 2 * DIFF_DQK
DIFF_V_W = DIFF_HEADS * DIFF_DV
IN_SPLITS = (MLSTM_QK_W, MLSTM_QK_W, MLSTM_V_W, 2 * MLSTM_HEADS, MLSTM_V_W,
             DIFF_QK_W, DIFF_QK_W, DIFF_V_W, POOL_CH, N_BRANCH * D_MODEL)
N_IN = (2 * MLSTM_QK_W + 2 * MLSTM_V_W + 2 * MLSTM_HEADS
        + 2 * DIFF_QK_W + DIFF_V_W + POOL_CH + N_BRANCH * D_MODEL)

kernel_name = 'hybrid_mlstm_diffattn_pool_decoder'


def _split_points():
    pts, acc = [], 0
    for s in IN_SPLITS[:-1]:
        acc += s
        pts.append(acc)
    return pts


def rmsnorm(x, g):
    xf = x.astype(jnp.float32)
    y = xf * lax.rsqrt(jnp.mean(xf * xf, axis=-1, keepdims=True) + RMS_EPS)
    return (y * g.astype(jnp.float32)).astype(x.dtype)


def t5_bucket(rel):
    max_exact = REL_BUCKETS // 2
    n = jnp.maximum(-rel, 0)
    large = max_exact + (jnp.log(jnp.maximum(n, 1).astype(jnp.float32) / max_exact)
                         / math.log(REL_MAX_DIST / max_exact)
                         * (REL_BUCKETS - max_exact)).astype(jnp.int32)
    large = jnp.minimum(large, REL_BUCKETS - 1)
    return jnp.where(n < max_exact, n, large)


def mlstm_chunkwise(q, k, v, i_pre, f_pre, c0, n0, m0):
    f32 = jnp.float32
    B, L, H, DK = q.shape
    DV = v.shape[-1]
    cs = MLSTM_CHUNK if L % MLSTM_CHUNK == 0 else L
    nc = L // cs

    def chunks(a):
        a = a.astype(f32).reshape((B, nc, cs, H) + a.shape[3:])
        return jnp.moveaxis(a, (1, 3), (0, 2))

    qc, kc, vc = chunks(q), chunks(k), chunks(v)
    lic = chunks(i_pre)
    lfc = jax.nn.log_sigmoid(chunks(f_pre))
    causal = jnp.tril(jnp.ones((cs, cs), dtype=bool))

    def step(carry, inp):
        c, n, m = carry
        qb, kb, vb, li, lf = inp
        b = jnp.cumsum(lf, axis=-1)
        dmat = jnp.where(causal, b[..., :, None] - b[..., None, :] + li[..., None, :], -jnp.inf)
        inter = b + m[..., None]
        m_t = jnp.maximum(inter, jnp.max(dmat, axis=-1))
        w_inter = jnp.exp(inter - m_t)
        sc = jnp.einsum('bhtd,bhsd->bhts', qb, kb) * jnp.exp(dmat - m_t[..., None])
        num = (w_inter[..., None] * jnp.einsum('bhtd,bhde->bhte', qb, c)
               + jnp.einsum('bhts,bhse->bhte', sc, vb))
        den = w_inter * jnp.einsum('bhtd,bhd->bht', qb, n) + jnp.sum(sc, axis=-1)
        h = num / jnp.maximum(jnp.abs(den), jnp.exp(-m_t))[..., None]
        m_new = m_t[..., -1]
        g_state = jnp.exp(b[..., -1] + m - m_new)
        g_tok = jnp.exp(b[..., -1:] - b + li - m_new[..., None])
        c_new = g_state[..., None, None] * c + jnp.einsum('bhs,bhsd,bhse->bhde', g_tok, kb, vb)
        n_new = g_state[..., None] * n + jnp.einsum('bhs,bhsd->bhd', g_tok, kb)
        return (c_new, n_new, m_new), h

    (c, n, m), h = lax.scan(step, (c0.astype(f32), n0.astype(f32), m0.astype(f32)),
                            (qc, kc, vc, lic, lfc))
    h = jnp.transpose(h, (1, 0, 3, 2, 4)).reshape(B, L, H, DV)
    return h, c, n, m


def diff_attend(q, k, v, q_pos, k_pos, lam, rel_bias):
    f32 = jnp.float32
    s = jnp.einsum('bqhmd,bkhmd->bhmqk', q.astype(f32), k.astype(f32)) * (DIFF_DQK ** -0.5)
    rel = k_pos[None, :] - q_pos[:, None]
    bias = jnp.transpose(rel_bias.astype(f32)[t5_bucket(rel)], (2, 0, 1))
    s = s + bias[None, :, None]
    s = jnp.where((rel <= 0)[None, None, None], s, -jnp.inf)
    p = jax.nn.softmax(s, axis=-1)
    a = p[:, :, 0] - lam * p[:, :, 1]
    return jnp.einsum('bhqk,bkhe->bqhe', a, v.astype(f32))


def diff_attention_prompt(q, k, v, lam, rel_bias):
    B, L = q.shape[:2]
    qb = Q_BLOCK if L % Q_BLOCK == 0 else L
    nb = L // qb
    q_blocks = jnp.moveaxis(q.reshape(B, nb, qb, DIFF_HEADS, 2, DIFF_DQK), 1, 0)
    k_pos = jnp.arange(L, dtype=jnp.int32)

    def one_block(args):
        q_blk, blk = args
        q_pos = blk * qb + jnp.arange(qb, dtype=jnp.int32)
        return diff_attend(q_blk, k, v, q_pos, k_pos, lam, rel_bias)

    out = lax.map(one_block, (q_blocks, jnp.arange(nb, dtype=jnp.int32)))
    return jnp.moveaxis(out, 0, 1).reshape(B, L, DIFF_HEADS, DIFF_DV)


def pool_mix(u, buf, n_valid, pool_w, pool_scale):
    f32 = jnp.float32
    B, L, _ = u.shape
    ucat = jnp.concatenate([buf.astype(u.dtype), u], axis=1)
    uf = ucat.astype(f32)
    cs = jnp.concatenate([jnp.zeros((B, 1, POOL_CH), f32), jnp.cumsum(uf, axis=1)], axis=1)
    t = jnp.arange(L, dtype=jnp.int32)
    groups = []
    for g, w in enumerate(POOL_WINDOWS):
        sl = slice(g * POOL_GC, (g + 1) * POOL_GC)
        win_sum = cs[:, POOL_BUF + 1:POOL_BUF + 1 + L, sl] - cs[:, POOL_BUF + 1 - w:POOL_BUF + 1 - w + L, sl]
        cnt = jnp.minimum(w, n_valid + t + 1).astype(f32)
        groups.append(win_sum / cnt[None, :, None] - uf[:, POOL_BUF:, sl])
    pooled = jnp.stack(groups, axis=2)
    mixed = jnp.einsum('blgc,gcd->blgd', pooled, pool_w.astype(f32)).reshape(B, L, POOL_CH)
    return (mixed * pool_scale.astype(f32)).astype(u.dtype), ucat[:, L:]


def conv_ffn(h, buf, w_up, conv_w, conv_b, w_down):
    L = h.shape[1]
    u = h @ w_up
    ucat = jnp.concatenate([buf.astype(u.dtype), u], axis=1)
    c = conv_b + sum(conv_w[j] * ucat[:, j:j + L] for j in range(CONV_W))
    gate, val = jnp.split(c, 2, axis=-1)
    return (jax.nn.silu(gate) * val) @ w_down, ucat[:, L:]


def trunk_layer(x, start, past_k, past_v, c0, n0, m0, pool_buf, pool_valid, conv_buf,
                rel_bias, layer_idx, norm_mix, w_in, b_in, mlstm_norm, q_norm, k_norm,
                lambda_q, lambda_k, diff_subln, pool_w, pool_scale, w_br_mlstm, w_br_diff,
                w_br_pool, w_out, norm_ffn, w_up, conv_w, conv_b, w_down):
    f32 = jnp.float32
    B, L, _ = x.shape
    h = rmsnorm(x, norm_mix)
    proj = h @ w_in + b_in
    mq, mk, mv, mif, mo, dq, dkk, dvv, pu, gates = jnp.split(proj, _split_points(), axis=-1)

    mq = mq.reshape(B, L, MLSTM_HEADS, MLSTM_DQK)
    mk = mk.reshape(B, L, MLSTM_HEADS, MLSTM_DQK) * (MLSTM_DQK ** -0.5)
    mv = mv.reshape(B, L, MLSTM_HEADS, MLSTM_DV)
    hm, c1, n1, m1 = mlstm_chunkwise(mq, mk, mv, mif[..., :MLSTM_HEADS], mif[..., MLSTM_HEADS:], c0, n0, m0)
    hm = (rmsnorm(hm, mlstm_norm).reshape(B, L, MLSTM_V_W) * jax.nn.sigmoid(mo.astype(f32))).astype(x.dtype)

    dq = rmsnorm(dq.reshape(B, L, DIFF_HEADS, 2, DIFF_DQK), q_norm)
    dkk = rmsnorm(dkk.reshape(B, L, DIFF_HEADS, 2, DIFF_DQK), k_norm)
    dvv = dvv.reshape(B, L, DIFF_HEADS, DIFF_DV)
    lam_init = 0.8 - 0.6 * math.exp(-0.3 * layer_idx)
    lq = lambda_q.astype(f32)
    lk = lambda_k.astype(f32)
    lam = jnp.exp(jnp.sum(lq[0] * lk[0])) - jnp.exp(jnp.sum(lq[1] * lk[1])) + lam_init
    if past_k is None:
        hd = diff_attention_prompt(dq, dkk, dvv, lam, rel_bias)
    else:
        k_all = jnp.concatenate([past_k.astype(dkk.dtype), dkk], axis=1)
        v_all = jnp.concatenate([past_v.astype(dvv.dtype), dvv], axis=1)
        q_pos = start + jnp.arange(L, dtype=jnp.int32)
        k_pos = jnp.arange(past_k.shape[1] + L, dtype=jnp.int32)
        hd = diff_attend(dq, k_all, v_all, q_pos, k_pos, lam, rel_bias)
    hd = (rmsnorm(hd, diff_subln) * (1.0 - lam_init)).reshape(B, L, DIFF_V_W).astype(x.dtype)

    hp, pool_new = pool_mix(pu, pool_buf, pool_valid, pool_w, pool_scale)

    g = jax.nn.sigmoid(gates.astype(f32)).reshape(B, L, N_BRANCH, D_MODEL)
    merged = (g[:, :, 0] * (hm @ w_br_mlstm).astype(f32)
              + g[:, :, 1] * (hd @ w_br_diff).astype(f32)
              + g[:, :, 2] * (hp @ w_br_pool).astype(f32))
    x = x + merged.astype(x.dtype) @ w_out

    y, conv_new = conv_ffn(rmsnorm(x, norm_ffn), conv_buf, w_up, conv_w, conv_b, w_down)
    x = x + y
    return x, dkk.reshape(B, L, DIFF_HEADS, 2 * DIFF_DQK), dvv, c1, n1, m1, pool_new, conv_new


def setup_inputs(seed: int = 0) -> dict:
    key = jax.random.key(seed)
    keys = jax.random.split(key, 40)
    f32 = jnp.float32

    def nrm(i, shape, scale):
        return jax.random.normal(keys[i], shape, f32) * scale

    n_pages = PAST_LEN // PAGE_SIZE
    n_used = DEC_BATCH * n_pages
    n_pool = n_used + (n_used + 3) // 4
    page_table = jax.random.permutation(keys[0], n_pool)[:n_used].reshape(DEC_BATCH, n_pages).astype(jnp.int32)
    f_off = _split_points()[2] + MLSTM_HEADS
    b_in = nrm(11, (DEPTH, N_IN), 0.02).at[:, f_off:f_off + MLSTM_HEADS].add(3.0)
    return {
        'x_prompt': nrm(1, (BATCH, SEQ, D_MODEL), 1.0),
        'x_sample': nrm(2, (DEC_BATCH, DEC_SEQ, D_MODEL), 1.0),
        'cache_k': nrm(3, (DEPTH, n_pool, PAGE_SIZE, DIFF_HEADS, 2 * DIFF_DQK), 1.0),
        'cache_v': nrm(4, (DEPTH, n_pool, PAGE_SIZE, DIFF_HEADS, DIFF_DV), 1.0),
        'page_table': page_table,
        'state_mlstm_c': nrm(5, (DEPTH, DEC_BATCH, MLSTM_HEADS, MLSTM_DQK, MLSTM_DV), 0.5),
        'state_mlstm_n': nrm(6, (DEPTH, DEC_BATCH, MLSTM_HEADS, MLSTM_DQK), 0.5),
        'state_mlstm_m': nrm(7, (DEPTH, DEC_BATCH, MLSTM_HEADS), 1.0),
        'state_pool': nrm(8, (DEPTH, DEC_BATCH, POOL_BUF, POOL_CH), 1.0),
        'state_conv': nrm(9, (DEPTH, DEC_BATCH, CONV_W - 1, 2 * D_FF), 1.0),
        'rel_bias': nrm(10, (REL_BUCKETS, DIFF_HEADS), 0.1),
        'norm_mix': 1.0 + nrm(12, (DEPTH, D_MODEL), 0.02),
        'w_in': nrm(13, (DEPTH, D_MODEL, N_IN), D_MODEL ** -0.5),
        'b_in': b_in,
        'mlstm_norm': 1.0 + nrm(14, (DEPTH, MLSTM_HEADS, MLSTM_DV), 0.02),
        'q_norm': 1.0 + nrm(15, (DEPTH, 2, DIFF_DQK), 0.02),
        'k_norm': 1.0 + nrm(16, (DEPTH, 2, DIFF_DQK), 0.02),
        'lambda_q': nrm(17, (DEPTH, 2, DIFF_DQK), 0.1),
        'lambda_k': nrm(18, (DEPTH, 2, DIFF_DQK), 0.1),
        'diff_subln': 1.0 + nrm(19, (DEPTH, DIFF_DV), 0.02),
        'pool_w': nrm(20, (DEPTH, POOL_GROUPS, POOL_GC, POOL_GC), POOL_GC ** -0.5),
        'pool_scale': 1.0 + nrm(21, (DEPTH, POOL_CH), 0.1),
        'w_br_mlstm': nrm(22, (DEPTH, MLSTM_V_W, D_MODEL), MLSTM_V_W ** -0.5),
        'w_br_diff': nrm(23, (DEPTH, DIFF_V_W, D_MODEL), DIFF_V_W ** -0.5),
        'w_br_pool': nrm(24, (DEPTH, POOL_CH, D_MODEL), POOL_CH ** -0.5),
        'w_out': nrm(25, (DEPTH, D_MODEL, D_MODEL), D_MODEL ** -0.5),
        'norm_ffn': 1.0 + nrm(26, (DEPTH, D_MODEL), 0.02),
        'w_up': nrm(27, (DEPTH, D_MODEL, 2 * D_FF), D_MODEL ** -0.5),
        'conv_w': nrm(28, (DEPTH, CONV_W, 2 * D_FF), CONV_W ** -0.5),
        'conv_b': nrm(29, (DEPTH, 2 * D_FF), 0.02),
        'w_down': nrm(30, (DEPTH, D_FF, D_MODEL), D_FF ** -0.5),
    }


def reference(x_prompt, x_sample, cache_k, cache_v, page_table, state_mlstm_c, state_mlstm_n,
              state_mlstm_m, state_pool, state_conv, rel_bias, norm_mix, w_in, b_in, mlstm_norm,
              q_norm, k_norm, lambda_q, lambda_k, diff_subln, pool_w, pool_scale, w_br_mlstm,
              w_br_diff, w_br_pool, w_out, norm_ffn, w_up, conv_w, conv_b, w_down):
    f32 = jnp.float32
    n_dec, n_pages = page_table.shape
    past_len = n_pages * PAGE_SIZE
    n_prompt = x_prompt.shape[0]
    xp, xs = x_prompt, x_sample
    kp_l, vp_l, ks_l, vs_l = [], [], [], []
    cp_l, np_l, mp_l, cs_l, ns_l, ms_l = [], [], [], [], [], []
    pp_l, ps_l, cvp_l, cvs_l = [], [], [], []
    for l in range(DEPTH):
        def run(x, start, past_k, past_v, c0, n0, m0, pbuf, pvalid, cbuf, l=l):
            return trunk_layer(x, start, past_k, past_v, c0, n0, m0, pbuf, pvalid, cbuf,
                               rel_bias, l, norm_mix[l], w_in[l], b_in[l], mlstm_norm[l],
                               q_norm[l], k_norm[l], lambda_q[l], lambda_k[l], diff_subln[l],
                               pool_w[l], pool_scale[l], w_br_mlstm[l], w_br_diff[l],
                               w_br_pool[l], w_out[l], norm_ffn[l], w_up[l], conv_w[l],
                               conv_b[l], w_down[l])

        xp, kp, vp, cp, n_p, mp, pp, cvp = run(
            xp, 0, None, None,
            jnp.zeros((n_prompt, MLSTM_HEADS, MLSTM_DQK, MLSTM_DV), f32),
            jnp.zeros((n_prompt, MLSTM_HEADS, MLSTM_DQK), f32),
            jnp.zeros((n_prompt, MLSTM_HEADS), f32),
            jnp.zeros((n_prompt, POOL_BUF, POOL_CH), xp.dtype), 0,
            jnp.zeros((n_prompt, CONV_W - 1, 2 * D_FF), xp.dtype))
        past_k = cache_k[l, page_table].reshape(n_dec, past_len, DIFF_HEADS, 2, DIFF_DQK)
        past_v = cache_v[l, page_table].reshape(n_dec, past_len, DIFF_HEADS, DIFF_DV)
        xs, ks, vs, c_s, n_s, m_s, ps, cvs = run(
            xs, past_len, past_k, past_v, state_mlstm_c[l], state_mlstm_n[l], state_mlstm_m[l],
            state_pool[l], min(POOL_BUF, past_len), state_conv[l])
        kp_l.append(kp); vp_l.append(vp); ks_l.append(ks); vs_l.append(vs)
        cp_l.append(cp); np_l.append(n_p); mp_l.append(mp)
        cs_l.append(c_s); ns_l.append(n_s); ms_l.append(m_s)
        pp_l.append(pp); ps_l.append(ps); cvp_l.append(cvp); cvs_l.append(cvs)
    return (xp, xs,
            jnp.stack(kp_l), jnp.stack(vp_l), jnp.stack(ks_l), jnp.stack(vs_l),
            jnp.stack(cp_l), jnp.stack(np_l), jnp.stack(mp_l),
            jnp.stack(cs_l), jnp.stack(ns_l), jnp.stack(ms_l),
            jnp.stack(pp_l), jnp.stack(ps_l), jnp.stack(cvp_l), jnp.stack(cvs_l))
```

```python
import functools
import math

import jax
import jax.numpy as jnp
from jax import lax
from jax.experimental import pallas as pl
from jax.experimental.pallas import tpu as pltpu

F32 = jnp.float32
BF16 = jnp.bfloat16

D_MODEL = 4096
PAGE_SIZE = 128
MLSTM_HEADS = 4
MLSTM_DV = 256
MLSTM_DQK = 128
MLSTM_CHUNK = 64
DIFF_HEADS = 8
DIFF_DV = 256
DIFF_DQK = 128
POOL_GROUPS = 4
POOL_WINDOWS = (2, 4, 8, 16)
POOL_GC = 256
POOL_CH = POOL_GROUPS * POOL_GC
POOL_BUF = 15
N_BRANCH = 3
D_FF = 11008
CONV_W = 3
REL_BUCKETS = 32
REL_MAX_DIST = 128
RMS_EPS = 1e-6

MLSTM_QK_W = MLSTM_HEADS * MLSTM_DQK
MLSTM_V_W = MLSTM_HEADS * MLSTM_DV
DIFF_QK_W = DIFF_HEADS * 2 * DIFF_DQK
DIFF_V_W = DIFF_HEADS * DIFF_DV

LANES = 128
SUBLANES = 8
VMEM_LIMIT = 56 * 1024 * 1024
NEG = -1e30

BIAS_SPAN = 128


def _cparams(sem):
    return pltpu.CompilerParams(dimension_semantics=sem, vmem_limit_bytes=VMEM_LIMIT)


def _rmsnorm_kernel(x_ref, g_ref, o_ref):
    x = x_ref[...]
    y = x * lax.rsqrt(jnp.mean(x * x, axis=-1, keepdims=True) + RMS_EPS)
    o_ref[...] = (y * g_ref[...]).astype(o_ref.dtype)


def _rmsnorm(x, g, tm):
    m, d = x.shape
    return pl.pallas_call(
        _rmsnorm_kernel,
        grid=(m // tm,),
        in_specs=[pl.BlockSpec((tm, d), lambda i: (i, 0)),
                  pl.BlockSpec((1, d), lambda i: (0, 0))],
        out_specs=pl.BlockSpec((tm, d), lambda i: (i, 0)),
        out_shape=jax.ShapeDtypeStruct((m, d), BF16),
        compiler_params=_cparams(("parallel",)),
        name="rmsnorm",
    )(x, g.reshape(1, d))


def _group_rmsnorm(y, g, width):
    outs = []
    for c in range(0, y.shape[1], width):
        yc = y[:, c:c + width]
        yn = yc * lax.rsqrt(jnp.mean(yc * yc, axis=-1, keepdims=True) + RMS_EPS)
        outs.append(yn * g[:, c:c + width])
    return outs[0] if len(outs) == 1 else jnp.concatenate(outs, axis=1)


def _ep_bias(acc, b):
    return (acc + b,)


def _ep_bias_bf16copy(acc, b):
    y = acc + b
    return (y, y.astype(BF16))


def _ep_bias_sigmoid(acc, b):
    return (jax.nn.sigmoid(acc + b),)


def _ep_qknorm(acc, b, g):
    y = _group_rmsnorm(acc + b, g, DIFF_DQK)
    return (y, y.astype(BF16))


def _ep_residual(acc, r):
    return (r + acc,)


def _mm_kernel(a_ref, w_ref, *rest, n_extra, n_out, epilogue):
    extras = rest[:n_extra]
    outs = rest[n_extra:n_extra + n_out]
    acc = jnp.dot(a_ref[...].astype(BF16), w_ref[...].astype(BF16), preferred_element_type=F32)
    res = epilogue(acc, *[e[...] for e in extras])
    for o, r in zip(outs, res):
        o[...] = r.astype(o.dtype)


def _matmul(a, w, extras, epilogue, out_dtypes, *, tm, tn, name):
    m, k = a.shape
    n = w.shape[1]
    tm = min(tm, m)
    tn = min(tn, n)
    assert m % tm == 0 and n % tn == 0, (m, n, tm, tn)
    in_specs = [pl.BlockSpec((tm, k), lambda i, j: (i, 0)),
                pl.BlockSpec((k, tn), lambda i, j: (0, j))]
    for e in extras:
        if e.shape[0] == 1 and m != 1:
            in_specs.append(pl.BlockSpec((1, tn), lambda i, j: (0, j)))
        else:
            in_specs.append(pl.BlockSpec((tm, tn), lambda i, j: (i, j)))
    out_specs = [pl.BlockSpec((tm, tn), lambda i, j: (i, j)) for _ in out_dtypes]
    out_shape = [jax.ShapeDtypeStruct((m, n), dt) for dt in out_dtypes]
    return pl.pallas_call(
        functools.partial(_mm_kernel, n_extra=len(extras), n_out=len(out_dtypes), epilogue=epilogue),
        grid=(m // tm, n // tn),
        in_specs=in_specs,
        out_specs=out_specs,
        out_shape=out_shape,
        compiler_params=_cparams(("parallel", "arbitrary")),
        name=name,
    )(a, w, *extras)


def _merge_kernel(hm_ref, hd_ref, hp_ref, wm_ref, wd_ref, wp_ref, g0_ref, g1_ref, g2_ref, o_ref):
    def proj(h_ref, w_ref):
        return jnp.dot(h_ref[...].astype(BF16), w_ref[...], preferred_element_type=F32)
    merged = (g0_ref[...] * proj(hm_ref, wm_ref)
              + g1_ref[...] * proj(hd_ref, wd_ref)
              + g2_ref[...] * proj(hp_ref, wp_ref))
    o_ref[...] = merged.astype(o_ref.dtype)


def _merge(hm, hd, hp, wm, wd, wp, gates, *, tm, tn):
    m = hm.shape[0]
    tm = min(tm, m)
    nj = D_MODEL // tn
    row = lambda width: pl.BlockSpec((tm, width), lambda i, j: (i, 0))
    col = lambda depth: pl.BlockSpec((depth, tn), lambda i, j: (0, j))
    gate = lambda br: pl.BlockSpec((tm, tn), lambda i, j: (i, br * nj + j))
    return pl.pallas_call(
        _merge_kernel,
        grid=(m // tm, nj),
        in_specs=[row(MLSTM_V_W), row(DIFF_V_W), row(POOL_CH),
                  col(MLSTM_V_W), col(DIFF_V_W), col(POOL_CH),
                  gate(0), gate(1), gate(2)],
        out_specs=pl.BlockSpec((tm, tn), lambda i, j: (i, j)),
        out_shape=jax.ShapeDtypeStruct((m, D_MODEL), BF16),
        compiler_params=_cparams(("parallel", "arbitrary")),
        name="merge",
    )(hm, hd, hp, wm, wd, wp, gates, gates, gates)


def _conv_gate(ug, uv, pg, pv, cw_g, cw_v, cb_g, cb_v):
    cg = cb_g + (cw_g[0:1] * pg[0] + cw_g[1:2] * pg[1] + cw_g[2:3] * ug)
    cv = cb_v + (cw_v[0:1] * pv[0] + cw_v[1:2] * pv[1] + cw_v[2:3] * uv)
    return jax.nn.silu(cg) * cv


def _shifted_rows(u, tail):
    ext = jnp.concatenate([tail, u], axis=0)
    n = u.shape[0]
    back1 = pltpu.roll(ext, 1, 0)[SUBLANES:SUBLANES + n]
    back2 = pltpu.roll(ext, 2, 0)[SUBLANES:SUBLANES + n]
    return back2, back1


def _ffn_up_prompt_kernel(a_ref, wg_ref, wv_ref, cwg_ref, cwv_ref, cbg_ref, cbv_ref,
                          act_ref, tail_ref, carry_ref, *, tiles_per_seq):
    i = pl.program_id(0)
    j = pl.program_id(1)
    a = a_ref[...]
    ug = jnp.dot(a, wg_ref[...], preferred_element_type=F32)
    uv = jnp.dot(a, wv_ref[...], preferred_element_type=F32)
    tm = ug.shape[0]

    @pl.when(i % tiles_per_seq == 0)
    def _():
        carry_ref[:, pl.ds(j, 1)] = jnp.zeros((2, 1) + carry_ref.shape[2:], F32)

    tail_g = carry_ref[0, j]
    tail_v = carry_ref[1, j]
    act = _conv_gate(ug, uv, _shifted_rows(ug, tail_g), _shifted_rows(uv, tail_v),
                     cwg_ref[...], cwv_ref[...], cbg_ref[...], cbv_ref[...])
    act_ref[...] = act.astype(act_ref.dtype)
    new_g = ug[tm - SUBLANES:]
    new_v = uv[tm - SUBLANES:]
    carry_ref[0, j] = new_g
    carry_ref[1, j] = new_v
    tail_ref[0, 0] = new_g
    tail_ref[0, 1] = new_v


def _ffn_up_prompt(h, w_up, conv_w, conv_b, *, batch, tm, tn):
    m, d = h.shape
    nj = D_FF // tn
    tiles_per_seq = (m // batch) // tm
    conv_b = conv_b.reshape(1, 2 * D_FF)
    act, tail = pl.pallas_call(
        functools.partial(_ffn_up_prompt_kernel, tiles_per_seq=tiles_per_seq),
        grid=(m // tm, nj),
        in_specs=[pl.BlockSpec((tm, d), lambda i, j: (i, 0)),
                  pl.BlockSpec((d, tn), lambda i, j: (0, j)),
                  pl.BlockSpec((d, tn), lambda i, j: (0, nj + j)),
                  pl.BlockSpec((CONV_W, tn), lambda i, j: (0, j)),
                  pl.BlockSpec((CONV_W, tn), lambda i, j: (0, nj + j)),
                  pl.BlockSpec((1, tn), lambda i, j: (0, j)),
                  pl.BlockSpec((1, tn), lambda i, j: (0, nj + j))],
        out_specs=[pl.BlockSpec((tm, tn), lambda i, j: (i, j)),
                   pl.BlockSpec((1, 2, SUBLANES, tn), lambda i, j: (i, 0, 0, j))],
        out_shape=[jax.ShapeDtypeStruct((m, D_FF), BF16),
                   jax.ShapeDtypeStruct((m // tm, 2, SUBLANES, D_FF), F32)],
        scratch_shapes=[pltpu.VMEM((2, nj, SUBLANES, tn), F32)],
        compiler_params=_cparams(("arbitrary", "arbitrary")),
        name="ffn_up_prompt",
    )(h, w_up, w_up, conv_w, conv_w, conv_b, conv_b)
    tail = tail[tiles_per_seq - 1::tiles_per_seq]
    conv_new = jnp.transpose(tail[:, :, SUBLANES - (CONV_W - 1):, :], (0, 2, 1, 3))
    return act, conv_new.reshape(batch, CONV_W - 1, 2 * D_FF)


def _ffn_up_sample_kernel(a_ref, wg_ref, wv_ref, cwg_ref, cwv_ref, cbg_ref, cbv_ref,
                          p2g_ref, p1g_ref, p2v_ref, p1v_ref, act_ref, ug_ref, uv_ref):
    a = a_ref[...].astype(BF16)
    ug = jnp.dot(a, wg_ref[...], preferred_element_type=F32)
    uv = jnp.dot(a, wv_ref[...], preferred_element_type=F32)
    act = _conv_gate(ug, uv, (p2g_ref[...], p1g_ref[...]), (p2v_ref[...], p1v_ref[...]),
                     cwg_ref[...], cwv_ref[...], cbg_ref[...], cbv_ref[...])
    act_ref[...] = act.astype(act_ref.dtype)
    ug_ref[...] = ug
    uv_ref[...] = uv


def _ffn_up_sample(h, w_up, conv_w, conv_b, conv_buf, *, tn):
    b, d = h.shape
    nj = D_FF // tn
    conv_b = conv_b.reshape(1, 2 * D_FF)
    prev2 = conv_buf[:, 0, :]
    prev1 = conv_buf[:, 1, :]
    lo = lambda rows: pl.BlockSpec((rows, tn), lambda j: (0, j))
    hi = lambda rows: pl.BlockSpec((rows, tn), lambda j: (0, nj + j))
    act, ug, uv = pl.pallas_call(
        _ffn_up_sample_kernel,
        grid=(nj,),
        in_specs=[pl.BlockSpec((b, d), lambda j: (0, 0)),
                  lo(d), hi(d), lo(CONV_W), hi(CONV_W), lo(1), hi(1),
                  lo(b), lo(b), hi(b), hi(b)],
        out_specs=[lo(b), lo(b), lo(b)],
        out_shape=[jax.ShapeDtypeStruct((b, D_FF), BF16),
                   jax.ShapeDtypeStruct((b, D_FF), F32),
                   jax.ShapeDtypeStruct((b, D_FF), F32)],
        compiler_params=_cparams(("arbitrary",)),
        name="ffn_up_sample",
    )(h, w_up, w_up, conv_w, conv_w, conv_b, conv_b, prev2, prev1, prev2, prev1)
    u = jnp.concatenate([ug, uv], axis=1)
    return act, jnp.concatenate([conv_buf[:, 1:], u[:, None, :]], axis=1)


def _log_sigmoid(x):
    return -(jnp.maximum(-x, 0.0) + jnp.log1p(jnp.exp(-jnp.abs(x))))


def _cumsum_rows(x):
    n = x.shape[0]
    row = lax.broadcasted_iota(jnp.int32, x.shape, 0)
    d = 1
    while d < n:
        x = x + jnp.where(row >= d, pltpu.roll(x, d, 0), 0.0)
        d *= 2
    return x


def _head_out_norm(h, g, o_gate):
    y = h * lax.rsqrt(jnp.mean(h * h, axis=-1, keepdims=True) + RMS_EPS)
    return (y * g) * jax.nn.sigmoid(o_gate)


def _mlstm_prompt_kernel(qkv_ref, mif_ref, mo_ref, nw_ref, hm_ref, c_out, n_out, m_out,
                         c_sc, n_sc, m_sc, *, cs, nc):
    ci = pl.program_id(1)

    @pl.when(ci == 0)
    def _():
        c_sc[...] = jnp.zeros(c_sc.shape, F32)
        n_sc[...] = jnp.zeros(n_sc.shape, F32)
        m_sc[...] = jnp.zeros(m_sc.shape, F32)

    mif = mif_ref[...]
    b_all = _cumsum_rows(_log_sigmoid(mif))
    row = lax.broadcasted_iota(jnp.int32, (cs, cs), 0)
    col = lax.broadcasted_iota(jnp.int32, (cs, cs), 1)
    causal = col <= row
    eye = col == row
    lane = lax.broadcasted_iota(jnp.int32, (1, LANES), 1)
    m_vec = jnp.zeros((1, LANES), F32)

    for h in range(MLSTM_HEADS):
        q = qkv_ref[:, h * MLSTM_DQK:(h + 1) * MLSTM_DQK]
        k = qkv_ref[:, MLSTM_QK_W + h * MLSTM_DQK:MLSTM_QK_W + (h + 1) * MLSTM_DQK] * (MLSTM_DQK ** -0.5)
        v = qkv_ref[:, 2 * MLSTM_QK_W + h * MLSTM_DV:2 * MLSTM_QK_W + (h + 1) * MLSTM_DV]
        li = mif[:, h:h + 1]
        b = b_all[:, MLSTM_HEADS + h:MLSTM_HEADS + h + 1]
        g_row = jnp.sum(jnp.where(eye, li - b, 0.0), axis=0, keepdims=True)
        dmat = jnp.where(causal, b + g_row, NEG)
        m_prev = m_sc[h]
        inter = b + m_prev
        m_t = jnp.maximum(inter, jnp.max(dmat, axis=-1, keepdims=True))
        w_inter = jnp.exp(inter - m_t)
        qb, kb, vb = q.astype(BF16), k.astype(BF16), v.astype(BF16)
        s = lax.dot_general(qb, kb, (((1,), (1,)), ((), ())), preferred_element_type=F32)
        sc = s * jnp.exp(dmat - m_t)
        c_prev = c_sc[h]
        n_prev = n_sc[h]
        num = (w_inter * jnp.dot(qb, c_prev.astype(BF16), preferred_element_type=F32)
               + jnp.dot(sc.astype(BF16), vb, preferred_element_type=F32))
        den = w_inter * jnp.sum(q * n_prev, axis=-1, keepdims=True) + jnp.sum(sc, axis=-1, keepdims=True)
        hh = num / jnp.maximum(jnp.abs(den), jnp.exp(-m_t))
        m_new = m_t[cs - 1:cs]
        b_last = b[cs - 1:cs]
        g_state = jnp.exp(b_last + m_prev - m_new)
        g_tok = jnp.exp(b_last - b + li - m_new)
        kg = k * g_tok
        c_new = g_state * c_prev + lax.dot_general(kg.astype(BF16), vb, (((0,), (0,)), ((), ())),
                                                   preferred_element_type=F32)
        n_new = g_state * n_prev + jnp.sum(kg, axis=0, keepdims=True)
        c_sc[h] = c_new
        n_sc[h] = n_new
        m_sc[h] = m_new
        m_vec = m_vec + jnp.where(lane == h, m_new, 0.0)
        sl = slice(h * MLSTM_DV, (h + 1) * MLSTM_DV)
        hm_ref[:, sl] = _head_out_norm(hh, nw_ref[:, sl], mo_ref[:, sl]).astype(hm_ref.dtype)

    @pl.when(ci == nc - 1)
    def _():
        c_out[0] = c_sc[...]
        n_out[0] = n_sc[...][:, 0, :]
        m_out[0] = m_vec


def _mlstm_prompt(mqkv, mif, mo, norm_w, *, batch):
    m = mqkv.shape[0]
    seq = m // batch
    cs = MLSTM_CHUNK if seq % MLSTM_CHUNK == 0 else seq
    nc = seq // cs
    rows = lambda width: pl.BlockSpec((cs, width), lambda b, c: (b * nc + c, 0))
    hm, c, n, mm = pl.pallas_call(
        functools.partial(_mlstm_prompt_kernel, cs=cs, nc=nc),
        grid=(batch, nc),
        in_specs=[rows(mqkv.shape[1]), rows(LANES), rows(MLSTM_V_W),
                  pl.BlockSpec((1, MLSTM_V_W), lambda b, c: (0, 0))],
        out_specs=[rows(MLSTM_V_W),
                   pl.BlockSpec((1, MLSTM_HEADS, MLSTM_DQK, MLSTM_DV), lambda b, c: (b, 0, 0, 0)),
                   pl.BlockSpec((1, MLSTM_HEADS, MLSTM_DQK), lambda b, c: (b, 0, 0)),
                   pl.BlockSpec((1, 1, LANES), lambda b, c: (b, 0, 0))],
        out_shape=[jax.ShapeDtypeStruct((m, MLSTM_V_W), BF16),
                   jax.ShapeDtypeStruct((batch, MLSTM_HEADS, MLSTM_DQK, MLSTM_DV), F32),
                   jax.ShapeDtypeStruct((batch, MLSTM_HEADS, MLSTM_DQK), F32),
                   jax.ShapeDtypeStruct((batch, 1, LANES), F32)],
        scratch_shapes=[pltpu.VMEM((MLSTM_HEADS, MLSTM_DQK, MLSTM_DV), F32),
                        pltpu.VMEM((MLSTM_HEADS, 1, MLSTM_DQK), F32),
                        pltpu.VMEM((MLSTM_HEADS, 1, 1), F32)],
        compiler_params=_cparams(("parallel", "arbitrary")),
        name="mlstm_prompt",
    )(mqkv, mif, mo, norm_w.reshape(1, MLSTM_V_W))
    return hm, c, n, mm[:, 0, :MLSTM_HEADS]


def _lanes_to_rows(x_row, eye):
    return jnp.sum(jnp.where(eye, x_row, 0.0), axis=1, keepdims=True)


def _mlstm_sample_kernel(qkv_ref, mif_ref, mo_ref, nw_ref, c_ref, n_ref, m_ref,
                         hm_ref, c_out, n_out, m_out):
    mif = mif_ref[0]
    lf_all = _log_sigmoid(mif)
    m_all = m_ref[0]
    row = lax.broadcasted_iota(jnp.int32, (MLSTM_DQK, MLSTM_DQK), 0)
    col = lax.broadcasted_iota(jnp.int32, (MLSTM_DQK, MLSTM_DQK), 1)
    eye = row == col
    lane = lax.broadcasted_iota(jnp.int32, (1, LANES), 1)
    m_vec = jnp.zeros((1, LANES), F32)
    for h in range(MLSTM_HEADS):
        q = qkv_ref[0, :, h * MLSTM_DQK:(h + 1) * MLSTM_DQK]
        k = qkv_ref[0, :, MLSTM_QK_W + h * MLSTM_DQK:MLSTM_QK_W + (h + 1) * MLSTM_DQK] * (MLSTM_DQK ** -0.5)
        v = qkv_ref[0, :, 2 * MLSTM_QK_W + h * MLSTM_DV:2 * MLSTM_QK_W + (h + 1) * MLSTM_DV]
        li = mif[:, h:h + 1]
        lf = lf_all[:, MLSTM_HEADS + h:MLSTM_HEADS + h + 1]
        m_prev = m_all[:, h:h + 1]
        inter = lf + m_prev
        m_t = jnp.maximum(inter, li)
        w_inter = jnp.exp(inter - m_t)
        sc = jnp.sum(q * k, axis=-1, keepdims=True) * jnp.exp(li - m_t)
        c_prev = c_ref[0, h]
        n_prev = n_ref[0, h:h + 1, :]
        q_col = _lanes_to_rows(q, eye)
        k_col = _lanes_to_rows(k, eye)
        num = w_inter * jnp.sum(q_col * c_prev, axis=0, keepdims=True) + sc * v
        den = w_inter * jnp.sum(q * n_prev, axis=-1, keepdims=True) + sc
        hh = num / jnp.maximum(jnp.abs(den), jnp.exp(-m_t))
        g_tok = jnp.exp(li - m_t)
        c_out[0, h] = w_inter * c_prev + (g_tok * k_col) * v
        n_out[0, h:h + 1, :] = w_inter * n_prev + g_tok * k
        m_vec = m_vec + jnp.where(lane == h, m_t, 0.0)
        sl = slice(h * MLSTM_DV, (h + 1) * MLSTM_DV)
        hm_ref[0, :, sl] = _head_out_norm(hh, nw_ref[:, sl], mo_ref[0, :, sl]).astype(hm_ref.dtype)
    m_out[0] = m_vec


def _mlstm_sample(mqkv, mif, mo, norm_w, c0, n0, m0):
    b = mqkv.shape[0]
    m0p = jnp.pad(m0, ((0, 0), (0, LANES - MLSTM_HEADS))).reshape(b, 1, LANES)
    per_seq = lambda width: pl.BlockSpec((1, 1, width), lambda i: (i, 0, 0))
    c_spec = pl.BlockSpec((1, MLSTM_HEADS, MLSTM_DQK, MLSTM_DV), lambda i: (i, 0, 0, 0))
    n_spec = pl.BlockSpec((1, MLSTM_HEADS, MLSTM_DQK), lambda i: (i, 0, 0))
    hm, c, n, mm = pl.pallas_call(
        _mlstm_sample_kernel,
        grid=(b,),
        in_specs=[per_seq(mqkv.shape[1]), per_seq(LANES), per_seq(MLSTM_V_W),
                  pl.BlockSpec((1, MLSTM_V_W), lambda i: (0, 0)),
                  c_spec, n_spec, per_seq(LANES)],
        out_specs=[per_seq(MLSTM_V_W), c_spec, n_spec, per_seq(LANES)],
        out_shape=[jax.ShapeDtypeStruct((b, 1, MLSTM_V_W), F32),
                   jax.ShapeDtypeStruct(c0.shape, F32),
                   jax.ShapeDtypeStruct(n0.shape, F32),
                   jax.ShapeDtypeStruct((b, 1, LANES), F32)],
        compiler_params=_cparams(("parallel",)),
        name="mlstm_sample",
    )(mqkv.reshape(b, 1, -1), mif.reshape(b, 1, LANES), mo.reshape(b, 1, -1),
      norm_w.reshape(1, MLSTM_V_W), c0, n0, m0p)
    return hm.reshape(b, MLSTM_V_W), c, n, mm[:, 0, :MLSTM_HEADS]


def _t5_bucket(n):
    max_exact = REL_BUCKETS // 2
    large = max_exact + (jnp.log(jnp.maximum(n, 1).astype(F32) / max_exact)
                         / math.log(REL_MAX_DIST / max_exact)
                         * (REL_BUCKETS - max_exact)).astype(jnp.int32)
    large = jnp.minimum(large, REL_BUCKETS - 1)
    return jnp.where(n < max_exact, n, large)


def _bias_by_distance(rel_bias):
    dist = jnp.arange(BIAS_SPAN + 1, dtype=jnp.int32)
    return jnp.transpose(rel_bias.astype(F32)[_t5_bucket(dist)], (1, 0))


def _lambda(lq_ref, lk_ref, lam_init):
    lq = lq_ref[...]
    lk = lk_ref[...]
    e0 = jnp.exp(jnp.sum(lq[0:1] * lk[0:1], axis=-1, keepdims=True))
    e1 = jnp.exp(jnp.sum(lq[1:2] * lk[1:2], axis=-1, keepdims=True))
    return e0 - e1 + lam_init


def _diff_out_norm(d, sub, lam_init):
    y = d * lax.rsqrt(jnp.mean(d * d, axis=-1, keepdims=True) + RMS_EPS)
    return (y * sub) * (1.0 - lam_init)


def _attn_prompt_kernel(q_ref, k_ref, v_ref, bias_ref, lq_ref, lk_ref, sub_ref, o_ref,
                        m_sc, l_sc, acc_sc, *, lam_init):
    qi = pl.program_id(2)
    ki = pl.program_id(3)

    @pl.when(ki == 0)
    def _():
        m_sc[...] = jnp.full(m_sc.shape, NEG, F32)
        l_sc[...] = jnp.zeros(l_sc.shape, F32)
        acc_sc[...] = jnp.zeros(acc_sc.shape, F32)

    @pl.when(ki <= qi)
    def _():
        bias = bias_ref[0, 0]
        v = v_ref[0]
        for mp in range(2):
            sl = slice(mp * DIFF_DQK, (mp + 1) * DIFF_DQK)
            s = lax.dot_general(q_ref[0, :, sl], k_ref[0, :, sl], (((1,), (1,)), ((), ())),
                                preferred_element_type=F32)
            s = s * (DIFF_DQK ** -0.5) + bias
            m_prev = m_sc[mp]
            m_new = jnp.maximum(m_prev, jnp.max(s, axis=-1, keepdims=True))
            alpha = jnp.exp(m_prev - m_new)
            p = jnp.exp(s - m_new)
            l_sc[mp] = alpha * l_sc[mp] + jnp.sum(p, axis=-1, keepdims=True)
            acc_sc[mp] = alpha * acc_sc[mp] + jnp.dot(p.astype(BF16), v, preferred_element_type=F32)
            m_sc[mp] = m_new

    @pl.when(ki == qi)
    def _():
        lam = _lambda(lq_ref, lk_ref, lam_init)
        d = acc_sc[0] / l_sc[0] - lam * (acc_sc[1] / l_sc[1])
        o_ref[0] = _diff_out_norm(d, sub_ref[...], lam_init).astype(o_ref.dtype)


def _attn_prompt(q, k, v, bias_tab, lam_q, lam_k, subln, *, batch, lam_init, tq):
    m = q.shape[0]
    seq = m // batch
    tq = min(tq, seq)
    assert tq >= BIAS_SPAN or tq == seq
    nq = seq // tq
    hw = 2 * DIFF_DQK
    i = jnp.arange(tq, dtype=jnp.int32)
    delta = i[:, None] - i[None, :]
    tiles = []
    for blk in range(3):
        dist = delta + blk * tq
        t = bias_tab[:, jnp.clip(dist, 0, BIAS_SPAN)]
        tiles.append(jnp.where((dist >= 0)[None], t, NEG))
    bias = jnp.stack(tiles, axis=1)

    q3, k3, v3 = (a.reshape(batch, seq, -1) for a in (q, k, v))
    kv_spec = pl.BlockSpec((1, tq, hw), lambda b, h, qi, ki: (b, jnp.minimum(ki, qi), h))
    out = pl.pallas_call(
        functools.partial(_attn_prompt_kernel, lam_init=lam_init),
        grid=(batch, DIFF_HEADS, nq, nq),
        in_specs=[pl.BlockSpec((1, tq, hw), lambda b, h, qi, ki: (b, qi, h)),
                  kv_spec, kv_spec,
                  pl.BlockSpec((1, 1, tq, tq), lambda b, h, qi, ki: (h, jnp.clip(qi - ki, 0, 2), 0, 0)),
                  pl.BlockSpec((2, DIFF_DQK), lambda b, h, qi, ki: (0, 0)),
                  pl.BlockSpec((2, DIFF_DQK), lambda b, h, qi, ki: (0, 0)),
                  pl.BlockSpec((1, DIFF_DV), lambda b, h, qi, ki: (0, 0))],
        out_specs=pl.BlockSpec((1, tq, DIFF_DV), lambda b, h, qi, ki: (b, qi, h)),
        out_shape=jax.ShapeDtypeStruct((batch, seq, DIFF_V_W), BF16),
        scratch_shapes=[pltpu.VMEM((2, tq, 1), F32), pltpu.VMEM((2, tq, 1), F32),
                        pltpu.VMEM((2, tq, DIFF_DV), F32)],
        compiler_params=_cparams(("parallel", "parallel", "parallel", "arbitrary")),
        name="attn_prompt",
    )(q3, k3, v3, bias, lam_q, lam_k, subln.reshape(1, DIFF_DV))
    return out.reshape(m, DIFF_V_W)


N_MAPS = 2 * DIFF_HEADS
PAGE_ROWS = PAGE_SIZE * DIFF_HEADS
NEW_TOKENS = 16


def _attn_sample_kernel(pt_ref, qm_ref, kn_ref, vn_ref, kc_ref, vc_ref, bfar_ref, blast_ref, bnew_ref,
                        lq_ref, lk_ref, sub_ref, o_ref, qx_sc, m_sc, l_sc, acc_sc, *, n_pages, lam_init):
    p = pl.program_id(1)
    sub_i = lax.broadcasted_iota(jnp.int32, (DIFF_HEADS, LANES), 0)
    lane_i = lax.broadcasted_iota(jnp.int32, (DIFF_HEADS, LANES), 1)
    own = (lane_i == 2 * sub_i) | (lane_i == 2 * sub_i + 1)

    @pl.when(p == 0)
    def _():
        qm = qm_ref[0]
        r = lax.broadcasted_iota(jnp.int32, qm.shape, 0)
        first = (r & 1) == 0
        qx = jnp.concatenate([jnp.where(first, qm, 0.0), jnp.where(first, 0.0, qm)], axis=1)
        qx_sc[...] = jnp.zeros(qx_sc.shape, BF16)
        qx_sc[0:N_MAPS, :] = qx.astype(BF16)
        m_sc[...] = jnp.where(own, NEG, 0.0)
        l_sc[...] = jnp.zeros(l_sc.shape, F32)
        acc_sc[...] = jnp.zeros(acc_sc.shape, F32)

    row = lax.broadcasted_iota(jnp.int32, (LANES, LANES), 0)
    col = lax.broadcasted_iota(jnp.int32, (LANES, LANES), 1)
    eye = row == col

    def per_lane_column(x):
        x_row = jnp.sum(jnp.where(own, x, 0.0), axis=0, keepdims=True)
        return jnp.sum(jnp.where(eye, x_row, 0.0), axis=1, keepdims=True)

    def page_update(kf, vf, bias3):
        rows = kf.shape[0]
        s_all = lax.dot_general(kf.astype(BF16), qx_sc[...], (((1,), (1,)), ((), ())),
                                preferred_element_type=F32)
        s3 = s_all.reshape(rows // DIFF_HEADS, DIFF_HEADS, LANES) * (DIFF_DQK ** -0.5) + bias3
        s3 = jnp.where(own[None], s3, NEG)
        m_prev = m_sc[...]
        m_new = jnp.maximum(m_prev, jnp.max(s3, axis=0))
        alpha = jnp.exp(m_prev - m_new)
        p3 = jnp.exp(s3 - m_new[None])
        l_sc[...] = alpha * l_sc[...] + jnp.sum(p3, axis=0)
        m_sc[...] = m_new
        p_t = p3.reshape(rows, LANES).T.astype(BF16)
        acc_sc[...] = per_lane_column(alpha) * acc_sc[...] + jnp.dot(
            p_t, vf.astype(BF16), preferred_element_type=F32)

    @pl.when(p < n_pages - 1)
    def _():
        page_update(kc_ref[...], vc_ref[...], bfar_ref[...][None])

    @pl.when(p == n_pages - 1)
    def _():
        page_update(kc_ref[...], vc_ref[...], blast_ref[...].reshape(PAGE_SIZE, DIFF_HEADS, LANES))
        page_update(kn_ref[0], vn_ref[0], bnew_ref[...].reshape(NEW_TOKENS, DIFF_HEADS, LANES))
        lam = _lambda(lq_ref, lk_ref, lam_init)
        l_col = per_lane_column(l_sc[...])
        out = acc_sc[0:N_MAPS, :] / l_col[0:N_MAPS]
        for h in range(DIFF_HEADS):
            d = out[2 * h:2 * h + 1] - lam * out[2 * h + 1:2 * h + 2]
            sl = slice(h * DIFF_DV, (h + 1) * DIFF_DV)
            o_ref[0, :, sl] = _diff_out_norm(d, sub_ref[...], lam_init).astype(o_ref.dtype)


def _attn_sample(q, k_new, v_new, cache_k, cache_v, layer, page_table, bias_tab, lam_q, lam_k, subln,
                 *, lam_init):
    b, n_pages = page_table.shape
    assert PAGE_SIZE >= BIAS_SPAN
    hw = 2 * DIFF_DQK
    b_far = jnp.broadcast_to(bias_tab[:, BIAS_SPAN:], (DIFF_HEADS, LANES))
    last = bias_tab[:, jnp.clip(PAGE_SIZE - jnp.arange(PAGE_SIZE), 0, BIAS_SPAN)]
    b_last = jnp.broadcast_to(last.T.reshape(PAGE_ROWS, 1), (PAGE_ROWS, LANES))
    new = jnp.full((NEW_TOKENS, DIFF_HEADS), NEG, F32).at[0].set(bias_tab[:, 0])
    b_new = jnp.broadcast_to(new.reshape(NEW_TOKENS * DIFF_HEADS, 1), (NEW_TOKENS * DIFF_HEADS, LANES))
    pad_page = lambda a: jnp.pad(a.reshape(b, DIFF_HEADS, hw), ((0, 0), (0, (NEW_TOKENS - 1) * DIFF_HEADS), (0, 0)))
    new_spec = pl.BlockSpec((1, NEW_TOKENS * DIFF_HEADS, hw), lambda i, p, pt: (i, 0, 0))
    page = pl.BlockSpec((None, None, PAGE_ROWS, hw), lambda i, p, pt: (layer, pt[i * n_pages + p], 0, 0))
    const = lambda shape: pl.BlockSpec(shape, lambda i, p, pt: (0,) * len(shape))
    out = pl.pallas_call(
        functools.partial(_attn_sample_kernel, n_pages=n_pages, lam_init=lam_init),
        grid_spec=pltpu.PrefetchScalarGridSpec(
            num_scalar_prefetch=1,
            grid=(b, n_pages),
            in_specs=[pl.BlockSpec((1, N_MAPS, DIFF_DQK), lambda i, p, pt: (i, 0, 0)),
                      new_spec, new_spec, page, page,
                      const((DIFF_HEADS, LANES)), const((PAGE_ROWS, LANES)),
                      const((NEW_TOKENS * DIFF_HEADS, LANES)),
                      const((2, DIFF_DQK)), const((2, DIFF_DQK)), const((1, DIFF_DV))],
            out_specs=pl.BlockSpec((1, 1, DIFF_V_W), lambda i, p, pt: (i, 0, 0)),
            scratch_shapes=[pltpu.VMEM((LANES, hw), BF16),
                            pltpu.VMEM((DIFF_HEADS, LANES), F32), pltpu.VMEM((DIFF_HEADS, LANES), F32),
                            pltpu.VMEM((LANES, DIFF_DV), F32)]),
        out_shape=jax.ShapeDtypeStruct((b, 1, DIFF_V_W), F32),
        compiler_params=_cparams(("parallel", "arbitrary")),
        name="attn_sample",
    )(page_table.reshape(-1), q.reshape(b, N_MAPS, DIFF_DQK), pad_page(k_new), pad_page(v_new),
      cache_k, cache_v, b_far, b_last, b_new, lam_q, lam_k, subln.reshape(1, DIFF_DV))
    return out.reshape(b, DIFF_V_W)


HIST = 16


def _window_sum(ext, w, n):
    s = ext
    d = 1
    while d < w:
        s = s + pltpu.roll(s, d, 0)
        d *= 2
    return s[HIST:HIST + n]


def _pool_prompt_kernel(u_ref, w_ref, sc_ref, o_ref, hist_out, hist_sc, *, tm):
    t = pl.program_id(1)

    @pl.when(t == 0)
    def _():
        hist_sc[...] = jnp.zeros(hist_sc.shape, F32)

    pos = t * tm + lax.broadcasted_iota(jnp.int32, (tm, 1), 0)
    for gi, w in enumerate(POOL_WINDOWS):
        sl = slice(gi * POOL_GC, (gi + 1) * POOL_GC)
        u = u_ref[:, sl]
        ext = jnp.concatenate([hist_sc[:, sl], u], axis=0)
        cnt = jnp.minimum(w, pos + 1).astype(F32)
        pooled = _window_sum(ext, w, tm) / cnt - u
        mixed = jnp.dot(pooled.astype(BF16), w_ref[gi], preferred_element_type=F32)
        o_ref[:, sl] = (mixed * sc_ref[:, sl]).astype(o_ref.dtype)
        new_hist = u[tm - HIST:]
        hist_sc[:, sl] = new_hist
        hist_out[0, :, sl] = new_hist


def _pool_prompt(u, pool_w, pool_scale, *, batch, tm):
    m = u.shape[0]
    seq = m // batch
    tm = min(tm, seq)
    assert tm >= HIST
    tps = seq // tm
    hp, hist = pl.pallas_call(
        functools.partial(_pool_prompt_kernel, tm=tm),
        grid=(batch, tps),
        in_specs=[pl.BlockSpec((tm, POOL_CH), lambda b, t: (b * tps + t, 0)),
                  pl.BlockSpec((POOL_GROUPS, POOL_GC, POOL_GC), lambda b, t: (0, 0, 0)),
                  pl.BlockSpec((1, POOL_CH), lambda b, t: (0, 0))],
        out_specs=[pl.BlockSpec((tm, POOL_CH), lambda b, t: (b * tps + t, 0)),
                   pl.BlockSpec((1, HIST, POOL_CH), lambda b, t: (b, 0, 0))],
        out_shape=[jax.ShapeDtypeStruct((m, POOL_CH), BF16),
                   jax.ShapeDtypeStruct((batch, HIST, POOL_CH), F32)],
        scratch_shapes=[pltpu.VMEM((HIST, POOL_CH), F32)],
        compiler_params=_cparams(("parallel", "arbitrary")),
        name="pool_prompt",
    )(u, pool_w, pool_scale.reshape(1, POOL_CH))
    return hp, hist[:, HIST - POOL_BUF:]


def _pool_sample_kernel(ucat_ref, w_ref, sc_ref, o_ref):
    for gi, w in enumerate(POOL_WINDOWS):
        sl = slice(gi * POOL_GC, (gi + 1) * POOL_GC)
        new = ucat_ref[HIST - 1, :, sl]
        win = new
        for r in range(HIST - w, HIST - 1):
            win = win + ucat_ref[r, :, sl]
        pooled = win / float(w) - new
        mixed = jnp.dot(pooled.astype(BF16), w_ref[gi], preferred_element_type=F32)
        o_ref[:, sl] = (mixed * sc_ref[:, sl]).astype(o_ref.dtype)


def _pool_sample(u, buf, pool_w, pool_scale):
    b = u.shape[0]
    assert buf.shape[1] == HIST - 1
    ucat = jnp.concatenate([buf, u[:, None, :]], axis=1)
    hp = pl.pallas_call(
        _pool_sample_kernel,
        out_shape=jax.ShapeDtypeStruct((b, POOL_CH), F32),
        compiler_params=pltpu.CompilerParams(vmem_limit_bytes=VMEM_LIMIT),
        name="pool_sample",
    )(jnp.transpose(ucat, (1, 0, 2)), pool_w, pool_scale.reshape(1, POOL_CH))
    return hp, ucat[:, 1:]


def _prep_layer(p, l):
    w_in, b_in = p["w_in"][l], p["b_in"][l]
    splits = (MLSTM_QK_W, MLSTM_QK_W, MLSTM_V_W, 2 * MLSTM_HEADS, MLSTM_V_W,
              DIFF_QK_W, DIFF_QK_W, DIFF_V_W, POOL_CH, N_BRANCH * D_MODEL)
    offs = [0]
    for s in splits:
        offs.append(offs[-1] + s)

    def seg(a, b, pad_to=None):
        w = w_in[:, offs[a]:offs[b]].astype(BF16)
        bias = b_in[offs[a]:offs[b]].reshape(1, -1)
        if pad_to is not None:
            w = jnp.pad(w, ((0, 0), (0, pad_to - w.shape[1])))
            bias = jnp.pad(bias, ((0, 0), (0, pad_to - bias.shape[1])))
        return w, bias

    return dict(
        seg_mqkv=seg(0, 3), seg_mif=seg(3, 4, LANES), seg_mo=seg(4, 5), seg_dq=seg(5, 6),
        seg_dk=seg(6, 7), seg_dv=seg(7, 8), seg_pu=seg(8, 9), seg_gates=seg(9, 10),
        q_norm=jnp.tile(p["q_norm"][l].reshape(1, -1), (1, DIFF_HEADS)),
        k_norm=jnp.tile(p["k_norm"][l].reshape(1, -1), (1, DIFF_HEADS)),
        norm_mix=p["norm_mix"][l], norm_ffn=p["norm_ffn"][l], mlstm_norm=p["mlstm_norm"][l],
        lambda_q=p["lambda_q"][l], lambda_k=p["lambda_k"][l], diff_subln=p["diff_subln"][l],
        pool_w=p["pool_w"][l].astype(BF16), pool_scale=p["pool_scale"][l],
        w_br_mlstm=p["w_br_mlstm"][l].astype(BF16), w_br_diff=p["w_br_diff"][l].astype(BF16),
        w_br_pool=p["w_br_pool"][l].astype(BF16), w_out=p["w_out"][l].astype(BF16),
        w_up=p["w_up"][l].astype(BF16), conv_w=p["conv_w"][l], conv_b=p["conv_b"][l],
        w_down=p["w_down"][l].astype(BF16),
    )


def _projections(h, lw, *, tm, tn):
    mm = functools.partial(_matmul, tm=tm, tn=tn)
    out = {}
    (out["mqkv"],) = mm(h, *_unpack(lw["seg_mqkv"]), _ep_bias, (F32,), name="proj_mqkv")
    (out["mif"],) = mm(h, *_unpack(lw["seg_mif"]), _ep_bias, (F32,), name="proj_mif")
    (out["mo"],) = mm(h, *_unpack(lw["seg_mo"]), _ep_bias, (F32,), name="proj_mo")
    wq, bq = lw["seg_dq"]
    out["dq"], out["dq16"] = mm(h, wq, (bq, lw["q_norm"]), _ep_qknorm, (F32, BF16), name="proj_dq")
    wk, bk = lw["seg_dk"]
    out["dk"], out["dk16"] = mm(h, wk, (bk, lw["k_norm"]), _ep_qknorm, (F32, BF16), name="proj_dk")
    out["dv"], out["dv16"] = mm(h, *_unpack(lw["seg_dv"]), _ep_bias_bf16copy, (F32, BF16), name="proj_dv")
    (out["pu"],) = mm(h, *_unpack(lw["seg_pu"]), _ep_bias, (F32,), name="proj_pu")
    (out["gates"],) = mm(h, *_unpack(lw["seg_gates"]), _ep_bias_sigmoid, (F32,), name="proj_gates")
    return out


def _unpack(seg):
    w, b = seg
    return w, (b,)


def _layer_prompt(x, lw, bias_tab, layer_idx, *, batch):
    m = x.shape[0]
    lam_init = 0.8 - 0.6 * math.exp(-0.3 * layer_idx)
    h = _rmsnorm(x, lw["norm_mix"], tm=512)
    pr = _projections(h, lw, tm=1024, tn=512)
    hm, c1, n1, m1 = _mlstm_prompt(pr["mqkv"], pr["mif"], pr["mo"], lw["mlstm_norm"], batch=batch)
    hd = _attn_prompt(pr["dq16"], pr["dk16"], pr["dv16"], bias_tab, lw["lambda_q"], lw["lambda_k"],
                      lw["diff_subln"], batch=batch, lam_init=lam_init, tq=512)
    hp, pool_new = _pool_prompt(pr["pu"], lw["pool_w"], lw["pool_scale"], batch=batch, tm=512)
    merged = _merge(hm, hd, hp, lw["w_br_mlstm"], lw["w_br_diff"], lw["w_br_pool"], pr["gates"],
                    tm=1024, tn=512)
    (x1,) = _matmul(merged, lw["w_out"], (x,), _ep_residual, (F32,), tm=1024, tn=512, name="out_proj")
    h2 = _rmsnorm(x1, lw["norm_ffn"], tm=512)
    act, conv_new = _ffn_up_prompt(h2, lw["w_up"], lw["conv_w"], lw["conv_b"], batch=batch, tm=1024, tn=256)
    (x2,) = _matmul(act, lw["w_down"], (x1,), _ep_residual, (F32,), tm=512, tn=256, name="ffn_down")
    seq = m // batch
    kv_shape = (batch, seq, DIFF_HEADS, DIFF_DV)
    return x2, pr["dk"].reshape(kv_shape), pr["dv"].reshape(kv_shape), c1, n1, m1, pool_new, conv_new


def _layer_sample(x, lw, bias_tab, layer_idx, cache_k, cache_v, page_table, c0, n0, m0, pool_buf, conv_buf):
    b = x.shape[0]
    lam_init = 0.8 - 0.6 * math.exp(-0.3 * layer_idx)
    h = _rmsnorm(x, lw["norm_mix"], tm=b)
    pr = _projections(h, lw, tm=b, tn=512)
    hm, c1, n1, m1 = _mlstm_sample(pr["mqkv"], pr["mif"], pr["mo"], lw["mlstm_norm"], c0, n0, m0)
    hd = _attn_sample(pr["dq"], pr["dk"], pr["dv"], cache_k, cache_v, layer_idx, page_table, bias_tab,
                      lw["lambda_q"], lw["lambda_k"], lw["diff_subln"], lam_init=lam_init)
    hp, pool_new = _pool_sample(pr["pu"], pool_buf, lw["pool_w"], lw["pool_scale"])
    merged = _merge(hm, hd, hp, lw["w_br_mlstm"], lw["w_br_diff"], lw["w_br_pool"], pr["gates"],
                    tm=b, tn=512)
    (x1,) = _matmul(merged, lw["w_out"], (x,), _ep_residual, (F32,), tm=b, tn=512, name="out_proj_s")
    h2 = _rmsnorm(x1, lw["norm_ffn"], tm=b)
    act, conv_new = _ffn_up_sample(h2, lw["w_up"], lw["conv_w"], lw["conv_b"], conv_buf, tn=256)
    (x2,) = _matmul(act, lw["w_down"], (x1,), _ep_residual, (F32,), tm=b, tn=512, name="ffn_down_s")
    kv_shape = (b, 1, DIFF_HEADS, DIFF_DV)
    return x2, pr["dk"].reshape(kv_shape), pr["dv"].reshape(kv_shape), c1, n1, m1, pool_new, conv_new


def kernel(x_prompt, x_sample, cache_k, cache_v, page_table, state_mlstm_c, state_mlstm_n, state_mlstm_m,
           state_pool, state_conv, rel_bias, norm_mix, w_in, b_in, mlstm_norm, q_norm, k_norm, lambda_q,
           lambda_k, diff_subln, pool_w, pool_scale, w_br_mlstm, w_br_diff, w_br_pool, w_out, norm_ffn,
           w_up, conv_w, conv_b, w_down):
    params = dict(norm_mix=norm_mix, w_in=w_in, b_in=b_in, mlstm_norm=mlstm_norm, q_norm=q_norm,
                  k_norm=k_norm, lambda_q=lambda_q, lambda_k=lambda_k, diff_subln=diff_subln,
                  pool_w=pool_w, pool_scale=pool_scale, w_br_mlstm=w_br_mlstm, w_br_diff=w_br_diff,
                  w_br_pool=w_br_pool, w_out=w_out, norm_ffn=norm_ffn, w_up=w_up, conv_w=conv_w,
                  conv_b=conv_b, w_down=w_down)
    depth = w_in.shape[0]
    n_prompt, seq, d = x_prompt.shape
    n_dec = x_sample.shape[0]
    assert x_sample.shape[1] == 1, "sample group decodes one token per sequence"
    bias_tab = _bias_by_distance(rel_bias)
    ck = cache_k.reshape(cache_k.shape[:2] + (PAGE_ROWS, 2 * DIFF_DQK))
    cv = cache_v.reshape(cache_v.shape[:2] + (PAGE_ROWS, DIFF_DV))
    xp = x_prompt.reshape(n_prompt * seq, d)
    xs = x_sample.reshape(n_dec, d)
    outs_p, outs_s = [], []
    for l in range(depth):
        lw = _prep_layer(params, l)
        xp, *rest_p = _layer_prompt(xp, lw, bias_tab, l, batch=n_prompt)
        xs, *rest_s = _layer_sample(xs, lw, bias_tab, l, ck, cv, page_table, state_mlstm_c[l],
                                    state_mlstm_n[l], state_mlstm_m[l], state_pool[l], state_conv[l])
        outs_p.append(rest_p)
        outs_s.append(rest_s)
    stack = lambda outs, idx: jnp.stack([o[idx] for o in outs])
    kp, vp, cp, np_, mp, pp, cvp = (stack(outs_p, i) for i in range(7))
    ks, vs, cs_, ns, ms, ps, cvs = (stack(outs_s, i) for i in range(7))
    return (xp.reshape(n_prompt, seq, d), xs.reshape(n_dec, 1, d),
            kp, vp, ks, vs, cp, np_, mp, cs_, ns, ms, pp, ps, cvp, cvs)
```

```python
import functools
import math

import jax
import jax.numpy as jnp
import numpy as np
from jax import lax
from jax.experimental import pallas as pl
from jax.experimental.pallas import tpu as pltpu

F32 = jnp.float32
BF16 = jnp.bfloat16

D_MODEL = 4096
PAGE_SIZE = 128
MLSTM_HEADS = 4
MLSTM_DV = 256
MLSTM_DQK = 128
MLSTM_CHUNK = 64
DIFF_HEADS = 8
DIFF_DV = 256
DIFF_DQK = 128
POOL_GROUPS = 4
POOL_WINDOWS = (2, 4, 8, 16)
POOL_GC = 256
POOL_CH = POOL_GROUPS * POOL_GC
POOL_BUF = 15
N_BRANCH = 3
D_FF = 11008
CONV_W = 3
REL_BUCKETS = 32
REL_MAX_DIST = 128
RMS_EPS = 1e-6

MLSTM_QK_W = MLSTM_HEADS * MLSTM_DQK
MLSTM_V_W = MLSTM_HEADS * MLSTM_DV
DIFF_QK_W = DIFF_HEADS * 2 * DIFF_DQK
DIFF_V_W = DIFF_HEADS * DIFF_DV

LANES = 128
SUBLANES = 8
VMEM_LIMIT = 56 * 1024 * 1024
NEG = -1e30
ROW_CHUNK = 512

BIAS_SPAN = 128


def _cparams(sem):
    return pltpu.CompilerParams(dimension_semantics=sem, vmem_limit_bytes=VMEM_LIMIT)


def _rmsnorm_kernel(x_ref, g_ref, o_ref):
    x = x_ref[...]
    y = x * lax.rsqrt(jnp.mean(x * x, axis=-1, keepdims=True) + RMS_EPS)
    o_ref[...] = (y * g_ref[...]).astype(o_ref.dtype)


def _rmsnorm(x, g, tm):
    m, d = x.shape
    return pl.pallas_call(
        _rmsnorm_kernel,
        grid=(m // tm,),
        in_specs=[pl.BlockSpec((tm, d), lambda i: (i, 0)),
                  pl.BlockSpec((1, d), lambda i: (0, 0))],
        out_specs=pl.BlockSpec((tm, d), lambda i: (i, 0)),
        out_shape=jax.ShapeDtypeStruct((m, d), BF16),
        compiler_params=_cparams(("parallel",)),
        name="rmsnorm",
    )(x, g.reshape(1, d))


def _group_rmsnorm(y, g, width):
    outs = []
    for c in range(0, y.shape[1], width):
        yc = y[:, c:c + width]
        yn = yc * lax.rsqrt(jnp.mean(yc * yc, axis=-1, keepdims=True) + RMS_EPS)
        outs.append(yn * g[:, c:c + width])
    return outs[0] if len(outs) == 1 else jnp.concatenate(outs, axis=1)


def _ep_bias(acc, b):
    return (acc + b,)


def _ep_bias_bf16copy(acc, b):
    y = acc + b
    return (y, y.astype(BF16))


def _ep_bias_sigmoid(acc, b):
    return (jax.nn.sigmoid(acc + b),)


def _ep_knorm(acc, b, g):
    y = _group_rmsnorm(acc + b, g, DIFF_DQK)
    return (y, y.astype(BF16))


def _ep_qnorm_scaled(acc, b, g):
    y = _group_rmsnorm(acc + b, g, DIFF_DQK) * (DIFF_DQK ** -0.5)
    return (y, y.astype(BF16))


def _ep_residual(acc, r):
    return (r + acc,)


def _row_chunks(tm):
    step = ROW_CHUNK if tm % ROW_CHUNK == 0 else tm
    return [slice(r, r + step) for r in range(0, tm, step)]


def _mm_kernel(a_ref, w_ref, *rest, n_extra, n_out, epilogue, tile_extra):
    extras = rest[:n_extra]
    outs = rest[n_extra:n_extra + n_out]
    w = w_ref[...].astype(BF16)
    for rows in _row_chunks(a_ref.shape[0]):
        acc = jnp.dot(a_ref[rows, :].astype(BF16), w, preferred_element_type=F32)
        res = epilogue(acc, *[e[rows, :] if t else e[...] for e, t in zip(extras, tile_extra)])
        for o, r in zip(outs, res):
            o[rows, :] = r.astype(o.dtype)


def _matmul(a, w, extras, epilogue, out_dtypes, *, tm, tn, name, layer=None, col0=0, n=None,
            single_buffer_a=False):
    m, k = a.shape
    n = w.shape[-1] - col0 if n is None else n
    tm = min(tm, m)
    tn = min(tn, n)
    assert m % tm == 0 and n % tn == 0 and col0 % tn == 0, (m, n, tm, tn, col0)
    jb = col0 // tn
    a_kwargs = dict(pipeline_mode=pl.Buffered(1)) if single_buffer_a else {}
    if layer is None:
        w_spec = pl.BlockSpec((k, tn), lambda i, j: (0, jb + j))
    else:
        w_spec = pl.BlockSpec((None, k, tn), lambda i, j: (layer, 0, jb + j))
    in_specs = [pl.BlockSpec((tm, k), lambda i, j: (i, 0), **a_kwargs), w_spec]
    tile_extra = []
    for e in extras:
        tile_extra.append(e.shape[0] != 1)
        if e.shape[0] == 1:
            in_specs.append(pl.BlockSpec((1, tn), lambda i, j: (0, j)))
        else:
            in_specs.append(pl.BlockSpec((tm, tn), lambda i, j: (i, j)))
    out_specs = [pl.BlockSpec((tm, tn), lambda i, j: (i, j)) for _ in out_dtypes]
    out_shape = [jax.ShapeDtypeStruct((m, n), dt) for dt in out_dtypes]
    return pl.pallas_call(
        functools.partial(_mm_kernel, n_extra=len(extras), n_out=len(out_dtypes), epilogue=epilogue,
                          tile_extra=tuple(tile_extra)),
        grid=(m // tm, n // tn),
        in_specs=in_specs,
        out_specs=out_specs,
        out_shape=out_shape,
        compiler_params=_cparams(("parallel", "arbitrary")),
        name=name,
    )(a, w, *extras)


def _merge_kernel(hm_ref, hd_ref, hp_ref, wm_ref, wd_ref, wp_ref, g0_ref, g1_ref, g2_ref, o_ref):
    wm, wd, wp = (w[...].astype(BF16) for w in (wm_ref, wd_ref, wp_ref))
    for rows in _row_chunks(o_ref.shape[0]):
        proj = lambda h_ref, w: jnp.dot(h_ref[rows, :].astype(BF16), w, preferred_element_type=F32)
        merged = (g0_ref[rows, :] * proj(hm_ref, wm)
                  + g1_ref[rows, :] * proj(hd_ref, wd)
                  + g2_ref[rows, :] * proj(hp_ref, wp))
        o_ref[rows, :] = merged.astype(o_ref.dtype)


def _merge(hm, hd, hp, wm, wd, wp, gates, *, layer, tm, tn):
    m = hm.shape[0]
    tm = min(tm, m)
    nj = D_MODEL // tn
    row = lambda width: pl.BlockSpec((tm, width), lambda i, j: (i, 0))
    col = lambda depth: pl.BlockSpec((None, depth, tn), lambda i, j: (layer, 0, j))
    gate = lambda br: pl.BlockSpec((tm, tn), lambda i, j: (i, br * nj + j))
    return pl.pallas_call(
        _merge_kernel,
        grid=(m // tm, nj),
        in_specs=[row(MLSTM_V_W), row(DIFF_V_W), row(POOL_CH),
                  col(MLSTM_V_W), col(DIFF_V_W), col(POOL_CH),
                  gate(0), gate(1), gate(2)],
        out_specs=pl.BlockSpec((tm, tn), lambda i, j: (i, j)),
        out_shape=jax.ShapeDtypeStruct((m, D_MODEL), BF16),
        compiler_params=_cparams(("parallel", "arbitrary")),
        name="merge",
    )(hm, hd, hp, wm, wd, wp, gates, gates, gates)


def _conv_gate(ug, uv, pg, pv, cw_g, cw_v, cb_g, cb_v):
    cg = cb_g + (cw_g[0:1] * pg[0] + cw_g[1:2] * pg[1] + cw_g[2:3] * ug)
    cv = cb_v + (cw_v[0:1] * pv[0] + cw_v[1:2] * pv[1] + cw_v[2:3] * uv)
    return jax.nn.silu(cg) * cv


def _shifted_rows(u, tail):
    ext = jnp.concatenate([tail, u], axis=0)
    n = u.shape[0]
    back1 = pltpu.roll(ext, 1, 0)[SUBLANES:SUBLANES + n]
    back2 = pltpu.roll(ext, 2, 0)[SUBLANES:SUBLANES + n]
    return back2, back1


def _ffn_up_prompt_kernel(a_ref, wg_ref, wv_ref, cwg_ref, cwv_ref, cbg_ref, cbv_ref,
                          act_ref, tail_ref, carry_ref, *, tiles_per_seq):
    i = pl.program_id(0)
    j = pl.program_id(1)

    @pl.when(i % tiles_per_seq == 0)
    def _():
        carry_ref[:, pl.ds(j, 1)] = jnp.zeros((2, 1) + carry_ref.shape[2:], F32)

    wg = wg_ref[...].astype(BF16)
    wv = wv_ref[...].astype(BF16)
    tail_g = carry_ref[0, j]
    tail_v = carry_ref[1, j]
    for rows in _row_chunks(a_ref.shape[0]):
        a = a_ref[rows, :]
        ug = jnp.dot(a, wg, preferred_element_type=F32)
        uv = jnp.dot(a, wv, preferred_element_type=F32)
        act = _conv_gate(ug, uv, _shifted_rows(ug, tail_g), _shifted_rows(uv, tail_v),
                         cwg_ref[...], cwv_ref[...], cbg_ref[...], cbv_ref[...])
        act_ref[rows, :] = act.astype(act_ref.dtype)
        tail_g = ug[ug.shape[0] - SUBLANES:]
        tail_v = uv[uv.shape[0] - SUBLANES:]
    carry_ref[0, j] = tail_g
    carry_ref[1, j] = tail_v
    tail_ref[0, 0] = tail_g
    tail_ref[0, 1] = tail_v


def _ffn_up_prompt(h, w_up, conv_w, conv_b, *, layer, batch, tm, tn):
    m, d = h.shape
    nj = D_FF // tn
    tiles_per_seq = (m // batch) // tm
    conv_b = conv_b.reshape(1, 2 * D_FF)
    act, tail = pl.pallas_call(
        functools.partial(_ffn_up_prompt_kernel, tiles_per_seq=tiles_per_seq),
        grid=(m // tm, nj),
        in_specs=[pl.BlockSpec((tm, d), lambda i, j: (i, 0)),
                  pl.BlockSpec((None, d, tn), lambda i, j: (layer, 0, j)),
                  pl.BlockSpec((None, d, tn), lambda i, j: (layer, 0, nj + j)),
                  pl.BlockSpec((CONV_W, tn), lambda i, j: (0, j)),
                  pl.BlockSpec((CONV_W, tn), lambda i, j: (0, nj + j)),
                  pl.BlockSpec((1, tn), lambda i, j: (0, j)),
                  pl.BlockSpec((1, tn), lambda i, j: (0, nj + j))],
        out_specs=[pl.BlockSpec((tm, tn), lambda i, j: (i, j)),
                   pl.BlockSpec((1, 2, SUBLANES, tn), lambda i, j: (i, 0, 0, j))],
        out_shape=[jax.ShapeDtypeStruct((m, D_FF), BF16),
                   jax.ShapeDtypeStruct((m // tm, 2, SUBLANES, D_FF), F32)],
        scratch_shapes=[pltpu.VMEM((2, nj, SUBLANES, tn), F32)],
        compiler_params=_cparams(("arbitrary", "arbitrary")),
        name="ffn_up_prompt",
    )(h, w_up, w_up, conv_w, conv_w, conv_b, conv_b)
    tail = tail[tiles_per_seq - 1::tiles_per_seq]
    conv_new = jnp.transpose(tail[:, :, SUBLANES - (CONV_W - 1):, :], (0, 2, 1, 3))
    return act, conv_new.reshape(batch, CONV_W - 1, 2 * D_FF)


def _ffn_up_sample_kernel(a_ref, wg_ref, wv_ref, cwg_ref, cwv_ref, cbg_ref, cbv_ref,
                          p2g_ref, p1g_ref, p2v_ref, p1v_ref, act_ref, ug_ref, uv_ref):
    a = a_ref[...].astype(BF16)
    ug = jnp.dot(a, wg_ref[...].astype(BF16), preferred_element_type=F32)
    uv = jnp.dot(a, wv_ref[...].astype(BF16), preferred_element_type=F32)
    act = _conv_gate(ug, uv, (p2g_ref[...], p1g_ref[...]), (p2v_ref[...], p1v_ref[...]),
                     cwg_ref[...], cwv_ref[...], cbg_ref[...], cbv_ref[...])
    act_ref[...] = act.astype(act_ref.dtype)
    ug_ref[...] = ug
    uv_ref[...] = uv


def _ffn_up_sample(h, w_up, conv_w, conv_b, conv_buf, *, layer, tn):
    b, d = h.shape
    nj = D_FF // tn
    conv_b = conv_b.reshape(1, 2 * D_FF)
    prev2 = conv_buf[:, 0, :]
    prev1 = conv_buf[:, 1, :]
    lo = lambda rows: pl.BlockSpec((rows, tn), lambda j: (0, j))
    hi = lambda rows: pl.BlockSpec((rows, tn), lambda j: (0, nj + j))
    act, ug, uv = pl.pallas_call(
        _ffn_up_sample_kernel,
        grid=(nj,),
        in_specs=[pl.BlockSpec((b, d), lambda j: (0, 0)),
                  pl.BlockSpec((None, d, tn), lambda j: (layer, 0, j)),
                  pl.BlockSpec((None, d, tn), lambda j: (layer, 0, nj + j)),
                  lo(CONV_W), hi(CONV_W), lo(1), hi(1),
                  lo(b), lo(b), hi(b), hi(b)],
        out_specs=[lo(b), lo(b), lo(b)],
        out_shape=[jax.ShapeDtypeStruct((b, D_FF), BF16),
                   jax.ShapeDtypeStruct((b, D_FF), F32),
                   jax.ShapeDtypeStruct((b, D_FF), F32)],
        compiler_params=_cparams(("arbitrary",)),
        name="ffn_up_sample",
    )(h, w_up, w_up, conv_w, conv_w, conv_b, conv_b, prev2, prev1, prev2, prev1)
    u = jnp.concatenate([ug, uv], axis=1)
    return act, jnp.concatenate([conv_buf[:, 1:], u[:, None, :]], axis=1)


def _log_sigmoid(x):
    return -(jnp.maximum(-x, 0.0) + jnp.log1p(jnp.exp(-jnp.abs(x))))


def _cumsum_rows(x):
    n = x.shape[0]
    row = lax.broadcasted_iota(jnp.int32, x.shape, 0)
    d = 1
    while d < n:
        x = x + jnp.where(row >= d, pltpu.roll(x, d, 0), 0.0)
        d *= 2
    return x


def _head_out_norm(h, g, o_gate):
    y = h * lax.rsqrt(jnp.mean(h * h, axis=-1, keepdims=True) + RMS_EPS)
    return (y * g) * jax.nn.sigmoid(o_gate)


def _mlstm_prompt_kernel(qkv_ref, mif_ref, mo_ref, nw_ref, hm_ref, c_out, n_out, m_out,
                         c_sc, n_sc, m_sc, *, cs, nc):
    ci = pl.program_id(1)

    @pl.when(ci == 0)
    def _():
        c_sc[...] = jnp.zeros(c_sc.shape, F32)
        n_sc[...] = jnp.zeros(n_sc.shape, F32)
        m_sc[...] = jnp.zeros(m_sc.shape, F32)

    mif = mif_ref[...]
    b_all = _cumsum_rows(_log_sigmoid(mif))
    row = lax.broadcasted_iota(jnp.int32, (cs, cs), 0)
    col = lax.broadcasted_iota(jnp.int32, (cs, cs), 1)
    causal = col <= row
    eye = col == row
    lane = lax.broadcasted_iota(jnp.int32, (1, LANES), 1)
    m_vec = jnp.zeros((1, LANES), F32)

    for h in range(MLSTM_HEADS):
        q = qkv_ref[:, h * MLSTM_DQK:(h + 1) * MLSTM_DQK]
        k = qkv_ref[:, MLSTM_QK_W + h * MLSTM_DQK:MLSTM_QK_W + (h + 1) * MLSTM_DQK] * (MLSTM_DQK ** -0.5)
        v = qkv_ref[:, 2 * MLSTM_QK_W + h * MLSTM_DV:2 * MLSTM_QK_W + (h + 1) * MLSTM_DV]
        li = mif[:, h:h + 1]
        b = b_all[:, MLSTM_HEADS + h:MLSTM_HEADS + h + 1]
        g_row = jnp.sum(jnp.where(eye, li - b, 0.0), axis=0, keepdims=True)
        dmat = jnp.where(causal, b + g_row, NEG)
        m_prev = m_sc[h]
        inter = b + m_prev
        m_t = jnp.maximum(inter, jnp.max(dmat, axis=-1, keepdims=True))
        w_inter = jnp.exp(inter - m_t)
        qb, kb, vb = q.astype(BF16), k.astype(BF16), v.astype(BF16)
        s = lax.dot_general(qb, kb, (((1,), (1,)), ((), ())), preferred_element_type=F32)
        sc = s * jnp.exp(dmat - m_t)
        c_prev = c_sc[h]
        n_prev = n_sc[h]
        num = (w_inter * jnp.dot(qb, c_prev.astype(BF16), preferred_element_type=F32)
               + jnp.dot(sc.astype(BF16), vb, preferred_element_type=F32))
        den = w_inter * jnp.sum(q * n_prev, axis=-1, keepdims=True) + jnp.sum(sc, axis=-1, keepdims=True)
        hh = num / jnp.maximum(jnp.abs(den), jnp.exp(-m_t))
        m_new = m_t[cs - 1:cs]
        b_last = b[cs - 1:cs]
        g_state = jnp.exp(b_last + m_prev - m_new)
        g_tok = jnp.exp(b_last - b + li - m_new)
        kg = k * g_tok
        c_new = g_state * c_prev + lax.dot_general(kg.astype(BF16), vb, (((0,), (0,)), ((), ())),
                                                   preferred_element_type=F32)
        n_new = g_state * n_prev + jnp.sum(kg, axis=0, keepdims=True)
        c_sc[h] = c_new
        n_sc[h] = n_new
        m_sc[h] = m_new
        m_vec = m_vec + jnp.where(lane == h, m_new, 0.0)
        sl = slice(h * MLSTM_DV, (h + 1) * MLSTM_DV)
        hm_ref[:, sl] = _head_out_norm(hh, nw_ref[:, sl], mo_ref[:, sl]).astype(hm_ref.dtype)

    @pl.when(ci == nc - 1)
    def _():
        c_out[0] = c_sc[...]
        n_out[0] = n_sc[...][:, 0, :]
        m_out[0] = m_vec


def _mlstm_prompt(mqkv, mif, mo, norm_w, *, batch):
    m = mqkv.shape[0]
    seq = m // batch
    cs = MLSTM_CHUNK if seq % MLSTM_CHUNK == 0 else seq
    nc = seq // cs
    rows = lambda width: pl.BlockSpec((cs, width), lambda b, c: (b * nc + c, 0))
    hm, c, n, mm = pl.pallas_call(
        functools.partial(_mlstm_prompt_kernel, cs=cs, nc=nc),
        grid=(batch, nc),
        in_specs=[rows(mqkv.shape[1]), rows(LANES), rows(MLSTM_V_W),
                  pl.BlockSpec((1, MLSTM_V_W), lambda b, c: (0, 0))],
        out_specs=[rows(MLSTM_V_W),
                   pl.BlockSpec((1, MLSTM_HEADS, MLSTM_DQK, MLSTM_DV), lambda b, c: (b, 0, 0, 0)),
                   pl.BlockSpec((1, MLSTM_HEADS, MLSTM_DQK), lambda b, c: (b, 0, 0)),
                   pl.BlockSpec((1, 1, LANES), lambda b, c: (b, 0, 0))],
        out_shape=[jax.ShapeDtypeStruct((m, MLSTM_V_W), BF16),
                   jax.ShapeDtypeStruct((batch, MLSTM_HEADS, MLSTM_DQK, MLSTM_DV), F32),
                   jax.ShapeDtypeStruct((batch, MLSTM_HEADS, MLSTM_DQK), F32),
                   jax.ShapeDtypeStruct((batch, 1, LANES), F32)],
        scratch_shapes=[pltpu.VMEM((MLSTM_HEADS, MLSTM_DQK, MLSTM_DV), F32),
                        pltpu.VMEM((MLSTM_HEADS, 1, MLSTM_DQK), F32),
                        pltpu.VMEM((MLSTM_HEADS, 1, 1), F32)],
        compiler_params=_cparams(("parallel", "arbitrary")),
        name="mlstm_prompt",
    )(mqkv, mif, mo, norm_w.reshape(1, MLSTM_V_W))
    return hm, c, n, mm[:, 0, :MLSTM_HEADS]


def _lanes_to_rows(x_row, eye):
    return jnp.sum(jnp.where(eye, x_row, 0.0), axis=1, keepdims=True)


def _mlstm_sample_kernel(qkv_ref, mif_ref, mo_ref, nw_ref, c_ref, n_ref, m_ref,
                         hm_ref, c_out, n_out, m_out):
    mif = mif_ref[0]
    lf_all = _log_sigmoid(mif)
    m_all = m_ref[0]
    row = lax.broadcasted_iota(jnp.int32, (MLSTM_DQK, MLSTM_DQK), 0)
    col = lax.broadcasted_iota(jnp.int32, (MLSTM_DQK, MLSTM_DQK), 1)
    eye = row == col
    lane = lax.broadcasted_iota(jnp.int32, (1, LANES), 1)
    m_vec = jnp.zeros((1, LANES), F32)
    for h in range(MLSTM_HEADS):
        q = qkv_ref[0, :, h * MLSTM_DQK:(h + 1) * MLSTM_DQK]
        k = qkv_ref[0, :, MLSTM_QK_W + h * MLSTM_DQK:MLSTM_QK_W + (h + 1) * MLSTM_DQK] * (MLSTM_DQK ** -0.5)
        v = qkv_ref[0, :, 2 * MLSTM_QK_W + h * MLSTM_DV:2 * MLSTM_QK_W + (h + 1) * MLSTM_DV]
        li = mif[:, h:h + 1]
        lf = lf_all[:, MLSTM_HEADS + h:MLSTM_HEADS + h + 1]
        m_prev = m_all[:, h:h + 1]
        inter = lf + m_prev
        m_t = jnp.maximum(inter, li)
        w_inter = jnp.exp(inter - m_t)
        sc = jnp.sum(q * k, axis=-1, keepdims=True) * jnp.exp(li - m_t)
        c_prev = c_ref[0, h]
        n_prev = n_ref[0, h:h + 1, :]
        q_col = _lanes_to_rows(q, eye)
        k_col = _lanes_to_rows(k, eye)
        num = w_inter * jnp.sum(q_col * c_prev, axis=0, keepdims=True) + sc * v
        den = w_inter * jnp.sum(q * n_prev, axis=-1, keepdims=True) + sc
        hh = num / jnp.maximum(jnp.abs(den), jnp.exp(-m_t))
        g_tok = jnp.exp(li - m_t)
        c_out[0, h] = w_inter * c_prev + (g_tok * k_col) * v
        n_out[0, h:h + 1, :] = w_inter * n_prev + g_tok * k
        m_vec = m_vec + jnp.where(lane == h, m_t, 0.0)
        sl = slice(h * MLSTM_DV, (h + 1) * MLSTM_DV)
        hm_ref[0, :, sl] = _head_out_norm(hh, nw_ref[:, sl], mo_ref[0, :, sl]).astype(hm_ref.dtype)
    m_out[0] = m_vec


def _mlstm_sample(mqkv, mif, mo, norm_w, c0, n0, m0):
    b = mqkv.shape[0]
    m0p = jnp.pad(m0, ((0, 0), (0, LANES - MLSTM_HEADS))).reshape(b, 1, LANES)
    per_seq = lambda width: pl.BlockSpec((1, 1, width), lambda i: (i, 0, 0))
    c_spec = pl.BlockSpec((1, MLSTM_HEADS, MLSTM_DQK, MLSTM_DV), lambda i: (i, 0, 0, 0))
    n_spec = pl.BlockSpec((1, MLSTM_HEADS, MLSTM_DQK), lambda i: (i, 0, 0))
    hm, c, n, mm = pl.pallas_call(
        _mlstm_sample_kernel,
        grid=(b,),
        in_specs=[per_seq(mqkv.shape[1]), per_seq(LANES), per_seq(MLSTM_V_W),
                  pl.BlockSpec((1, MLSTM_V_W), lambda i: (0, 0)),
                  c_spec, n_spec, per_seq(LANES)],
        out_specs=[per_seq(MLSTM_V_W), c_spec, n_spec, per_seq(LANES)],
        out_shape=[jax.ShapeDtypeStruct((b, 1, MLSTM_V_W), F32),
                   jax.ShapeDtypeStruct(c0.shape, F32),
                   jax.ShapeDtypeStruct(n0.shape, F32),
                   jax.ShapeDtypeStruct((b, 1, LANES), F32)],
        compiler_params=_cparams(("parallel",)),
        name="mlstm_sample",
    )(mqkv.reshape(b, 1, -1), mif.reshape(b, 1, LANES), mo.reshape(b, 1, -1),
      norm_w.reshape(1, MLSTM_V_W), c0, n0, m0p)
    return hm.reshape(b, MLSTM_V_W), c, n, mm[:, 0, :MLSTM_HEADS]


def _bucket_by_distance():
    max_exact = REL_BUCKETS // 2
    n = np.arange(BIAS_SPAN + 1)
    large = max_exact + np.floor(np.log(np.maximum(n, 1) / max_exact) / math.log(REL_MAX_DIST / max_exact)
                                 * (REL_BUCKETS - max_exact)).astype(np.int64)
    return np.where(n < max_exact, n, np.minimum(large, REL_BUCKETS - 1))


def _bucket_starts():
    bucket = _bucket_by_distance()
    return [int(np.argmax(bucket >= b)) for b in range(1, REL_BUCKETS)]


def _bias_by_distance(rel_bias):
    tab = jnp.transpose(rel_bias.astype(F32)[_bucket_by_distance()], (1, 0))
    return tab - tab[:, BIAS_SPAN:]


def _lambda(lq_ref, lk_ref, lam_init):
    lq = lq_ref[...]
    lk = lk_ref[...]
    e0 = jnp.exp(jnp.sum(lq[0:1] * lk[0:1], axis=-1, keepdims=True))
    e1 = jnp.exp(jnp.sum(lq[1:2] * lk[1:2], axis=-1, keepdims=True))
    return e0 - e1 + lam_init


def _diff_out_norm(d, sub, lam_init):
    y = d * lax.rsqrt(jnp.mean(d * d, axis=-1, keepdims=True) + RMS_EPS)
    return (y * sub) * (1.0 - lam_init)


def _bias_tiles_kernel(rb_ref, o_ref, *, tq):
    h = pl.program_id(0)
    blk = pl.program_id(1)
    row = lax.broadcasted_iota(jnp.int32, (tq, tq), 0)
    col = lax.broadcasted_iota(jnp.int32, (tq, tq), 1)
    dist = row - col + blk * tq
    far = rb_ref[h, REL_BUCKETS - 1]
    val = jnp.full((tq, tq), rb_ref[h, 0] - far, F32)
    for b, start in enumerate(_bucket_starts(), start=1):
        val = jnp.where(dist >= start, rb_ref[h, b] - far, val)
    o_ref[0, 0] = jnp.where(dist >= 0, val, NEG)


def _bias_tiles(rel_bias, tq):
    assert tq + 1 >= _bucket_starts()[-1]
    return pl.pallas_call(
        functools.partial(_bias_tiles_kernel, tq=tq),
        grid=(DIFF_HEADS, 2),
        in_specs=[pl.BlockSpec(memory_space=pltpu.SMEM)],
        out_specs=pl.BlockSpec((1, 1, tq, tq), lambda h, blk: (h, blk, 0, 0)),
        out_shape=jax.ShapeDtypeStruct((DIFF_HEADS, 2, tq, tq), F32),
        compiler_params=_cparams(("parallel", "parallel")),
        name="bias_tiles",
    )(jnp.transpose(rel_bias.astype(F32)))


def _attn_prompt_kernel(q_ref, k_ref, v_ref, bias_ref, lq_ref, lk_ref, sub_ref, o_ref,
                        m_sc, l_sc, acc_sc, *, lam_init):
    qi = pl.program_id(2)
    ki = pl.program_id(3)

    @pl.when(ki == 0)
    def _():
        m_sc[...] = jnp.full(m_sc.shape, NEG, F32)
        l_sc[...] = jnp.zeros(l_sc.shape, F32)
        acc_sc[...] = jnp.zeros(acc_sc.shape, F32)

    def tile(bias):
        v = v_ref[0]
        for mp in range(2):
            sl = slice(mp * DIFF_DQK, (mp + 1) * DIFF_DQK)
            s = lax.dot_general(q_ref[0, :, sl], k_ref[0, :, sl], (((1,), (1,)), ((), ())),
                                preferred_element_type=F32)
            if bias is not None:
                s = s + bias
            m_prev = m_sc[mp]
            m_new = jnp.maximum(m_prev, jnp.max(s, axis=-1, keepdims=True))
            alpha = jnp.exp(m_prev - m_new)
            p = jnp.exp(s - m_new)
            l_sc[mp] = alpha * l_sc[mp] + jnp.sum(p, axis=-1, keepdims=True)
            acc_sc[mp] = alpha * acc_sc[mp] + jnp.dot(p.astype(BF16), v, preferred_element_type=F32)
            m_sc[mp] = m_new

    @pl.when(ki < qi - 1)
    def _():
        tile(None)

    @pl.when((ki >= qi - 1) & (ki <= qi))
    def _():
        tile(bias_ref[0, 0])

    @pl.when(ki == qi)
    def _():
        lam = _lambda(lq_ref, lk_ref, lam_init)
        d = acc_sc[0] / l_sc[0] - lam * (acc_sc[1] / l_sc[1])
        o_ref[0] = _diff_out_norm(d, sub_ref[...], lam_init).astype(o_ref.dtype)


def _attn_prompt(q, k, v, bias, lam_q, lam_k, subln, *, batch, lam_init):
    m = q.shape[0]
    seq = m // batch
    tq = bias.shape[-1]
    assert seq % tq == 0
    nq = seq // tq
    hw = 2 * DIFF_DQK
    q3, k3, v3 = (a.reshape(batch, seq, -1) for a in (q, k, v))
    kv_spec = pl.BlockSpec((1, tq, hw), lambda b, h, qi, ki: (b, jnp.minimum(ki, qi), h))
    out = pl.pallas_call(
        functools.partial(_attn_prompt_kernel, lam_init=lam_init),
        grid=(batch, DIFF_HEADS, nq, nq),
        in_specs=[pl.BlockSpec((1, tq, hw), lambda b, h, qi, ki: (b, qi, h)),
                  kv_spec, kv_spec,
                  pl.BlockSpec((1, 1, tq, tq), lambda b, h, qi, ki: (h, jnp.clip(qi - ki, 0, 1), 0, 0)),
                  pl.BlockSpec((2, DIFF_DQK), lambda b, h, qi, ki: (0, 0)),
                  pl.BlockSpec((2, DIFF_DQK), lambda b, h, qi, ki: (0, 0)),
                  pl.BlockSpec((1, DIFF_DV), lambda b, h, qi, ki: (0, 0))],
        out_specs=pl.BlockSpec((1, tq, DIFF_DV), lambda b, h, qi, ki: (b, qi, h)),
        out_shape=jax.ShapeDtypeStruct((batch, seq, DIFF_V_W), BF16),
        scratch_shapes=[pltpu.VMEM((2, tq, 1), F32), pltpu.VMEM((2, tq, 1), F32),
                        pltpu.VMEM((2, tq, DIFF_DV), F32)],
        compiler_params=_cparams(("parallel", "parallel", "parallel", "arbitrary")),
        name="attn_prompt",
    )(q3, k3, v3, bias, lam_q, lam_k, subln.reshape(1, DIFF_DV))
    return out.reshape(m, DIFF_V_W)


N_MAPS = 2 * DIFF_HEADS
PAGE_ROWS = PAGE_SIZE * DIFF_HEADS
NEW_TOKENS = 16
DECODE_PAGES_PER_STEP = 4


def _attn_sample_kernel(pt_ref, qm_ref, kn_ref, vn_ref, *rest, n_steps, pages, lam_init):
    kc_refs, vc_refs = rest[:pages], rest[pages:2 * pages]
    (bfar_ref, blast_ref, bnew_ref, lq_ref, lk_ref, sub_ref, o_ref,
     qx_sc, m_sc, l_sc, acc_sc) = rest[2 * pages:]
    p = pl.program_id(1)
    sub_i = lax.broadcasted_iota(jnp.int32, (DIFF_HEADS, LANES), 0)
    lane_i = lax.broadcasted_iota(jnp.int32, (DIFF_HEADS, LANES), 1)
    own = (lane_i == 2 * sub_i) | (lane_i == 2 * sub_i + 1)

    @pl.when(p == 0)
    def _():
        qm = qm_ref[0]
        r = lax.broadcasted_iota(jnp.int32, qm.shape, 0)
        first = (r & 1) == 0
        qx = jnp.concatenate([jnp.where(first, qm, 0.0), jnp.where(first, 0.0, qm)], axis=1)
        qx_sc[...] = jnp.zeros(qx_sc.shape, BF16)
        qx_sc[0:N_MAPS, :] = qx.astype(BF16)
        m_sc[...] = jnp.where(own, NEG, 0.0)
        l_sc[...] = jnp.zeros(l_sc.shape, F32)
        acc_sc[...] = jnp.zeros(acc_sc.shape, F32)

    row = lax.broadcasted_iota(jnp.int32, (LANES, LANES), 0)
    col = lax.broadcasted_iota(jnp.int32, (LANES, LANES), 1)
    eye = row == col

    def per_lane_column(x):
        x_row = jnp.sum(jnp.where(own, x, 0.0), axis=0, keepdims=True)
        return jnp.sum(jnp.where(eye, x_row, 0.0), axis=1, keepdims=True)

    def group_update(blocks):
        scores = []
        m_prev = m_sc[...]
        m_new = m_prev
        for kf, _, bias3 in blocks:
            s_all = lax.dot_general(kf.astype(BF16), qx_sc[...], (((1,), (1,)), ((), ())),
                                    preferred_element_type=F32)
            s3 = s_all.reshape(kf.shape[0] // DIFF_HEADS, DIFF_HEADS, LANES) + bias3
            m_new = jnp.maximum(m_new, jnp.max(s3, axis=0))
            scores.append(s3)
        alpha = jnp.exp(m_prev - m_new)
        l_new = alpha * l_sc[...]
        pv = None
        for s3, (_, vf, _) in zip(scores, blocks):
            p3 = jnp.exp(s3 - m_new[None])
            l_new = l_new + jnp.sum(p3, axis=0)
            p_t = p3.reshape(vf.shape[0], LANES).T.astype(BF16)
            part = jnp.dot(p_t, vf.astype(BF16), preferred_element_type=F32)
            pv = part if pv is None else pv + part
        l_sc[...] = l_new
        m_sc[...] = m_new
        acc_sc[...] = per_lane_column(alpha) * acc_sc[...] + pv

    far = bfar_ref[...][None]

    @pl.when(p < n_steps - 1)
    def _():
        group_update([(kc[...], vc[...], far) for kc, vc in zip(kc_refs, vc_refs)])

    @pl.when(p == n_steps - 1)
    def _():
        biases = [far] * (pages - 1) + [blast_ref[...].reshape(PAGE_SIZE, DIFF_HEADS, LANES)]
        blocks = [(kc[...], vc[...], b3) for kc, vc, b3 in zip(kc_refs, vc_refs, biases)]
        blocks.append((kn_ref[0], vn_ref[0], bnew_ref[...].reshape(NEW_TOKENS, DIFF_HEADS, LANES)))
        group_update(blocks)
        lam = _lambda(lq_ref, lk_ref, lam_init)
        l_col = per_lane_column(l_sc[...])
        out = acc_sc[0:N_MAPS, :] / l_col[0:N_MAPS]
        for h in range(DIFF_HEADS):
            d = out[2 * h:2 * h + 1] - lam * out[2 * h + 1:2 * h + 2]
            sl = slice(h * DIFF_DV, (h + 1) * DIFF_DV)
            o_ref[0, :, sl] = _diff_out_norm(d, sub_ref[...], lam_init).astype(o_ref.dtype)


def _attn_sample(q, k_new, v_new, cache_k, cache_v, layer, page_table, bias_tab, lam_q, lam_k, subln,
                 *, lam_init):
    b, n_pages = page_table.shape
    assert PAGE_SIZE >= BIAS_SPAN
    pages = DECODE_PAGES_PER_STEP if n_pages % DECODE_PAGES_PER_STEP == 0 else 1
    n_steps = n_pages // pages
    hw = 2 * DIFF_DQK
    sub_i = lax.broadcasted_iota(jnp.int32, (DIFF_HEADS, LANES), 0)
    lane_i = lax.broadcasted_iota(jnp.int32, (DIFF_HEADS, LANES), 1)
    own = (lane_i == 2 * sub_i) | (lane_i == 2 * sub_i + 1)

    def rows_bias(per_token):
        t = per_token.shape[0]
        return jnp.where(own[None], per_token[:, :, None], NEG).reshape(t * DIFF_HEADS, LANES)

    b_far = jnp.where(own, 0.0, NEG)
    b_last = rows_bias(jnp.transpose(bias_tab[:, :0:-1][:, :PAGE_SIZE]))
    b_new = rows_bias(jnp.full((NEW_TOKENS, DIFF_HEADS), NEG, F32).at[0].set(bias_tab[:, 0]))
    pad_page = lambda a: jnp.pad(a.reshape(b, DIFF_HEADS, hw), ((0, 0), (0, (NEW_TOKENS - 1) * DIFF_HEADS), (0, 0)))
    new_spec = pl.BlockSpec((1, NEW_TOKENS * DIFF_HEADS, hw), lambda i, p, pt: (i, 0, 0))
    page = lambda r: pl.BlockSpec((None, None, PAGE_ROWS, hw),
                                  lambda i, p, pt: (layer, pt[i * n_pages + p * pages + r], 0, 0))
    const = lambda shape: pl.BlockSpec(shape, lambda i, p, pt: (0,) * len(shape))
    out = pl.pallas_call(
        functools.partial(_attn_sample_kernel, n_steps=n_steps, pages=pages, lam_init=lam_init),
        grid_spec=pltpu.PrefetchScalarGridSpec(
            num_scalar_prefetch=1,
            grid=(b, n_steps),
            in_specs=[pl.BlockSpec((1, N_MAPS, DIFF_DQK), lambda i, p, pt: (i, 0, 0)),
                      new_spec, new_spec, *[page(r) for r in range(pages)], *[page(r) for r in range(pages)],
                      const((DIFF_HEADS, LANES)), const((PAGE_ROWS, LANES)),
                      const((NEW_TOKENS * DIFF_HEADS, LANES)),
                      const((2, DIFF_DQK)), const((2, DIFF_DQK)), const((1, DIFF_DV))],
            out_specs=pl.BlockSpec((1, 1, DIFF_V_W), lambda i, p, pt: (i, 0, 0)),
            scratch_shapes=[pltpu.VMEM((LANES, hw), BF16),
                            pltpu.VMEM((DIFF_HEADS, LANES), F32), pltpu.VMEM((DIFF_HEADS, LANES), F32),
                            pltpu.VMEM((LANES, DIFF_DV), F32)]),
        out_shape=jax.ShapeDtypeStruct((b, 1, DIFF_V_W), F32),
        compiler_params=_cparams(("parallel", "arbitrary")),
        name="attn_sample",
    )(page_table.reshape(-1), q.reshape(b, N_MAPS, DIFF_DQK), pad_page(k_new), pad_page(v_new),
      *[cache_k] * pages, *[cache_v] * pages, b_far, b_last, b_new, lam_q, lam_k, subln.reshape(1, DIFF_DV))
    return out.reshape(b, DIFF_V_W)


HIST = 16


def _window_sum(ext, w, n):
    s = ext
    d = 1
    while d < w:
        s = s + pltpu.roll(s, d, 0)
        d *= 2
    return s[HIST:HIST + n]


def _pool_prompt_kernel(u_ref, w_ref, sc_ref, o_ref, hist_out, hist_sc, *, tm):
    t = pl.program_id(1)

    @pl.when(t == 0)
    def _():
        hist_sc[...] = jnp.zeros(hist_sc.shape, F32)

    pos = t * tm + lax.broadcasted_iota(jnp.int32, (tm, 1), 0)
    for gi, w in enumerate(POOL_WINDOWS):
        sl = slice(gi * POOL_GC, (gi + 1) * POOL_GC)
        u = u_ref[:, sl]
        ext = jnp.concatenate([hist_sc[:, sl], u], axis=0)
        cnt = jnp.minimum(w, pos + 1).astype(F32)
        pooled = _window_sum(ext, w, tm) / cnt - u
        mixed = jnp.dot(pooled.astype(BF16), w_ref[gi], preferred_element_type=F32)
        o_ref[:, sl] = (mixed * sc_ref[:, sl]).astype(o_ref.dtype)
        new_hist = u[tm - HIST:]
        hist_sc[:, sl] = new_hist
        hist_out[0, :, sl] = new_hist


def _pool_prompt(u, pool_w, pool_scale, *, batch, tm):
    m = u.shape[0]
    seq = m // batch
    tm = min(tm, seq)
    assert tm >= HIST
    tps = seq // tm
    hp, hist = pl.pallas_call(
        functools.partial(_pool_prompt_kernel, tm=tm),
        grid=(batch, tps),
        in_specs=[pl.BlockSpec((tm, POOL_CH), lambda b, t: (b * tps + t, 0)),
                  pl.BlockSpec((POOL_GROUPS, POOL_GC, POOL_GC), lambda b, t: (0, 0, 0)),
                  pl.BlockSpec((1, POOL_CH), lambda b, t: (0, 0))],
        out_specs=[pl.BlockSpec((tm, POOL_CH), lambda b, t: (b * tps + t, 0)),
                   pl.BlockSpec((1, HIST, POOL_CH), lambda b, t: (b, 0, 0))],
        out_shape=[jax.ShapeDtypeStruct((m, POOL_CH), BF16),
                   jax.ShapeDtypeStruct((batch, HIST, POOL_CH), F32)],
        scratch_shapes=[pltpu.VMEM((HIST, POOL_CH), F32)],
        compiler_params=_cparams(("parallel", "arbitrary")),
        name="pool_prompt",
    )(u, pool_w, pool_scale.reshape(1, POOL_CH))
    return hp, hist[:, HIST - POOL_BUF:]


def _pool_sample_kernel(ucat_ref, w_ref, sc_ref, o_ref):
    for gi, w in enumerate(POOL_WINDOWS):
        sl = slice(gi * POOL_GC, (gi + 1) * POOL_GC)
        new = ucat_ref[HIST - 1, :, sl]
        win = new
        for r in range(HIST - w, HIST - 1):
            win = win + ucat_ref[r, :, sl]
        pooled = win / float(w) - new
        mixed = jnp.dot(pooled.astype(BF16), w_ref[gi], preferred_element_type=F32)
        o_ref[:, sl] = (mixed * sc_ref[:, sl]).astype(o_ref.dtype)


def _pool_sample(u, buf, pool_w, pool_scale):
    b = u.shape[0]
    assert buf.shape[1] == HIST - 1
    ucat = jnp.concatenate([buf, u[:, None, :]], axis=1)
    hp = pl.pallas_call(
        _pool_sample_kernel,
        out_shape=jax.ShapeDtypeStruct((b, POOL_CH), F32),
        compiler_params=pltpu.CompilerParams(vmem_limit_bytes=VMEM_LIMIT),
        name="pool_sample",
    )(jnp.transpose(ucat, (1, 0, 2)), pool_w, pool_scale.reshape(1, POOL_CH))
    return hp, ucat[:, 1:]


IN_SPLITS = (MLSTM_QK_W, MLSTM_QK_W, MLSTM_V_W, 2 * MLSTM_HEADS, MLSTM_V_W,
             DIFF_QK_W, DIFF_QK_W, DIFF_V_W, POOL_CH, N_BRANCH * D_MODEL)
IN_OFFS = tuple(int(o) for o in np.cumsum((0,) + IN_SPLITS))
GATE_COLS = (IN_OFFS[3], IN_OFFS[4])


def _prep_layer(p, l):
    w_in, b_in = p["w_in"][l], p["b_in"][l]
    g0, g1 = GATE_COLS
    pad = ((0, 0), (0, LANES - (g1 - g0)))
    return dict(
        w_tail=w_in[:, g1:].astype(BF16), b_tail=b_in[g1:].reshape(1, -1),
        w_mif=jnp.pad(w_in[:, g0:g1].astype(BF16), pad), b_mif=jnp.pad(b_in[g0:g1].reshape(1, -1), pad),
        b_head=b_in[:g0].reshape(1, -1),
        q_norm=jnp.tile(p["q_norm"][l].reshape(1, -1), (1, DIFF_HEADS)),
        k_norm=jnp.tile(p["k_norm"][l].reshape(1, -1), (1, DIFF_HEADS)),
        norm_mix=p["norm_mix"][l], norm_ffn=p["norm_ffn"][l], mlstm_norm=p["mlstm_norm"][l],
        lambda_q=p["lambda_q"][l], lambda_k=p["lambda_k"][l], diff_subln=p["diff_subln"][l],
        pool_w=p["pool_w"][l].astype(BF16), pool_scale=p["pool_scale"][l],
        conv_w=p["conv_w"][l], conv_b=p["conv_b"][l],
    )


def _projections(h, p, lw, l, *, tm, tn):
    g1 = GATE_COLS[1]

    def tail(seg, epilogue, extras, out_dtypes, name):
        c0, c1 = IN_OFFS[seg] - g1, IN_OFFS[seg + 1] - g1
        return _matmul(h, lw["w_tail"], (lw["b_tail"][:, c0:c1],) + extras, epilogue, out_dtypes,
                       tm=tm, tn=tn, col0=c0, n=c1 - c0, name=name)

    out = {}
    (out["mqkv"],) = _matmul(h, p["w_in"], (lw["b_head"],), _ep_bias, (F32,), tm=tm, tn=tn, layer=l,
                             n=GATE_COLS[0], name="proj_mqkv")
    (out["mif"],) = _matmul(h, lw["w_mif"], (lw["b_mif"],), _ep_bias, (F32,), tm=tm, tn=tn, name="proj_mif")
    (out["mo"],) = tail(4, _ep_bias, (), (F32,), "proj_mo")
    out["dq"], out["dq16"] = tail(5, _ep_qnorm_scaled, (lw["q_norm"],), (F32, BF16), "proj_dq")
    out["dk"], out["dk16"] = tail(6, _ep_knorm, (lw["k_norm"],), (F32, BF16), "proj_dk")
    out["dv"], out["dv16"] = tail(7, _ep_bias_bf16copy, (), (F32, BF16), "proj_dv")
    (out["pu"],) = tail(8, _ep_bias, (), (F32,), "proj_pu")
    (out["gates"],) = tail(9, _ep_bias_sigmoid, (), (F32,), "proj_gates")
    return out


ATTN_TQ = 512
TILES_PROMPT = dict(norm=512, proj=(1024, 512), merge=(1024, 256), out=(1024, 512),
                    ffn_up=(1024, 256), ffn_down=(512, 256), pool=512)
TILES_SAMPLE = dict(proj=512, merge=256, out=512, ffn_up=256, ffn_down=256)


def _layer_prompt(x, p, lw, bias_tiles, l, *, batch):
    m = x.shape[0]
    t = TILES_PROMPT
    lam_init = 0.8 - 0.6 * math.exp(-0.3 * l)
    h = _rmsnorm(x, lw["norm_mix"], tm=t["norm"])
    pr = _projections(h, p, lw, l, tm=t["proj"][0], tn=t["proj"][1])
    hm, c1, n1, m1 = _mlstm_prompt(pr["mqkv"], pr["mif"], pr["mo"], lw["mlstm_norm"], batch=batch)
    hd = _attn_prompt(pr["dq16"], pr["dk16"], pr["dv16"], bias_tiles, lw["lambda_q"], lw["lambda_k"],
                      lw["diff_subln"], batch=batch, lam_init=lam_init)
    hp, pool_new = _pool_prompt(pr["pu"], lw["pool_w"], lw["pool_scale"], batch=batch, tm=t["pool"])
    merged = _merge(hm, hd, hp, p["w_br_mlstm"], p["w_br_diff"], p["w_br_pool"], pr["gates"],
                    layer=l, tm=t["merge"][0], tn=t["merge"][1])
    (x1,) = _matmul(merged, p["w_out"], (x,), _ep_residual, (F32,), layer=l, tm=t["out"][0], tn=t["out"][1],
                    name="out_proj")
    h2 = _rmsnorm(x1, lw["norm_ffn"], tm=t["norm"])
    act, conv_new = _ffn_up_prompt(h2, p["w_up"], lw["conv_w"], lw["conv_b"], layer=l, batch=batch,
                                   tm=t["ffn_up"][0], tn=t["ffn_up"][1])
    (x2,) = _matmul(act, p["w_down"], (x1,), _ep_residual, (F32,), layer=l, tm=t["ffn_down"][0],
                    tn=t["ffn_down"][1], single_buffer_a=True, name="ffn_down")
    seq = m // batch
    kv_shape = (batch, seq, DIFF_HEADS, DIFF_DV)
    return x2, pr["dk"].reshape(kv_shape), pr["dv"].reshape(kv_shape), c1, n1, m1, pool_new, conv_new


def _layer_sample(x, p, lw, bias_tab, l, cache_k, cache_v, page_table, c0, n0, m0, pool_buf, conv_buf):
    b = x.shape[0]
    t = TILES_SAMPLE
    lam_init = 0.8 - 0.6 * math.exp(-0.3 * l)
    h = _rmsnorm(x, lw["norm_mix"], tm=b)
    pr = _projections(h, p, lw, l, tm=b, tn=t["proj"])
    hm, c1, n1, m1 = _mlstm_sample(pr["mqkv"], pr["mif"], pr["mo"], lw["mlstm_norm"], c0, n0, m0)
    hd = _attn_sample(pr["dq"], pr["dk"], pr["dv"], cache_k, cache_v, l, page_table, bias_tab,
                      lw["lambda_q"], lw["lambda_k"], lw["diff_subln"], lam_init=lam_init)
    hp, pool_new = _pool_sample(pr["pu"], pool_buf, lw["pool_w"], lw["pool_scale"])
    merged = _merge(hm, hd, hp, p["w_br_mlstm"], p["w_br_diff"], p["w_br_pool"], pr["gates"],
                    layer=l, tm=b, tn=t["merge"])
    (x1,) = _matmul(merged, p["w_out"], (x,), _ep_residual, (F32,), layer=l, tm=b, tn=t["out"],
                    name="out_proj_s")
    h2 = _rmsnorm(x1, lw["norm_ffn"], tm=b)
    act, conv_new = _ffn_up_sample(h2, p["w_up"], lw["conv_w"], lw["conv_b"], conv_buf, layer=l, tn=t["ffn_up"])
    (x2,) = _matmul(act, p["w_down"], (x1,), _ep_residual, (F32,), layer=l, tm=b, tn=t["ffn_down"],
                    name="ffn_down_s")
    kv_shape = (b, 1, DIFF_HEADS, DIFF_DV)
    return x2, pr["dk"].reshape(kv_shape), pr["dv"].reshape(kv_shape), c1, n1, m1, pool_new, conv_new


def kernel(x_prompt, x_sample, cache_k, cache_v, page_table, state_mlstm_c, state_mlstm_n, state_mlstm_m,
           state_pool, state_conv, rel_bias, norm_mix, w_in, b_in, mlstm_norm, q_norm, k_norm, lambda_q,
           lambda_k, diff_subln, pool_w, pool_scale, w_br_mlstm, w_br_diff, w_br_pool, w_out, norm_ffn,
           w_up, conv_w, conv_b, w_down):
    params = dict(norm_mix=norm_mix, w_in=w_in, b_in=b_in, mlstm_norm=mlstm_norm, q_norm=q_norm,
                  k_norm=k_norm, lambda_q=lambda_q, lambda_k=lambda_k, diff_subln=diff_subln,
                  pool_w=pool_w, pool_scale=pool_scale, w_br_mlstm=w_br_mlstm, w_br_diff=w_br_diff,
                  w_br_pool=w_br_pool, w_out=w_out, norm_ffn=norm_ffn, w_up=w_up, conv_w=conv_w,
                  conv_b=conv_b, w_down=w_down)
    depth = w_in.shape[0]
    n_prompt, seq, d = x_prompt.shape
    n_dec = x_sample.shape[0]
    assert x_sample.shape[1] == 1, "sample group decodes one token per sequence"
    bias_tab = _bias_by_distance(rel_bias)
    bias_tiles = _bias_tiles(rel_bias, min(ATTN_TQ, seq))
    ck = cache_k.reshape(cache_k.shape[:2] + (PAGE_ROWS, 2 * DIFF_DQK))
    cv = cache_v.reshape(cache_v.shape[:2] + (PAGE_ROWS, DIFF_DV))
    xp = x_prompt.reshape(n_prompt * seq, d)
    xs = x_sample.reshape(n_dec, d)
    outs_p, outs_s = [], []
    for l in range(depth):
        lw = _prep_layer(params, l)
        xp, *rest_p = _layer_prompt(xp, params, lw, bias_tiles, l, batch=n_prompt)
        xs, *rest_s = _layer_sample(xs, params, lw, bias_tab, l, ck, cv, page_table, state_mlstm_c[l],
                                    state_mlstm_n[l], state_mlstm_m[l], state_pool[l], state_conv[l])
        outs_p.append(rest_p)
        outs_s.append(rest_s)
    stack = lambda outs, idx: jnp.stack([o[idx] for o in outs])
    kp, vp, cp, np_, mp, pp, cvp = (stack(outs_p, i) for i in range(7))
    ks, vs, cs_, ns, ms, ps, cvs = (stack(outs_s, i) for i in range(7))
    return (xp.reshape(n_prompt, seq, d), xs.reshape(n_dec, 1, d),
            kp, vp, ks, vs, cp, np_, mp, cs_, ns, ms, pp, ps, cvp, cvs)
```

```python
import functools
import math

import jax
import jax.numpy as jnp
import numpy as np
from jax import lax
from jax.experimental import pallas as pl
from jax.experimental.pallas import tpu as pltpu

F32 = jnp.float32
BF16 = jnp.bfloat16

D_MODEL = 4096
PAGE_SIZE = 128
MLSTM_HEADS = 4
MLSTM_DV = 256
MLSTM_DQK = 128
MLSTM_CHUNK = 64
DIFF_HEADS = 8
DIFF_DV = 256
DIFF_DQK = 128
POOL_GROUPS = 4
POOL_WINDOWS = (2, 4, 8, 16)
POOL_GC = 256
POOL_CH = POOL_GROUPS * POOL_GC
POOL_BUF = 15
N_BRANCH = 3
D_FF = 11008
CONV_W = 3
REL_BUCKETS = 32
REL_MAX_DIST = 128
RMS_EPS = 1e-6

MLSTM_QK_W = MLSTM_HEADS * MLSTM_DQK
MLSTM_V_W = MLSTM_HEADS * MLSTM_DV
DIFF_QK_W = DIFF_HEADS * 2 * DIFF_DQK
DIFF_V_W = DIFF_HEADS * DIFF_DV

LANES = 128
SUBLANES = 8
VMEM_LIMIT = 56 * 1024 * 1024
NEG = -1e30
ROW_CHUNK = 512

BIAS_SPAN = 128


def _cparams(sem, flags=None):
    return pltpu.CompilerParams(dimension_semantics=sem, vmem_limit_bytes=VMEM_LIMIT, flags=flags)


def _rmsnorm_kernel(x_ref, g_ref, o_ref):
    x = x_ref[...]
    y = x * lax.rsqrt(jnp.mean(x * x, axis=-1, keepdims=True) + RMS_EPS)
    o_ref[...] = (y * g_ref[...]).astype(o_ref.dtype)


def _rmsnorm(x, g, tm):
    m, d = x.shape
    return pl.pallas_call(
        _rmsnorm_kernel,
        grid=(m // tm,),
        in_specs=[pl.BlockSpec((tm, d), lambda i: (i, 0)),
                  pl.BlockSpec((1, d), lambda i: (0, 0))],
        out_specs=pl.BlockSpec((tm, d), lambda i: (i, 0)),
        out_shape=jax.ShapeDtypeStruct((m, d), BF16),
        compiler_params=_cparams(("parallel",)),
        name="rmsnorm",
    )(x, g.reshape(1, d))


def _group_rmsnorm(y, g, width):
    outs = []
    for c in range(0, y.shape[1], width):
        yc = y[:, c:c + width]
        yn = yc * lax.rsqrt(jnp.mean(yc * yc, axis=-1, keepdims=True) + RMS_EPS)
        outs.append(yn * g[:, c:c + width])
    return outs[0] if len(outs) == 1 else jnp.concatenate(outs, axis=1)


def _ep_bias(acc, b):
    return (acc + b,)


def _ep_bias_bf16copy(acc, b):
    y = acc + b
    return (y, y.astype(BF16))


def _ep_bias_sigmoid(acc, b):
    return (jax.nn.sigmoid(acc + b),)


def _ep_knorm(acc, b, g):
    y = _group_rmsnorm(acc + b, g, DIFF_DQK)
    return (y, y.astype(BF16))


def _ep_qnorm_scaled(acc, b, g):
    y = _group_rmsnorm(acc + b, g, DIFF_DQK) * (DIFF_DQK ** -0.5)
    return (y, y.astype(BF16))


def _ep_residual(acc, r):
    return (r + acc,)


def _row_chunks(tm):
    step = ROW_CHUNK if tm % ROW_CHUNK == 0 else tm
    return [slice(r, r + step) for r in range(0, tm, step)]


def _mm_kernel(a_ref, w_ref, *rest, n_extra, n_out, epilogue, tile_extra):
    extras = rest[:n_extra]
    outs = rest[n_extra:n_extra + n_out]
    w = w_ref[...].astype(BF16)
    for rows in _row_chunks(a_ref.shape[0]):
        acc = jnp.dot(a_ref[rows, :].astype(BF16), w, preferred_element_type=F32)
        res = epilogue(acc, *[e[rows, :] if t else e[...] for e, t in zip(extras, tile_extra)])
        for o, r in zip(outs, res):
            o[rows, :] = r.astype(o.dtype)


def _matmul(a, w, extras, epilogue, out_dtypes, *, tm, tn, name, layer=None, col0=0, n=None):
    m, k = a.shape
    n = w.shape[-1] - col0 if n is None else n
    tm = min(tm, m)
    tn = min(tn, n)
    assert m % tm == 0 and n % tn == 0 and col0 % tn == 0, (m, n, tm, tn, col0)
    jb = col0 // tn
    if layer is None:
        w_spec = pl.BlockSpec((k, tn), lambda i, j: (0, jb + j))
    else:
        w_spec = pl.BlockSpec((None, k, tn), lambda i, j: (layer, 0, jb + j))
    in_specs = [pl.BlockSpec((tm, k), lambda i, j: (i, 0)), w_spec]
    tile_extra = []
    for e in extras:
        tile_extra.append(e.shape[0] != 1)
        if e.shape[0] == 1:
            in_specs.append(pl.BlockSpec((1, tn), lambda i, j: (0, j)))
        else:
            in_specs.append(pl.BlockSpec((tm, tn), lambda i, j: (i, j)))
    out_specs = [pl.BlockSpec((tm, tn), lambda i, j: (i, j)) for _ in out_dtypes]
    out_shape = [jax.ShapeDtypeStruct((m, n), dt) for dt in out_dtypes]
    return pl.pallas_call(
        functools.partial(_mm_kernel, n_extra=len(extras), n_out=len(out_dtypes), epilogue=epilogue,
                          tile_extra=tuple(tile_extra)),
        grid=(m // tm, n // tn),
        in_specs=in_specs,
        out_specs=out_specs,
        out_shape=out_shape,
        compiler_params=_cparams(("parallel", "arbitrary")),
        name=name,
    )(a, w, *extras)


def _merge_kernel(hm_ref, hd_ref, hp_ref, wm_ref, wd_ref, wp_ref, g0_ref, g1_ref, g2_ref, o_ref):
    wm, wd, wp = (w[...].astype(BF16) for w in (wm_ref, wd_ref, wp_ref))
    for rows in _row_chunks(o_ref.shape[0]):
        proj = lambda h_ref, w: jnp.dot(h_ref[rows, :].astype(BF16), w, preferred_element_type=F32)
        merged = (g0_ref[rows, :] * proj(hm_ref, wm)
                  + g1_ref[rows, :] * proj(hd_ref, wd)
                  + g2_ref[rows, :] * proj(hp_ref, wp))
        o_ref[rows, :] = merged.astype(o_ref.dtype)


def _merge(hm, hd, hp, wm, wd, wp, gates, *, layer, tm, tn):
    m = hm.shape[0]
    tm = min(tm, m)
    nj = D_MODEL // tn
    row = lambda width: pl.BlockSpec((tm, width), lambda i, j: (i, 0))
    col = lambda depth: pl.BlockSpec((None, depth, tn), lambda i, j: (layer, 0, j))
    gate = lambda br: pl.BlockSpec((tm, tn), lambda i, j: (i, br * nj + j))
    return pl.pallas_call(
        _merge_kernel,
        grid=(m // tm, nj),
        in_specs=[row(MLSTM_V_W), row(DIFF_V_W), row(POOL_CH),
                  col(MLSTM_V_W), col(DIFF_V_W), col(POOL_CH),
                  gate(0), gate(1), gate(2)],
        out_specs=pl.BlockSpec((tm, tn), lambda i, j: (i, j)),
        out_shape=jax.ShapeDtypeStruct((m, D_MODEL), BF16),
        compiler_params=_cparams(("parallel", "arbitrary")),
        name="merge",
    )(hm, hd, hp, wm, wd, wp, gates, gates, gates)


def _conv_gate(ug, uv, pg, pv, cw_g, cw_v, cb_g, cb_v):
    cg = cb_g + (cw_g[0:1] * pg[0] + cw_g[1:2] * pg[1] + cw_g[2:3] * ug)
    cv = cb_v + (cw_v[0:1] * pv[0] + cw_v[1:2] * pv[1] + cw_v[2:3] * uv)
    return jax.nn.silu(cg) * cv


def _ffn_up_prompt_kernel(a_ref, wg_ref, wv_ref, cwg_ref, cwv_ref, cbg_ref, cbv_ref,
                          act_ref, tail_ref, carry_ref, ext_ref, *, tiles_per_seq):
    i = pl.program_id(0)
    j = pl.program_id(1)
    tm = a_ref.shape[0]

    @pl.when(i % tiles_per_seq == 0)
    def _():
        carry_ref[:, pl.ds(j, 1)] = jnp.zeros((2, 1) + carry_ref.shape[2:], F32)

    ext_ref[0, 0:SUBLANES] = carry_ref[0, j]
    ext_ref[1, 0:SUBLANES] = carry_ref[1, j]
    wg = wg_ref[...].astype(BF16)
    wv = wv_ref[...].astype(BF16)
    for rows in _row_chunks(tm):
        a = a_ref[rows, :]
        ug = jnp.dot(a, wg, preferred_element_type=F32)
        uv = jnp.dot(a, wv, preferred_element_type=F32)
        r0, n = rows.start + SUBLANES, rows.stop - rows.start
        ext_ref[0, r0:r0 + n] = ug
        ext_ref[1, r0:r0 + n] = uv
        back = lambda half: (ext_ref[half, r0 - 2:r0 - 2 + n], ext_ref[half, r0 - 1:r0 - 1 + n])
        act = _conv_gate(ug, uv, back(0), back(1), cwg_ref[...], cwv_ref[...], cbg_ref[...], cbv_ref[...])
        act_ref[rows, :] = act.astype(act_ref.dtype)
    for half in range(2):
        last = ext_ref[half, tm:tm + SUBLANES]
        carry_ref[half, j] = last
        tail_ref[0, half] = last


def _ffn_up_prompt(h, w_up, conv_w, conv_b, *, layer, batch, tm, tn):
    m, d = h.shape
    nj = D_FF // tn
    tiles_per_seq = (m // batch) // tm
    conv_b = conv_b.reshape(1, 2 * D_FF)
    act, tail = pl.pallas_call(
        functools.partial(_ffn_up_prompt_kernel, tiles_per_seq=tiles_per_seq),
        grid=(m // tm, nj),
        in_specs=[pl.BlockSpec((tm, d), lambda i, j: (i, 0)),
                  pl.BlockSpec((None, d, tn), lambda i, j: (layer, 0, j)),
                  pl.BlockSpec((None, d, tn), lambda i, j: (layer, 0, nj + j)),
                  pl.BlockSpec((CONV_W, tn), lambda i, j: (0, j)),
                  pl.BlockSpec((CONV_W, tn), lambda i, j: (0, nj + j)),
                  pl.BlockSpec((1, tn), lambda i, j: (0, j)),
                  pl.BlockSpec((1, tn), lambda i, j: (0, nj + j))],
        out_specs=[pl.BlockSpec((tm, tn), lambda i, j: (i, j)),
                   pl.BlockSpec((1, 2, SUBLANES, tn), lambda i, j: (i, 0, 0, j))],
        out_shape=[jax.ShapeDtypeStruct((m, D_FF), BF16),
                   jax.ShapeDtypeStruct((m // tm, 2, SUBLANES, D_FF), F32)],
        scratch_shapes=[pltpu.VMEM((2, nj, SUBLANES, tn), F32),
                        pltpu.VMEM((2, SUBLANES + tm, tn), F32)],
        compiler_params=_cparams(("arbitrary", "arbitrary")),
        name="ffn_up_prompt",
    )(h, w_up, w_up, conv_w, conv_w, conv_b, conv_b)
    tail = tail[tiles_per_seq - 1::tiles_per_seq]
    conv_new = jnp.transpose(tail[:, :, SUBLANES - (CONV_W - 1):, :], (0, 2, 1, 3))
    return act, conv_new.reshape(batch, CONV_W - 1, 2 * D_FF)


def _ffn_up_sample_kernel(a_ref, wg_ref, wv_ref, cwg_ref, cwv_ref, cbg_ref, cbv_ref,
                          p2g_ref, p1g_ref, p2v_ref, p1v_ref, act_ref, ug_ref, uv_ref):
    a = a_ref[...].astype(BF16)
    ug = jnp.dot(a, wg_ref[...].astype(BF16), preferred_element_type=F32)
    uv = jnp.dot(a, wv_ref[...].astype(BF16), preferred_element_type=F32)
    act = _conv_gate(ug, uv, (p2g_ref[...], p1g_ref[...]), (p2v_ref[...], p1v_ref[...]),
                     cwg_ref[...], cwv_ref[...], cbg_ref[...], cbv_ref[...])
    act_ref[...] = act.astype(act_ref.dtype)
    ug_ref[...] = ug
    uv_ref[...] = uv


def _ffn_up_sample(h, w_up, conv_w, conv_b, conv_buf, *, layer, tn):
    b, d = h.shape
    nj = D_FF // tn
    conv_b = conv_b.reshape(1, 2 * D_FF)
    prev2 = conv_buf[:, 0, :]
    prev1 = conv_buf[:, 1, :]
    lo = lambda rows: pl.BlockSpec((rows, tn), lambda j: (0, j))
    hi = lambda rows: pl.BlockSpec((rows, tn), lambda j: (0, nj + j))
    act, ug, uv = pl.pallas_call(
        _ffn_up_sample_kernel,
        grid=(nj,),
        in_specs=[pl.BlockSpec((b, d), lambda j: (0, 0)),
                  pl.BlockSpec((None, d, tn), lambda j: (layer, 0, j)),
                  pl.BlockSpec((None, d, tn), lambda j: (layer, 0, nj + j)),
                  lo(CONV_W), hi(CONV_W), lo(1), hi(1),
                  lo(b), lo(b), hi(b), hi(b)],
        out_specs=[lo(b), lo(b), lo(b)],
        out_shape=[jax.ShapeDtypeStruct((b, D_FF), BF16),
                   jax.ShapeDtypeStruct((b, D_FF), F32),
                   jax.ShapeDtypeStruct((b, D_FF), F32)],
        compiler_params=_cparams(("arbitrary",)),
        name="ffn_up_sample",
    )(h, w_up, w_up, conv_w, conv_w, conv_b, conv_b, prev2, prev1, prev2, prev1)
    u = jnp.concatenate([ug, uv], axis=1)
    return act, jnp.concatenate([conv_buf[:, 1:], u[:, None, :]], axis=1)


def _log_sigmoid(x):
    return -(jnp.maximum(-x, 0.0) + jnp.log1p(jnp.exp(-jnp.abs(x))))


def _cumsum_rows(x):
    n = x.shape[0]
    row = lax.broadcasted_iota(jnp.int32, x.shape, 0)
    d = 1
    while d < n:
        x = x + jnp.where(row >= d, pltpu.roll(x, d, 0), 0.0)
        d *= 2
    return x


def _head_out_norm(h, g, o_gate):
    y = h * lax.rsqrt(jnp.mean(h * h, axis=-1, keepdims=True) + RMS_EPS)
    return (y * g) * jax.nn.sigmoid(o_gate)


def _mlstm_prompt_kernel(qkv_ref, mif_ref, mo_ref, nw_ref, hm_ref, c_out, n_out, m_out,
                         c_sc, n_sc, m_sc, *, cs, nc):
    ci = pl.program_id(1)

    @pl.when(ci == 0)
    def _():
        c_sc[...] = jnp.zeros(c_sc.shape, F32)
        n_sc[...] = jnp.zeros(n_sc.shape, F32)
        m_sc[...] = jnp.zeros(m_sc.shape, F32)

    mif = mif_ref[...]
    b_all = _cumsum_rows(_log_sigmoid(mif))
    row = lax.broadcasted_iota(jnp.int32, (cs, cs), 0)
    col = lax.broadcasted_iota(jnp.int32, (cs, cs), 1)
    causal = col <= row
    eye = col == row
    lane = lax.broadcasted_iota(jnp.int32, (1, LANES), 1)
    m_vec = jnp.zeros((1, LANES), F32)

    for h in range(MLSTM_HEADS):
        q = qkv_ref[:, h * MLSTM_DQK:(h + 1) * MLSTM_DQK]
        k = qkv_ref[:, MLSTM_QK_W + h * MLSTM_DQK:MLSTM_QK_W + (h + 1) * MLSTM_DQK] * (MLSTM_DQK ** -0.5)
        v = qkv_ref[:, 2 * MLSTM_QK_W + h * MLSTM_DV:2 * MLSTM_QK_W + (h + 1) * MLSTM_DV]
        li = mif[:, h:h + 1]
        b = b_all[:, MLSTM_HEADS + h:MLSTM_HEADS + h + 1]
        g_row = jnp.sum(jnp.where(eye, li - b, 0.0), axis=0, keepdims=True)
        dmat = jnp.where(causal, b + g_row, NEG)
        m_prev = m_sc[h]
        inter = b + m_prev
        m_t = jnp.maximum(inter, jnp.max(dmat, axis=-1, keepdims=True))
        w_inter = jnp.exp(inter - m_t)
        qb, kb, vb = q.astype(BF16), k.astype(BF16), v.astype(BF16)
        s = lax.dot_general(qb, kb, (((1,), (1,)), ((), ())), preferred_element_type=F32)
        sc = s * jnp.exp(dmat - m_t)
        c_prev = c_sc[h]
        n_prev = n_sc[h]
        num = (w_inter * jnp.dot(qb, c_prev.astype(BF16), preferred_element_type=F32)
               + jnp.dot(sc.astype(BF16), vb, preferred_element_type=F32))
        den = w_inter * jnp.sum(q * n_prev, axis=-1, keepdims=True) + jnp.sum(sc, axis=-1, keepdims=True)
        hh = num / jnp.maximum(jnp.abs(den), jnp.exp(-m_t))
        m_new = m_t[cs - 1:cs]
        b_last = b[cs - 1:cs]
        g_state = jnp.exp(b_last + m_prev - m_new)
        g_tok = jnp.exp(b_last - b + li - m_new)
        kg = k * g_tok
        c_new = g_state * c_prev + lax.dot_general(kg.astype(BF16), vb, (((0,), (0,)), ((), ())),
                                                   preferred_element_type=F32)
        n_new = g_state * n_prev + jnp.sum(kg, axis=0, keepdims=True)
        c_sc[h] = c_new
        n_sc[h] = n_new
        m_sc[h] = m_new
        m_vec = m_vec + jnp.where(lane == h, m_new, 0.0)
        sl = slice(h * MLSTM_DV, (h + 1) * MLSTM_DV)
        hm_ref[:, sl] = _head_out_norm(hh, nw_ref[:, sl], mo_ref[:, sl]).astype(hm_ref.dtype)

    @pl.when(ci == nc - 1)
    def _():
        c_out[0] = c_sc[...]
        n_out[0] = n_sc[...][:, 0, :]
        m_out[0] = m_vec


def _mlstm_prompt(mqkv, mif, mo, norm_w, *, batch):
    m = mqkv.shape[0]
    seq = m // batch
    cs = MLSTM_CHUNK if seq % MLSTM_CHUNK == 0 else seq
    nc = seq // cs
    rows = lambda width: pl.BlockSpec((cs, width), lambda b, c: (b * nc + c, 0))
    hm, c, n, mm = pl.pallas_call(
        functools.partial(_mlstm_prompt_kernel, cs=cs, nc=nc),
        grid=(batch, nc),
        in_specs=[rows(mqkv.shape[1]), rows(LANES), rows(MLSTM_V_W),
                  pl.BlockSpec((1, MLSTM_V_W), lambda b, c: (0, 0))],
        out_specs=[rows(MLSTM_V_W),
                   pl.BlockSpec((1, MLSTM_HEADS, MLSTM_DQK, MLSTM_DV), lambda b, c: (b, 0, 0, 0)),
                   pl.BlockSpec((1, MLSTM_HEADS, MLSTM_DQK), lambda b, c: (b, 0, 0)),
                   pl.BlockSpec((1, 1, LANES), lambda b, c: (b, 0, 0))],
        out_shape=[jax.ShapeDtypeStruct((m, MLSTM_V_W), BF16),
                   jax.ShapeDtypeStruct((batch, MLSTM_HEADS, MLSTM_DQK, MLSTM_DV), F32),
                   jax.ShapeDtypeStruct((batch, MLSTM_HEADS, MLSTM_DQK), F32),
                   jax.ShapeDtypeStruct((batch, 1, LANES), F32)],
        scratch_shapes=[pltpu.VMEM((MLSTM_HEADS, MLSTM_DQK, MLSTM_DV), F32),
                        pltpu.VMEM((MLSTM_HEADS, 1, MLSTM_DQK), F32),
                        pltpu.VMEM((MLSTM_HEADS, 1, 1), F32)],
        compiler_params=_cparams(("parallel", "arbitrary")),
        name="mlstm_prompt",
    )(mqkv, mif, mo, norm_w.reshape(1, MLSTM_V_W))
    return hm, c, n, mm[:, 0, :MLSTM_HEADS]


def _lanes_to_rows(x_row, eye):
    return jnp.sum(jnp.where(eye, x_row, 0.0), axis=1, keepdims=True)


def _mlstm_sample_kernel(qkv_ref, mif_ref, mo_ref, nw_ref, c_ref, n_ref, m_ref,
                         hm_ref, c_out, n_out, m_out):
    mif = mif_ref[0]
    lf_all = _log_sigmoid(mif)
    m_all = m_ref[0]
    row = lax.broadcasted_iota(jnp.int32, (MLSTM_DQK, MLSTM_DQK), 0)
    col = lax.broadcasted_iota(jnp.int32, (MLSTM_DQK, MLSTM_DQK), 1)
    eye = row == col
    lane = lax.broadcasted_iota(jnp.int32, (1, LANES), 1)
    m_vec = jnp.zeros((1, LANES), F32)
    for h in range(MLSTM_HEADS):
        q = qkv_ref[0, :, h * MLSTM_DQK:(h + 1) * MLSTM_DQK]
        k = qkv_ref[0, :, MLSTM_QK_W + h * MLSTM_DQK:MLSTM_QK_W + (h + 1) * MLSTM_DQK] * (MLSTM_DQK ** -0.5)
        v = qkv_ref[0, :, 2 * MLSTM_QK_W + h * MLSTM_DV:2 * MLSTM_QK_W + (h + 1) * MLSTM_DV]
        li = mif[:, h:h + 1]
        lf = lf_all[:, MLSTM_HEADS + h:MLSTM_HEADS + h + 1]
        m_prev = m_all[:, h:h + 1]
        inter = lf + m_prev
        m_t = jnp.maximum(inter, li)
        w_inter = jnp.exp(inter - m_t)
        sc = jnp.sum(q * k, axis=-1, keepdims=True) * jnp.exp(li - m_t)
        c_prev = c_ref[0, h]
        n_prev = n_ref[0, h:h + 1, :]
        q_col = _lanes_to_rows(q, eye)
        k_col = _lanes_to_rows(k, eye)
        num = w_inter * jnp.sum(q_col * c_prev, axis=0, keepdims=True) + sc * v
        den = w_inter * jnp.sum(q * n_prev, axis=-1, keepdims=True) + sc
        hh = num / jnp.maximum(jnp.abs(den), jnp.exp(-m_t))
        g_tok = jnp.exp(li - m_t)
        c_out[0, h] = w_inter * c_prev + (g_tok * k_col) * v
        n_out[0, h:h + 1, :] = w_inter * n_prev + g_tok * k
        m_vec = m_vec + jnp.where(lane == h, m_t, 0.0)
        sl = slice(h * MLSTM_DV, (h + 1) * MLSTM_DV)
        hm_ref[0, :, sl] = _head_out_norm(hh, nw_ref[:, sl], mo_ref[0, :, sl]).astype(hm_ref.dtype)
    m_out[0] = m_vec


def _mlstm_sample(mqkv, mif, mo, norm_w, c0, n0, m0):
    b = mqkv.shape[0]
    m0p = jnp.pad(m0, ((0, 0), (0, LANES - MLSTM_HEADS))).reshape(b, 1, LANES)
    per_seq = lambda width: pl.BlockSpec((1, 1, width), lambda i: (i, 0, 0))
    c_spec = pl.BlockSpec((1, MLSTM_HEADS, MLSTM_DQK, MLSTM_DV), lambda i: (i, 0, 0, 0))
    n_spec = pl.BlockSpec((1, MLSTM_HEADS, MLSTM_DQK), lambda i: (i, 0, 0))
    hm, c, n, mm = pl.pallas_call(
        _mlstm_sample_kernel,
        grid=(b,),
        in_specs=[per_seq(mqkv.shape[1]), per_seq(LANES), per_seq(MLSTM_V_W),
                  pl.BlockSpec((1, MLSTM_V_W), lambda i: (0, 0)),
                  c_spec, n_spec, per_seq(LANES)],
        out_specs=[per_seq(MLSTM_V_W), c_spec, n_spec, per_seq(LANES)],
        out_shape=[jax.ShapeDtypeStruct((b, 1, MLSTM_V_W), F32),
                   jax.ShapeDtypeStruct(c0.shape, F32),
                   jax.ShapeDtypeStruct(n0.shape, F32),
                   jax.ShapeDtypeStruct((b, 1, LANES), F32)],
        compiler_params=_cparams(("parallel",)),
        name="mlstm_sample",
    )(mqkv.reshape(b, 1, -1), mif.reshape(b, 1, LANES), mo.reshape(b, 1, -1),
      norm_w.reshape(1, MLSTM_V_W), c0, n0, m0p)
    return hm.reshape(b, MLSTM_V_W), c, n, mm[:, 0, :MLSTM_HEADS]


def _bucket_by_distance():
    max_exact = REL_BUCKETS // 2
    n = np.arange(BIAS_SPAN + 1)
    large = max_exact + np.floor(np.log(np.maximum(n, 1) / max_exact) / math.log(REL_MAX_DIST / max_exact)
                                 * (REL_BUCKETS - max_exact)).astype(np.int64)
    return np.where(n < max_exact, n, np.minimum(large, REL_BUCKETS - 1))


def _bucket_starts():
    bucket = _bucket_by_distance()
    return [int(np.argmax(bucket >= b)) for b in range(1, REL_BUCKETS)]


def _bias_by_distance(rel_bias):
    tab = jnp.transpose(rel_bias.astype(F32)[_bucket_by_distance()], (1, 0))
    return tab - tab[:, BIAS_SPAN:]


def _lambda(lq_ref, lk_ref, lam_init):
    lq = lq_ref[...]
    lk = lk_ref[...]
    e0 = jnp.exp(jnp.sum(lq[0:1] * lk[0:1], axis=-1, keepdims=True))
    e1 = jnp.exp(jnp.sum(lq[1:2] * lk[1:2], axis=-1, keepdims=True))
    return e0 - e1 + lam_init


def _diff_out_norm(d, sub, lam_init):
    y = d * lax.rsqrt(jnp.mean(d * d, axis=-1, keepdims=True) + RMS_EPS)
    return (y * sub) * (1.0 - lam_init)


def _bias_tiles_kernel(rb_ref, o_ref, *, tq):
    h = pl.program_id(0)
    blk = pl.program_id(1)
    row = lax.broadcasted_iota(jnp.int32, (tq, tq), 0)
    col = lax.broadcasted_iota(jnp.int32, (tq, tq), 1)
    dist = row - col + blk * tq
    far = rb_ref[h, REL_BUCKETS - 1]
    val = jnp.full((tq, tq), rb_ref[h, 0] - far, F32)
    for b, start in enumerate(_bucket_starts(), start=1):
        val = jnp.where(dist >= start, rb_ref[h, b] - far, val)
    o_ref[0, 0] = jnp.where(dist >= 0, val, NEG)


def _bias_tiles(rel_bias, tq):
    assert tq + 1 >= _bucket_starts()[-1]
    return pl.pallas_call(
        functools.partial(_bias_tiles_kernel, tq=tq),
        grid=(DIFF_HEADS, 2),
        in_specs=[pl.BlockSpec(memory_space=pltpu.SMEM)],
        out_specs=pl.BlockSpec((1, 1, tq, tq), lambda h, blk: (h, blk, 0, 0)),
        out_shape=jax.ShapeDtypeStruct((DIFF_HEADS, 2, tq, tq), F32),
        compiler_params=_cparams(("parallel", "parallel")),
        name="bias_tiles",
    )(jnp.transpose(rel_bias.astype(F32)))


def _attn_prompt_kernel(q_ref, k_ref, v_ref, bias_ref, lq_ref, lk_ref, sub_ref, o_ref,
                        m_sc, l_sc, acc_sc, *, lam_init):
    qi = pl.program_id(2)
    ki = pl.program_id(3)

    @pl.when(ki == 0)
    def _():
        m_sc[...] = jnp.full(m_sc.shape, NEG, F32)
        l_sc[...] = jnp.zeros(l_sc.shape, F32)
        acc_sc[...] = jnp.zeros(acc_sc.shape, F32)

    def tile(bias):
        v = v_ref[0]
        for mp in range(2):
            sl = slice(mp * DIFF_DQK, (mp + 1) * DIFF_DQK)
            s = lax.dot_general(q_ref[0, :, sl], k_ref[0, :, sl], (((1,), (1,)), ((), ())),
                                preferred_element_type=F32)
            if bias is not None:
                s = s + bias
            m_prev = m_sc[mp]
            m_new = jnp.maximum(m_prev, jnp.max(s, axis=-1, keepdims=True))
            alpha = jnp.exp(m_prev - m_new)
            p = jnp.exp(s - m_new)
            l_sc[mp] = alpha * l_sc[mp] + jnp.sum(p, axis=-1, keepdims=True)
            acc_sc[mp] = alpha * acc_sc[mp] + jnp.dot(p.astype(BF16), v, preferred_element_type=F32)
            m_sc[mp] = m_new

    @pl.when(ki < qi - 1)
    def _():
        tile(None)

    @pl.when((ki >= qi - 1) & (ki <= qi))
    def _():
        tile(bias_ref[0, 0])

    @pl.when(ki == qi)
    def _():
        lam = _lambda(lq_ref, lk_ref, lam_init)
        d = acc_sc[0] / l_sc[0] - lam * (acc_sc[1] / l_sc[1])
        o_ref[0] = _diff_out_norm(d, sub_ref[...], lam_init).astype(o_ref.dtype)


def _attn_prompt(q, k, v, bias, lam_q, lam_k, subln, *, batch, lam_init):
    m = q.shape[0]
    seq = m // batch
    tq = bias.shape[-1]
    assert seq % tq == 0
    nq = seq // tq
    hw = 2 * DIFF_DQK
    q3, k3, v3 = (a.reshape(batch, seq, -1) for a in (q, k, v))
    kv_spec = pl.BlockSpec((1, tq, hw), lambda b, h, qi, ki: (b, jnp.minimum(ki, qi), h))
    out = pl.pallas_call(
        functools.partial(_attn_prompt_kernel, lam_init=lam_init),
        grid=(batch, DIFF_HEADS, nq, nq),
        in_specs=[pl.BlockSpec((1, tq, hw), lambda b, h, qi, ki: (b, qi, h)),
                  kv_spec, kv_spec,
                  pl.BlockSpec((1, 1, tq, tq), lambda b, h, qi, ki: (h, jnp.clip(qi - ki, 0, 1), 0, 0)),
                  pl.BlockSpec((2, DIFF_DQK), lambda b, h, qi, ki: (0, 0)),
                  pl.BlockSpec((2, DIFF_DQK), lambda b, h, qi, ki: (0, 0)),
                  pl.BlockSpec((1, DIFF_DV), lambda b, h, qi, ki: (0, 0))],
        out_specs=pl.BlockSpec((1, tq, DIFF_DV), lambda b, h, qi, ki: (b, qi, h)),
        out_shape=jax.ShapeDtypeStruct((batch, seq, DIFF_V_W), BF16),
        scratch_shapes=[pltpu.VMEM((2, tq, 1), F32), pltpu.VMEM((2, tq, 1), F32),
                        pltpu.VMEM((2, tq, DIFF_DV), F32)],
        compiler_params=_cparams(("parallel", "parallel", "parallel", "arbitrary")),
        name="attn_prompt",
    )(q3, k3, v3, bias, lam_q, lam_k, subln.reshape(1, DIFF_DV))
    return out.reshape(m, DIFF_V_W)


N_MAPS = 2 * DIFF_HEADS
PAGE_ROWS = PAGE_SIZE * DIFF_HEADS
NEW_TOKENS = 16
DECODE_PAGES_PER_STEP = 4


def _attn_sample_kernel(pt_ref, qm_ref, kn_ref, vn_ref, *rest, n_steps, pages, lam_init):
    kc_refs, vc_refs = rest[:pages], rest[pages:2 * pages]
    (bfar_ref, blast_ref, bnew_ref, lq_ref, lk_ref, sub_ref, o_ref,
     qx_sc, m_sc, l_sc, acc_sc) = rest[2 * pages:]
    p = pl.program_id(1)
    sub_i = lax.broadcasted_iota(jnp.int32, (DIFF_HEADS, LANES), 0)
    lane_i = lax.broadcasted_iota(jnp.int32, (DIFF_HEADS, LANES), 1)
    own = (lane_i == 2 * sub_i) | (lane_i == 2 * sub_i + 1)

    @pl.when(p == 0)
    def _():
        qm = qm_ref[0]
        r = lax.broadcasted_iota(jnp.int32, qm.shape, 0)
        first = (r & 1) == 0
        qx = jnp.concatenate([jnp.where(first, qm, 0.0), jnp.where(first, 0.0, qm)], axis=1)
        qx_sc[...] = jnp.zeros(qx_sc.shape, BF16)
        qx_sc[0:N_MAPS, :] = qx.astype(BF16)
        m_sc[...] = jnp.where(own, NEG, 0.0)
        l_sc[...] = jnp.zeros(l_sc.shape, F32)
        acc_sc[...] = jnp.zeros(acc_sc.shape, F32)

    row = lax.broadcasted_iota(jnp.int32, (LANES, LANES), 0)
    col = lax.broadcasted_iota(jnp.int32, (LANES, LANES), 1)
    eye = row == col

    def per_lane_column(x):
        x_row = jnp.sum(jnp.where(own, x, 0.0), axis=0, keepdims=True)
        return jnp.sum(jnp.where(eye, x_row, 0.0), axis=1, keepdims=True)

    def group_update(blocks):
        scores = []
        m_prev = m_sc[...]
        m_new = m_prev
        for kf, _, bias3 in blocks:
            s_all = lax.dot_general(kf.astype(BF16), qx_sc[...], (((1,), (1,)), ((), ())),
                                    preferred_element_type=F32)
            s3 = s_all.reshape(kf.shape[0] // DIFF_HEADS, DIFF_HEADS, LANES) + bias3
            m_new = jnp.maximum(m_new, jnp.max(s3, axis=0))
            scores.append(s3)
        alpha = jnp.exp(m_prev - m_new)
        l_new = alpha * l_sc[...]
        pv = None
        for s3, (_, vf, _) in zip(scores, blocks):
            p3 = jnp.exp(s3 - m_new[None])
            l_new = l_new + jnp.sum(p3, axis=0)
            p_t = p3.reshape(vf.shape[0], LANES).T.astype(BF16)
            part = jnp.dot(p_t, vf.astype(BF16), preferred_element_type=F32)
            pv = part if pv is None else pv + part
        l_sc[...] = l_new
        m_sc[...] = m_new
        acc_sc[...] = per_lane_column(alpha) * acc_sc[...] + pv

    far = bfar_ref[...][None]

    @pl.when(p < n_steps - 1)
    def _():
        group_update([(kc[...], vc[...], far) for kc, vc in zip(kc_refs, vc_refs)])

    @pl.when(p == n_steps - 1)
    def _():
        biases = [far] * (pages - 1) + [blast_ref[...].reshape(PAGE_SIZE, DIFF_HEADS, LANES)]
        blocks = [(kc[...], vc[...], b3) for kc, vc, b3 in zip(kc_refs, vc_refs, biases)]
        blocks.append((kn_ref[0], vn_ref[0], bnew_ref[...].reshape(NEW_TOKENS, DIFF_HEADS, LANES)))
        group_update(blocks)
        lam = _lambda(lq_ref, lk_ref, lam_init)
        l_col = per_lane_column(l_sc[...])
        out = acc_sc[0:N_MAPS, :] / l_col[0:N_MAPS]
        for h in range(DIFF_HEADS):
            d = out[2 * h:2 * h + 1] - lam * out[2 * h + 1:2 * h + 2]
            sl = slice(h * DIFF_DV, (h + 1) * DIFF_DV)
            o_ref[0, :, sl] = _diff_out_norm(d, sub_ref[...], lam_init).astype(o_ref.dtype)


def _attn_sample(q, k_new, v_new, cache_k, cache_v, layer, page_table, bias_tab, lam_q, lam_k, subln,
                 *, lam_init):
    b, n_pages = page_table.shape
    assert PAGE_SIZE >= BIAS_SPAN
    pages = DECODE_PAGES_PER_STEP if n_pages % DECODE_PAGES_PER_STEP == 0 else 1
    n_steps = n_pages // pages
    hw = 2 * DIFF_DQK
    sub_i = lax.broadcasted_iota(jnp.int32, (DIFF_HEADS, LANES), 0)
    lane_i = lax.broadcasted_iota(jnp.int32, (DIFF_HEADS, LANES), 1)
    own = (lane_i == 2 * sub_i) | (lane_i == 2 * sub_i + 1)

    def rows_bias(per_token):
        t = per_token.shape[0]
        return jnp.where(own[None], per_token[:, :, None], NEG).reshape(t * DIFF_HEADS, LANES)

    b_far = jnp.where(own, 0.0, NEG)
    b_last = rows_bias(jnp.transpose(bias_tab[:, :0:-1][:, :PAGE_SIZE]))
    b_new = rows_bias(jnp.full((NEW_TOKENS, DIFF_HEADS), NEG, F32).at[0].set(bias_tab[:, 0]))
    pad_page = lambda a: jnp.pad(a.reshape(b, DIFF_HEADS, hw), ((0, 0), (0, (NEW_TOKENS - 1) * DIFF_HEADS), (0, 0)))
    new_spec = pl.BlockSpec((1, NEW_TOKENS * DIFF_HEADS, hw), lambda i, p, pt: (i, 0, 0))
    page = lambda r: pl.BlockSpec((None, None, PAGE_ROWS, hw),
                                  lambda i, p, pt: (layer, pt[i * n_pages + p * pages + r], 0, 0))
    const = lambda shape: pl.BlockSpec(shape, lambda i, p, pt: (0,) * len(shape))
    out = pl.pallas_call(
        functools.partial(_attn_sample_kernel, n_steps=n_steps, pages=pages, lam_init=lam_init),
        grid_spec=pltpu.PrefetchScalarGridSpec(
            num_scalar_prefetch=1,
            grid=(b, n_steps),
            in_specs=[pl.BlockSpec((1, N_MAPS, DIFF_DQK), lambda i, p, pt: (i, 0, 0)),
                      new_spec, new_spec, *[page(r) for r in range(pages)], *[page(r) for r in range(pages)],
                      const((DIFF_HEADS, LANES)), const((PAGE_ROWS, LANES)),
                      const((NEW_TOKENS * DIFF_HEADS, LANES)),
                      const((2, DIFF_DQK)), const((2, DIFF_DQK)), const((1, DIFF_DV))],
            out_specs=pl.BlockSpec((1, 1, DIFF_V_W), lambda i, p, pt: (i, 0, 0)),
            scratch_shapes=[pltpu.VMEM((LANES, hw), BF16),
                            pltpu.VMEM((DIFF_HEADS, LANES), F32), pltpu.VMEM((DIFF_HEADS, LANES), F32),
                            pltpu.VMEM((LANES, DIFF_DV), F32)]),
        out_shape=jax.ShapeDtypeStruct((b, 1, DIFF_V_W), F32),
        compiler_params=_cparams(("parallel", "arbitrary")),
        name="attn_sample",
    )(page_table.reshape(-1), q.reshape(b, N_MAPS, DIFF_DQK), pad_page(k_new), pad_page(v_new),
      *[cache_k] * pages, *[cache_v] * pages, b_far, b_last, b_new, lam_q, lam_k, subln.reshape(1, DIFF_DV))
    return out.reshape(b, DIFF_V_W)


HIST = 16


def _window_sum(ext, w, n):
    s = ext
    d = 1
    while d < w:
        s = s + pltpu.roll(s, d, 0)
        d *= 2
    return s[HIST:HIST + n]


def _pool_prompt_kernel(u_ref, w_ref, sc_ref, o_ref, hist_out, hist_sc, *, tm):
    t = pl.program_id(1)

    @pl.when(t == 0)
    def _():
        hist_sc[...] = jnp.zeros(hist_sc.shape, F32)

    pos = t * tm + lax.broadcasted_iota(jnp.int32, (tm, 1), 0)
    for gi, w in enumerate(POOL_WINDOWS):
        sl = slice(gi * POOL_GC, (gi + 1) * POOL_GC)
        u = u_ref[:, sl]
        ext = jnp.concatenate([hist_sc[:, sl], u], axis=0)
        cnt = jnp.minimum(w, pos + 1).astype(F32)
        pooled = _window_sum(ext, w, tm) / cnt - u
        mixed = jnp.dot(pooled.astype(BF16), w_ref[gi], preferred_element_type=F32)
        o_ref[:, sl] = (mixed * sc_ref[:, sl]).astype(o_ref.dtype)
        new_hist = u[tm - HIST:]
        hist_sc[:, sl] = new_hist
        hist_out[0, :, sl] = new_hist


def _pool_prompt(u, pool_w, pool_scale, *, batch, tm):
    m = u.shape[0]
    seq = m // batch
    tm = min(tm, seq)
    assert tm >= HIST
    tps = seq // tm
    hp, hist = pl.pallas_call(
        functools.partial(_pool_prompt_kernel, tm=tm),
        grid=(batch, tps),
        in_specs=[pl.BlockSpec((tm, POOL_CH), lambda b, t: (b * tps + t, 0)),
                  pl.BlockSpec((POOL_GROUPS, POOL_GC, POOL_GC), lambda b, t: (0, 0, 0)),
                  pl.BlockSpec((1, POOL_CH), lambda b, t: (0, 0))],
        out_specs=[pl.BlockSpec((tm, POOL_CH), lambda b, t: (b * tps + t, 0)),
                   pl.BlockSpec((1, HIST, POOL_CH), lambda b, t: (b, 0, 0))],
        out_shape=[jax.ShapeDtypeStruct((m, POOL_CH), BF16),
                   jax.ShapeDtypeStruct((batch, HIST, POOL_CH), F32)],
        scratch_shapes=[pltpu.VMEM((HIST, POOL_CH), F32)],
        compiler_params=_cparams(("parallel", "arbitrary")),
        name="pool_prompt",
    )(u, pool_w, pool_scale.reshape(1, POOL_CH))
    return hp, hist[:, HIST - POOL_BUF:]


def _pool_sample_kernel(ucat_ref, w_ref, sc_ref, o_ref):
    for gi, w in enumerate(POOL_WINDOWS):
        sl = slice(gi * POOL_GC, (gi + 1) * POOL_GC)
        new = ucat_ref[HIST - 1, :, sl]
        win = new
        for r in range(HIST - w, HIST - 1):
            win = win + ucat_ref[r, :, sl]
        pooled = win / float(w) - new
        mixed = jnp.dot(pooled.astype(BF16), w_ref[gi], preferred_element_type=F32)
        o_ref[:, sl] = (mixed * sc_ref[:, sl]).astype(o_ref.dtype)


def _pool_sample(u, buf, pool_w, pool_scale):
    b = u.shape[0]
    assert buf.shape[1] == HIST - 1
    ucat = jnp.concatenate([buf, u[:, None, :]], axis=1)
    hp = pl.pallas_call(
        _pool_sample_kernel,
        out_shape=jax.ShapeDtypeStruct((b, POOL_CH), F32),
        compiler_params=pltpu.CompilerParams(vmem_limit_bytes=VMEM_LIMIT),
        name="pool_sample",
    )(jnp.transpose(ucat, (1, 0, 2)), pool_w, pool_scale.reshape(1, POOL_CH))
    return hp, ucat[:, 1:]


IN_SPLITS = (MLSTM_QK_W, MLSTM_QK_W, MLSTM_V_W, 2 * MLSTM_HEADS, MLSTM_V_W,
             DIFF_QK_W, DIFF_QK_W, DIFF_V_W, POOL_CH, N_BRANCH * D_MODEL)
IN_OFFS = tuple(int(o) for o in np.cumsum((0,) + IN_SPLITS))
GATE_COLS = (IN_OFFS[3], IN_OFFS[4])


def _prep_layer(p, l):
    w_in, b_in = p["w_in"][l], p["b_in"][l]
    g0, g1 = GATE_COLS
    pad = ((0, 0), (0, LANES - (g1 - g0)))
    return dict(
        w_tail=w_in[:, g1:].astype(BF16), b_tail=b_in[g1:].reshape(1, -1),
        w_mif=jnp.pad(w_in[:, g0:g1].astype(BF16), pad), b_mif=jnp.pad(b_in[g0:g1].reshape(1, -1), pad),
        w_head=w_in[:, :g0].astype(BF16), b_head=b_in[:g0].reshape(1, -1),
        w_down=p["w_down"][l].astype(BF16),
        q_norm=jnp.tile(p["q_norm"][l].reshape(1, -1), (1, DIFF_HEADS)),
        k_norm=jnp.tile(p["k_norm"][l].reshape(1, -1), (1, DIFF_HEADS)),
        norm_mix=p["norm_mix"][l], norm_ffn=p["norm_ffn"][l], mlstm_norm=p["mlstm_norm"][l],
        lambda_q=p["lambda_q"][l], lambda_k=p["lambda_k"][l], diff_subln=p["diff_subln"][l],
        pool_w=p["pool_w"][l].astype(BF16), pool_scale=p["pool_scale"][l],
        conv_w=p["conv_w"][l], conv_b=p["conv_b"][l],
    )


def _projections(h, lw, *, tm, tn):
    g1 = GATE_COLS[1]

    def tail(seg, epilogue, extras, out_dtypes, name):
        c0, c1 = IN_OFFS[seg] - g1, IN_OFFS[seg + 1] - g1
        return _matmul(h, lw["w_tail"], (lw["b_tail"][:, c0:c1],) + extras, epilogue, out_dtypes,
                       tm=tm, tn=tn, col0=c0, n=c1 - c0, name=name)

    out = {}
    (out["mqkv"],) = _matmul(h, lw["w_head"], (lw["b_head"],), _ep_bias, (F32,), tm=tm, tn=tn, name="proj_mqkv")
    (out["mif"],) = _matmul(h, lw["w_mif"], (lw["b_mif"],), _ep_bias, (F32,), tm=tm, tn=tn, name="proj_mif")
    (out["mo"],) = tail(4, _ep_bias, (), (F32,), "proj_mo")
    out["dq"], out["dq16"] = tail(5, _ep_qnorm_scaled, (lw["q_norm"],), (F32, BF16), "proj_dq")
    out["dk"], out["dk16"] = tail(6, _ep_knorm, (lw["k_norm"],), (F32, BF16), "proj_dk")
    out["dv"], out["dv16"] = tail(7, _ep_bias_bf16copy, (), (F32, BF16), "proj_dv")
    (out["pu"],) = tail(8, _ep_bias, (), (F32,), "proj_pu")
    (out["gates"],) = tail(9, _ep_bias_sigmoid, (), (F32,), "proj_gates")
    return out


ATTN_TQ = 512
TILES_PROMPT = dict(norm=512, proj=(1024, 512), merge=(1024, 256), out=(1024, 512),
                    ffn_up=(1024, 256), ffn_down=(512, 256), pool=512)
TILES_SAMPLE = dict(proj=512, merge=256, out=512, ffn_up=256, ffn_down=256)


def _layer_prompt(x, p, lw, bias_tiles, l, *, batch):
    m = x.shape[0]
    t = TILES_PROMPT
    lam_init = 0.8 - 0.6 * math.exp(-0.3 * l)
    h = _rmsnorm(x, lw["norm_mix"], tm=t["norm"])
    pr = _projections(h, lw, tm=t["proj"][0], tn=t["proj"][1])
    hm, c1, n1, m1 = _mlstm_prompt(pr["mqkv"], pr["mif"], pr["mo"], lw["mlstm_norm"], batch=batch)
    hd = _attn_prompt(pr["dq16"], pr["dk16"], pr["dv16"], bias_tiles, lw["lambda_q"], lw["lambda_k"],
                      lw["diff_subln"], batch=batch, lam_init=lam_init)
    hp, pool_new = _pool_prompt(pr["pu"], lw["pool_w"], lw["pool_scale"], batch=batch, tm=t["pool"])
    merged = _merge(hm, hd, hp, p["w_br_mlstm"], p["w_br_diff"], p["w_br_pool"], pr["gates"],
                    layer=l, tm=t["merge"][0], tn=t["merge"][1])
    (x1,) = _matmul(merged, p["w_out"], (x,), _ep_residual, (F32,), layer=l, tm=t["out"][0], tn=t["out"][1],
                    name="out_proj")
    h2 = _rmsnorm(x1, lw["norm_ffn"], tm=t["norm"])
    act, conv_new = _ffn_up_prompt(h2, p["w_up"], lw["conv_w"], lw["conv_b"], layer=l, batch=batch,
                                   tm=t["ffn_up"][0], tn=t["ffn_up"][1])
    (x2,) = _matmul(act, lw["w_down"], (x1,), _ep_residual, (F32,), tm=t["ffn_down"][0],
                    tn=t["ffn_down"][1], name="ffn_down")
    seq = m // batch
    kv_shape = (batch, seq, DIFF_HEADS, DIFF_DV)
    return x2, pr["dk"].reshape(kv_shape), pr["dv"].reshape(kv_shape), c1, n1, m1, pool_new, conv_new


def _layer_sample(x, p, lw, bias_tab, l, cache_k, cache_v, page_table, c0, n0, m0, pool_buf, conv_buf):
    b = x.shape[0]
    t = TILES_SAMPLE
    lam_init = 0.8 - 0.6 * math.exp(-0.3 * l)
    h = _rmsnorm(x, lw["norm_mix"], tm=b)
    pr = _projections(h, lw, tm=b, tn=t["proj"])
    hm, c1, n1, m1 = _mlstm_sample(pr["mqkv"], pr["mif"], pr["mo"], lw["mlstm_norm"], c0, n0, m0)
    hd = _attn_sample(pr["dq"], pr["dk"], pr["dv"], cache_k, cache_v, l, page_table, bias_tab,
                      lw["lambda_q"], lw["lambda_k"], lw["diff_subln"], lam_init=lam_init)
    hp, pool_new = _pool_sample(pr["pu"], pool_buf, lw["pool_w"], lw["pool_scale"])
    merged = _merge(hm, hd, hp, p["w_br_mlstm"], p["w_br_diff"], p["w_br_pool"], pr["gates"],
                    layer=l, tm=b, tn=t["merge"])
    (x1,) = _matmul(merged, p["w_out"], (x,), _ep_residual, (F32,), layer=l, tm=b, tn=t["out"],
                    name="out_proj_s")
    h2 = _rmsnorm(x1, lw["norm_ffn"], tm=b)
    act, conv_new = _ffn_up_sample(h2, p["w_up"], lw["conv_w"], lw["conv_b"], conv_buf, layer=l, tn=t["ffn_up"])
    (x2,) = _matmul(act, lw["w_down"], (x1,), _ep_residual, (F32,), tm=b, tn=t["ffn_down"],
                    name="ffn_down_s")
    kv_shape = (b, 1, DIFF_HEADS, DIFF_DV)
    return x2, pr["dk"].reshape(kv_shape), pr["dv"].reshape(kv_shape), c1, n1, m1, pool_new, conv_new


def kernel(x_prompt, x_sample, cache_k, cache_v, page_table, state_mlstm_c, state_mlstm_n, state_mlstm_m,
           state_pool, state_conv, rel_bias, norm_mix, w_in, b_in, mlstm_norm, q_norm, k_norm, lambda_q,
           lambda_k, diff_subln, pool_w, pool_scale, w_br_mlstm, w_br_diff, w_br_pool, w_out, norm_ffn,
           w_up, conv_w, conv_b, w_down):
    params = dict(norm_mix=norm_mix, w_in=w_in, b_in=b_in, mlstm_norm=mlstm_norm, q_norm=q_norm,
                  k_norm=k_norm, lambda_q=lambda_q, lambda_k=lambda_k, diff_subln=diff_subln,
                  pool_w=pool_w, pool_scale=pool_scale, w_br_mlstm=w_br_mlstm, w_br_diff=w_br_diff,
                  w_br_pool=w_br_pool, w_out=w_out, norm_ffn=norm_ffn, w_up=w_up, conv_w=conv_w,
                  conv_b=conv_b, w_down=w_down)
    depth = w_in.shape[0]
    n_prompt, seq, d = x_prompt.shape
    n_dec = x_sample.shape[0]
    assert x_sample.shape[1] == 1, "sample group decodes one token per sequence"
    bias_tab = _bias_by_distance(rel_bias)
    bias_tiles = _bias_tiles(rel_bias, min(ATTN_TQ, seq))
    ck = cache_k.reshape(cache_k.shape[:2] + (PAGE_ROWS, 2 * DIFF_DQK))
    cv = cache_v.reshape(cache_v.shape[:2] + (PAGE_ROWS, DIFF_DV))
    xp = x_prompt.reshape(n_prompt * seq, d)
    xs = x_sample.reshape(n_dec, d)
    outs_p, outs_s = [], []
    for l in range(depth):
        lw = _prep_layer(params, l)
        xp, *rest_p = _layer_prompt(xp, params, lw, bias_tiles, l, batch=n_prompt)
        xs, *rest_s = _layer_sample(xs, params, lw, bias_tab, l, ck, cv, page_table, state_mlstm_c[l],
                                    state_mlstm_n[l], state_mlstm_m[l], state_pool[l], state_conv[l])
        outs_p.append(rest_p)
        outs_s.append(rest_s)
    stack = lambda outs, idx: jnp.stack([o[idx] for o in outs])
    kp, vp, cp, np_, mp, pp, cvp = (stack(outs_p, i) for i in range(7))
    ks, vs, cs_, ns, ms, ps, cvs = (stack(outs_s, i) for i in range(7))
    return (xp.reshape(n_prompt, seq, d), xs.reshape(n_dec, 1, d),
            kp, vp, ks, vs, cp, np_, mp, cs_, ns, ms, pp, ps, cvp, cvs)
```

```python
import functools
import math

import jax
import jax.numpy as jnp
import numpy as np
from jax import lax
from jax.experimental import pallas as pl
from jax.experimental.pallas import tpu as pltpu

F32 = jnp.float32
BF16 = jnp.bfloat16

D_MODEL = 4096
PAGE_SIZE = 128
MLSTM_HEADS = 4
MLSTM_DV = 256
MLSTM_DQK = 128
MLSTM_CHUNK = 64
DIFF_HEADS = 8
DIFF_DV = 256
DIFF_DQK = 128
POOL_GROUPS = 4
POOL_WINDOWS = (2, 4, 8, 16)
POOL_GC = 256
POOL_CH = POOL_GROUPS * POOL_GC
POOL_BUF = 15
N_BRANCH = 3
D_FF = 11008
CONV_W = 3
REL_BUCKETS = 32
REL_MAX_DIST = 128
RMS_EPS = 1e-6

MLSTM_QK_W = MLSTM_HEADS * MLSTM_DQK
MLSTM_V_W = MLSTM_HEADS * MLSTM_DV
DIFF_QK_W = DIFF_HEADS * 2 * DIFF_DQK
DIFF_V_W = DIFF_HEADS * DIFF_DV

LANES = 128
SUBLANES = 8
VMEM_LIMIT = 56 * 1024 * 1024
NEG = -1e30
ROW_CHUNK = 512

BIAS_SPAN = 128


def _cparams(sem, flags=None):
    return pltpu.CompilerParams(dimension_semantics=sem, vmem_limit_bytes=VMEM_LIMIT, flags=flags)


def _rmsnorm_kernel(x_ref, g_ref, o_ref):
    x = x_ref[...]
    y = x * lax.rsqrt(jnp.mean(x * x, axis=-1, keepdims=True) + RMS_EPS)
    o_ref[...] = (y * g_ref[...]).astype(o_ref.dtype)


def _rmsnorm(x, g, tm):
    m, d = x.shape
    return pl.pallas_call(
        _rmsnorm_kernel,
        grid=(m // tm,),
        in_specs=[pl.BlockSpec((tm, d), lambda i: (i, 0)),
                  pl.BlockSpec((1, d), lambda i: (0, 0))],
        out_specs=pl.BlockSpec((tm, d), lambda i: (i, 0)),
        out_shape=jax.ShapeDtypeStruct((m, d), BF16),
        compiler_params=_cparams(("parallel",)),
        name="rmsnorm",
    )(x, g.reshape(1, d))


def _group_rmsnorm(y, g, width):
    outs = []
    for c in range(0, y.shape[1], width):
        yc = y[:, c:c + width]
        yn = yc * lax.rsqrt(jnp.mean(yc * yc, axis=-1, keepdims=True) + RMS_EPS)
        outs.append(yn * g[:, c:c + width])
    return outs[0] if len(outs) == 1 else jnp.concatenate(outs, axis=1)


def _ep_bias(acc, b):
    return (acc + b,)


def _ep_bias_bf16copy(acc, b):
    y = acc + b
    return (y, y.astype(BF16))


def _ep_bias_sigmoid(acc, b):
    return (jax.nn.sigmoid(acc + b),)


def _ep_knorm(acc, b, g):
    y = _group_rmsnorm(acc + b, g, DIFF_DQK)
    return (y, y.astype(BF16))


def _ep_qnorm_scaled(acc, b, g):
    y = _group_rmsnorm(acc + b, g, DIFF_DQK) * (DIFF_DQK ** -0.5)
    return (y, y.astype(BF16))


def _ep_residual(acc, r):
    return (r + acc,)


def _row_chunks(tm):
    step = ROW_CHUNK if tm % ROW_CHUNK == 0 else tm
    return [slice(r, r + step) for r in range(0, tm, step)]


def _mm_kernel(a_ref, w_ref, *rest, n_extra, n_out, epilogue, tile_extra):
    extras = rest[:n_extra]
    outs = rest[n_extra:n_extra + n_out]
    w = w_ref[...].astype(BF16)
    for rows in _row_chunks(a_ref.shape[0]):
        acc = jnp.dot(a_ref[rows, :].astype(BF16), w, preferred_element_type=F32)
        res = epilogue(acc, *[e[rows, :] if t else e[...] for e, t in zip(extras, tile_extra)])
        for o, r in zip(outs, res):
            o[rows, :] = r.astype(o.dtype)


def _matmul(a, w, extras, epilogue, out_dtypes, *, tm, tn, name, layer=None, col0=0, n=None):
    m, k = a.shape
    n = w.shape[-1] - col0 if n is None else n
    tm = min(tm, m)
    tn = min(tn, n)
    assert m % tm == 0 and n % tn == 0 and col0 % tn == 0, (m, n, tm, tn, col0)
    jb = col0 // tn
    if w.ndim == 2:
        w_spec = pl.BlockSpec((k, tn), lambda i, j: (layer or 0, jb + j))
    else:
        w_spec = pl.BlockSpec((None, k, tn), lambda i, j: (layer, 0, jb + j))
    in_specs = [pl.BlockSpec((tm, k), lambda i, j: (i, 0)), w_spec]
    tile_extra = []
    for e in extras:
        tile_extra.append(e.shape[0] != 1)
        if e.shape[0] == 1:
            in_specs.append(pl.BlockSpec((1, tn), lambda i, j: (0, j)))
        else:
            in_specs.append(pl.BlockSpec((tm, tn), lambda i, j: (i, j)))
    out_specs = [pl.BlockSpec((tm, tn), lambda i, j: (i, j)) for _ in out_dtypes]
    out_shape = [jax.ShapeDtypeStruct((m, n), dt) for dt in out_dtypes]
    return pl.pallas_call(
        functools.partial(_mm_kernel, n_extra=len(extras), n_out=len(out_dtypes), epilogue=epilogue,
                          tile_extra=tuple(tile_extra)),
        grid=(m // tm, n // tn),
        in_specs=in_specs,
        out_specs=out_specs,
        out_shape=out_shape,
        compiler_params=_cparams(("parallel", "arbitrary")),
        name=name,
    )(a, w, *extras)


def _merge_kernel(hm_ref, hd_ref, hp_ref, wm_ref, wd_ref, wp_ref, g0_ref, g1_ref, g2_ref, o_ref):
    wm, wd, wp = (w[...].astype(BF16) for w in (wm_ref, wd_ref, wp_ref))
    for rows in _row_chunks(o_ref.shape[0]):
        proj = lambda h_ref, w: jnp.dot(h_ref[rows, :].astype(BF16), w, preferred_element_type=F32)
        merged = (g0_ref[rows, :] * proj(hm_ref, wm)
                  + g1_ref[rows, :] * proj(hd_ref, wd)
                  + g2_ref[rows, :] * proj(hp_ref, wp))
        o_ref[rows, :] = merged.astype(o_ref.dtype)


def _merge(hm, hd, hp, wm, wd, wp, gates, *, layer, tm, tn):
    m = hm.shape[0]
    tm = min(tm, m)
    nj = D_MODEL // tn
    row = lambda width: pl.BlockSpec((tm, width), lambda i, j: (i, 0))
    col = lambda depth: pl.BlockSpec((None, depth, tn), lambda i, j: (layer, 0, j))
    gate = lambda br: pl.BlockSpec((tm, tn), lambda i, j: (i, br * nj + j))
    return pl.pallas_call(
        _merge_kernel,
        grid=(m // tm, nj),
        in_specs=[row(MLSTM_V_W), row(DIFF_V_W), row(POOL_CH),
                  col(MLSTM_V_W), col(DIFF_V_W), col(POOL_CH),
                  gate(0), gate(1), gate(2)],
        out_specs=pl.BlockSpec((tm, tn), lambda i, j: (i, j)),
        out_shape=jax.ShapeDtypeStruct((m, D_MODEL), BF16),
        compiler_params=_cparams(("parallel", "arbitrary")),
        name="merge",
    )(hm, hd, hp, wm, wd, wp, gates, gates, gates)


def _conv_gate(ug, uv, pg, pv, cw_g, cw_v, cb_g, cb_v):
    cg = cb_g + (cw_g[0:1] * pg[0] + cw_g[1:2] * pg[1] + cw_g[2:3] * ug)
    cv = cb_v + (cw_v[0:1] * pv[0] + cw_v[1:2] * pv[1] + cw_v[2:3] * uv)
    return jax.nn.silu(cg) * cv


def _ffn_up_prompt_kernel(a_ref, wg_ref, wv_ref, cwg_ref, cwv_ref, cbg_ref, cbv_ref,
                          act_ref, tail_ref, carry_ref, *, tiles_per_seq):
    i = pl.program_id(0)
    j = pl.program_id(1)
    tm = a_ref.shape[0]

    @pl.when(i % tiles_per_seq == 0)
    def _():
        carry_ref[:, pl.ds(j, 1)] = jnp.zeros((2, 1) + carry_ref.shape[2:], F32)

    wg = wg_ref[...].astype(BF16)
    wv = wv_ref[...].astype(BF16)
    tails = [carry_ref[0, j], carry_ref[1, j]]
    r8 = lax.broadcasted_iota(jnp.int32, (SUBLANES, wg.shape[1]), 0)

    def back(u, tail8):
        out = []
        for k in (2, 1):
            rolled = pltpu.roll(u, k, 0)
            head = rolled[:SUBLANES]
            for r in range(k):
                head = jnp.where(r8 == r, tail8[SUBLANES - k + r:SUBLANES - k + r + 1], head)
            out.append(jnp.concatenate([head, rolled[SUBLANES:]], axis=0))
        return tuple(out)

    for rows in _row_chunks(tm):
        a = a_ref[rows, :]
        ug = jnp.dot(a, wg, preferred_element_type=F32)
        uv = jnp.dot(a, wv, preferred_element_type=F32)
        act = _conv_gate(ug, uv, back(ug, tails[0]), back(uv, tails[1]),
                         cwg_ref[...], cwv_ref[...], cbg_ref[...], cbv_ref[...])
        act_ref[rows, :] = act.astype(act_ref.dtype)
        tails = [ug[ug.shape[0] - SUBLANES:], uv[uv.shape[0] - SUBLANES:]]
    for half in range(2):
        carry_ref[half, j] = tails[half]
        tail_ref[0, half] = tails[half]


def _ffn_up_prompt(h, w_up, conv_w, conv_b, *, layer, batch, tm, tn):
    m, d = h.shape
    nj = D_FF // tn
    tiles_per_seq = (m // batch) // tm
    conv_b = conv_b.reshape(1, 2 * D_FF)
    act, tail = pl.pallas_call(
        functools.partial(_ffn_up_prompt_kernel, tiles_per_seq=tiles_per_seq),
        grid=(m // tm, nj),
        in_specs=[pl.BlockSpec((tm, d), lambda i, j: (i, 0)),
                  pl.BlockSpec((None, d, tn), lambda i, j: (layer, 0, j)),
                  pl.BlockSpec((None, d, tn), lambda i, j: (layer, 0, nj + j)),
                  pl.BlockSpec((CONV_W, tn), lambda i, j: (0, j)),
                  pl.BlockSpec((CONV_W, tn), lambda i, j: (0, nj + j)),
                  pl.BlockSpec((1, tn), lambda i, j: (0, j)),
                  pl.BlockSpec((1, tn), lambda i, j: (0, nj + j))],
        out_specs=[pl.BlockSpec((tm, tn), lambda i, j: (i, j)),
                   pl.BlockSpec((1, 2, SUBLANES, tn), lambda i, j: (i, 0, 0, j))],
        out_shape=[jax.ShapeDtypeStruct((m, D_FF), BF16),
                   jax.ShapeDtypeStruct((m // tm, 2, SUBLANES, D_FF), F32)],
        scratch_shapes=[pltpu.VMEM((2, nj, SUBLANES, tn), F32)],
        compiler_params=_cparams(("arbitrary", "arbitrary")),
        name="ffn_up_prompt",
    )(h, w_up, w_up, conv_w, conv_w, conv_b, conv_b)
    tail = tail[tiles_per_seq - 1::tiles_per_seq]
    conv_new = jnp.transpose(tail[:, :, SUBLANES - (CONV_W - 1):, :], (0, 2, 1, 3))
    return act, conv_new.reshape(batch, CONV_W - 1, 2 * D_FF)


def _ffn_up_sample_kernel(a_ref, wg_ref, wv_ref, cwg_ref, cwv_ref, cbg_ref, cbv_ref,
                          p2g_ref, p1g_ref, p2v_ref, p1v_ref, act_ref, ug_ref, uv_ref):
    a = a_ref[...].astype(BF16)
    ug = jnp.dot(a, wg_ref[...].astype(BF16), preferred_element_type=F32)
    uv = jnp.dot(a, wv_ref[...].astype(BF16), preferred_element_type=F32)
    act = _conv_gate(ug, uv, (p2g_ref[...], p1g_ref[...]), (p2v_ref[...], p1v_ref[...]),
                     cwg_ref[...], cwv_ref[...], cbg_ref[...], cbv_ref[...])
    act_ref[...] = act.astype(act_ref.dtype)
    ug_ref[...] = ug
    uv_ref[...] = uv


def _ffn_up_sample(h, w_up, conv_w, conv_b, conv_buf, *, layer, tn):
    b, d = h.shape
    nj = D_FF // tn
    conv_b = conv_b.reshape(1, 2 * D_FF)
    prev2 = conv_buf[:, 0, :]
    prev1 = conv_buf[:, 1, :]
    lo = lambda rows: pl.BlockSpec((rows, tn), lambda j: (0, j))
    hi = lambda rows: pl.BlockSpec((rows, tn), lambda j: (0, nj + j))
    act, ug, uv = pl.pallas_call(
        _ffn_up_sample_kernel,
        grid=(nj,),
        in_specs=[pl.BlockSpec((b, d), lambda j: (0, 0)),
                  pl.BlockSpec((None, d, tn), lambda j: (layer, 0, j)),
                  pl.BlockSpec((None, d, tn), lambda j: (layer, 0, nj + j)),
                  lo(CONV_W), hi(CONV_W), lo(1), hi(1),
                  lo(b), lo(b), hi(b), hi(b)],
        out_specs=[lo(b), lo(b), lo(b)],
        out_shape=[jax.ShapeDtypeStruct((b, D_FF), BF16),
                   jax.ShapeDtypeStruct((b, D_FF), F32),
                   jax.ShapeDtypeStruct((b, D_FF), F32)],
        compiler_params=_cparams(("arbitrary",)),
        name="ffn_up_sample",
    )(h, w_up, w_up, conv_w, conv_w, conv_b, conv_b, prev2, prev1, prev2, prev1)
    u = jnp.concatenate([ug, uv], axis=1)
    return act, jnp.concatenate([conv_buf[:, 1:], u[:, None, :]], axis=1)


def _log_sigmoid(x):
    return -(jnp.maximum(-x, 0.0) + jnp.log1p(jnp.exp(-jnp.abs(x))))


def _cumsum_rows(x):
    n = x.shape[0]
    row = lax.broadcasted_iota(jnp.int32, x.shape, 0)
    d = 1
    while d < n:
        x = x + jnp.where(row >= d, pltpu.roll(x, d, 0), 0.0)
        d *= 2
    return x


def _head_out_norm(h, g, o_gate):
    y = h * lax.rsqrt(jnp.mean(h * h, axis=-1, keepdims=True) + RMS_EPS)
    return (y * g) * jax.nn.sigmoid(o_gate)


def _mlstm_prompt_kernel(qkv_ref, mif_ref, mo_ref, nw_ref, hm_ref, c_out, n_out, m_out,
                         c_sc, n_sc, m_sc, *, cs, nc):
    ci = pl.program_id(0)

    @pl.when(ci == 0)
    def _():
        c_sc[...] = jnp.zeros(c_sc.shape, F32)
        n_sc[...] = jnp.zeros(n_sc.shape, F32)
        m_sc[...] = jnp.zeros(m_sc.shape, F32)

    row = lax.broadcasted_iota(jnp.int32, (cs, cs), 0)
    col = lax.broadcasted_iota(jnp.int32, (cs, cs), 1)
    causal = col <= row
    eye = col == row
    lane = lax.broadcasted_iota(jnp.int32, (1, LANES), 1)
    for bi in range(qkv_ref.shape[0]):
        m_vec = _mlstm_chunk(bi, qkv_ref, mif_ref, mo_ref, nw_ref, hm_ref, c_sc, n_sc, m_sc,
                             causal, eye, lane, cs)

        @pl.when(ci == nc - 1)
        def _():
            m_out[bi] = m_vec

    @pl.when(ci == nc - 1)
    def _():
        c_out[...] = c_sc[...]
        n_out[...] = n_sc[...][:, :, 0, :]


def _mlstm_chunk(bi, qkv_ref, mif_ref, mo_ref, nw_ref, hm_ref, c_sc, n_sc, m_sc, causal, eye, lane, cs):
    mif = mif_ref[bi]
    b_all = _cumsum_rows(_log_sigmoid(mif))
    m_vec = jnp.zeros((1, LANES), F32)
    for h in range(MLSTM_HEADS):
        q = qkv_ref[bi, :, h * MLSTM_DQK:(h + 1) * MLSTM_DQK]
        k = qkv_ref[bi, :, MLSTM_QK_W + h * MLSTM_DQK:MLSTM_QK_W + (h + 1) * MLSTM_DQK] * (MLSTM_DQK ** -0.5)
        v = qkv_ref[bi, :, 2 * MLSTM_QK_W + h * MLSTM_DV:2 * MLSTM_QK_W + (h + 1) * MLSTM_DV]
        li = mif[:, h:h + 1]
        b = b_all[:, MLSTM_HEADS + h:MLSTM_HEADS + h + 1]
        g_row = jnp.sum(jnp.where(eye, li - b, 0.0), axis=0, keepdims=True)
        dmat = jnp.where(causal, b + g_row, NEG)
        m_prev = m_sc[bi, h]
        inter = b + m_prev
        m_t = jnp.maximum(inter, jnp.max(dmat, axis=-1, keepdims=True))
        w_inter = jnp.exp(inter - m_t)
        qb, kb, vb = q.astype(BF16), k.astype(BF16), v.astype(BF16)
        s = lax.dot_general(qb, kb, (((1,), (1,)), ((), ())), preferred_element_type=F32)
        sc = s * jnp.exp(dmat - m_t)
        c_prev = c_sc[bi, h]
        n_prev = n_sc[bi, h]
        num = (w_inter * jnp.dot(qb, c_prev.astype(BF16), preferred_element_type=F32)
               + jnp.dot(sc.astype(BF16), vb, preferred_element_type=F32))
        den = w_inter * jnp.sum(q * n_prev, axis=-1, keepdims=True) + jnp.sum(sc, axis=-1, keepdims=True)
        hh = num / jnp.maximum(jnp.abs(den), jnp.exp(-m_t))
        m_new = m_t[cs - 1:cs]
        b_last = b[cs - 1:cs]
        g_state = jnp.exp(b_last + m_prev - m_new)
        g_tok = jnp.exp(b_last - b + li - m_new)
        kg = k * g_tok
        c_new = g_state * c_prev + lax.dot_general(kg.astype(BF16), vb, (((0,), (0,)), ((), ())),
                                                   preferred_element_type=F32)
        n_new = g_state * n_prev + jnp.sum(kg, axis=0, keepdims=True)
        c_sc[bi, h] = c_new
        n_sc[bi, h] = n_new
        m_sc[bi, h] = m_new
        m_vec = m_vec + jnp.where(lane == h, m_new, 0.0)
        sl = slice(h * MLSTM_DV, (h + 1) * MLSTM_DV)
        hm_ref[bi, :, sl] = _head_out_norm(hh, nw_ref[:, sl], mo_ref[bi, :, sl]).astype(hm_ref.dtype)
    return m_vec


def _mlstm_prompt(mqkv, mif, mo, norm_w, *, batch):
    m = mqkv.shape[0]
    seq = m // batch
    cs = MLSTM_CHUNK if seq % MLSTM_CHUNK == 0 else seq
    nc = seq // cs
    per_seq = lambda a: a.reshape(batch, seq, a.shape[-1])
    rows = lambda width: pl.BlockSpec((batch, cs, width), lambda c: (0, c, 0))
    whole = lambda shape: pl.BlockSpec(shape, lambda c: (0,) * len(shape))
    state_shapes = [(batch, MLSTM_HEADS, MLSTM_DQK, MLSTM_DV), (batch, MLSTM_HEADS, MLSTM_DQK),
                    (batch, 1, LANES)]
    hm, c, n, mm = pl.pallas_call(
        functools.partial(_mlstm_prompt_kernel, cs=cs, nc=nc),
        grid=(nc,),
        in_specs=[rows(mqkv.shape[1]), rows(LANES), rows(MLSTM_V_W), whole((1, MLSTM_V_W))],
        out_specs=[rows(MLSTM_V_W)] + [whole(s) for s in state_shapes],
        out_shape=[jax.ShapeDtypeStruct((batch, seq, MLSTM_V_W), BF16)]
        + [jax.ShapeDtypeStruct(s, F32) for s in state_shapes],
        scratch_shapes=[pltpu.VMEM((batch, MLSTM_HEADS, MLSTM_DQK, MLSTM_DV), F32),
                        pltpu.VMEM((batch, MLSTM_HEADS, 1, MLSTM_DQK), F32),
                        pltpu.VMEM((batch, MLSTM_HEADS, 1, 1), F32)],
        compiler_params=_cparams(("arbitrary",)),
        name="mlstm_prompt",
    )(per_seq(mqkv), per_seq(mif), per_seq(mo), norm_w.reshape(1, MLSTM_V_W))
    return hm.reshape(m, MLSTM_V_W), c, n, mm[:, 0, :MLSTM_HEADS]


def _lanes_to_rows(x_row, eye):
    return jnp.sum(jnp.where(eye, x_row, 0.0), axis=1, keepdims=True)


def _mlstm_sample_kernel(qkv_ref, mif_ref, mo_ref, nw_ref, c_ref, n_ref, m_ref,
                         hm_ref, c_out, n_out, m_out):
    mif = mif_ref[0]
    lf_all = _log_sigmoid(mif)
    m_all = m_ref[0]
    row = lax.broadcasted_iota(jnp.int32, (MLSTM_DQK, MLSTM_DQK), 0)
    col = lax.broadcasted_iota(jnp.int32, (MLSTM_DQK, MLSTM_DQK), 1)
    eye = row == col
    lane = lax.broadcasted_iota(jnp.int32, (1, LANES), 1)
    m_vec = jnp.zeros((1, LANES), F32)
    for h in range(MLSTM_HEADS):
        q = qkv_ref[0, :, h * MLSTM_DQK:(h + 1) * MLSTM_DQK]
        k = qkv_ref[0, :, MLSTM_QK_W + h * MLSTM_DQK:MLSTM_QK_W + (h + 1) * MLSTM_DQK] * (MLSTM_DQK ** -0.5)
        v = qkv_ref[0, :, 2 * MLSTM_QK_W + h * MLSTM_DV:2 * MLSTM_QK_W + (h + 1) * MLSTM_DV]
        li = mif[:, h:h + 1]
        lf = lf_all[:, MLSTM_HEADS + h:MLSTM_HEADS + h + 1]
        m_prev = m_all[:, h:h + 1]
        inter = lf + m_prev
        m_t = jnp.maximum(inter, li)
        w_inter = jnp.exp(inter - m_t)
        sc = jnp.sum(q * k, axis=-1, keepdims=True) * jnp.exp(li - m_t)
        c_prev = c_ref[0, h]
        n_prev = n_ref[0, h:h + 1, :]
        q_col = _lanes_to_rows(q, eye)
        k_col = _lanes_to_rows(k, eye)
        num = w_inter * jnp.sum(q_col * c_prev, axis=0, keepdims=True) + sc * v
        den = w_inter * jnp.sum(q * n_prev, axis=-1, keepdims=True) + sc
        hh = num / jnp.maximum(jnp.abs(den), jnp.exp(-m_t))
        g_tok = jnp.exp(li - m_t)
        c_out[0, h] = w_inter * c_prev + (g_tok * k_col) * v
        n_out[0, h:h + 1, :] = w_inter * n_prev + g_tok * k
        m_vec = m_vec + jnp.where(lane == h, m_t, 0.0)
        sl = slice(h * MLSTM_DV, (h + 1) * MLSTM_DV)
        hm_ref[0, :, sl] = _head_out_norm(hh, nw_ref[:, sl], mo_ref[0, :, sl]).astype(hm_ref.dtype)
    m_out[0] = m_vec


def _mlstm_sample(mqkv, mif, mo, norm_w, c0, n0, m0):
    b = mqkv.shape[0]
    m0p = jnp.pad(m0, ((0, 0), (0, LANES - MLSTM_HEADS))).reshape(b, 1, LANES)
    per_seq = lambda width: pl.BlockSpec((1, 1, width), lambda i: (i, 0, 0))
    c_spec = pl.BlockSpec((1, MLSTM_HEADS, MLSTM_DQK, MLSTM_DV), lambda i: (i, 0, 0, 0))
    n_spec = pl.BlockSpec((1, MLSTM_HEADS, MLSTM_DQK), lambda i: (i, 0, 0))
    hm, c, n, mm = pl.pallas_call(
        _mlstm_sample_kernel,
        grid=(b,),
        in_specs=[per_seq(mqkv.shape[1]), per_seq(LANES), per_seq(MLSTM_V_W),
                  pl.BlockSpec((1, MLSTM_V_W), lambda i: (0, 0)),
                  c_spec, n_spec, per_seq(LANES)],
        out_specs=[per_seq(MLSTM_V_W), c_spec, n_spec, per_seq(LANES)],
        out_shape=[jax.ShapeDtypeStruct((b, 1, MLSTM_V_W), F32),
                   jax.ShapeDtypeStruct(c0.shape, F32),
                   jax.ShapeDtypeStruct(n0.shape, F32),
                   jax.ShapeDtypeStruct((b, 1, LANES), F32)],
        compiler_params=_cparams(("parallel",)),
        name="mlstm_sample",
    )(mqkv.reshape(b, 1, -1), mif.reshape(b, 1, LANES), mo.reshape(b, 1, -1),
      norm_w.reshape(1, MLSTM_V_W), c0, n0, m0p)
    return hm.reshape(b, MLSTM_V_W), c, n, mm[:, 0, :MLSTM_HEADS]


def _bucket_by_distance():
    max_exact = REL_BUCKETS // 2
    n = np.arange(BIAS_SPAN + 1)
    large = max_exact + np.floor(np.log(np.maximum(n, 1) / max_exact) / math.log(REL_MAX_DIST / max_exact)
                                 * (REL_BUCKETS - max_exact)).astype(np.int64)
    return np.where(n < max_exact, n, np.minimum(large, REL_BUCKETS - 1))


def _bucket_starts():
    bucket = _bucket_by_distance()
    return [int(np.argmax(bucket >= b)) for b in range(1, REL_BUCKETS)]


def _bias_by_distance(rel_bias):
    tab = jnp.transpose(rel_bias.astype(F32)[_bucket_by_distance()], (1, 0))
    return tab - tab[:, BIAS_SPAN:]


def _lambda(lq_ref, lk_ref, lam_init):
    lq = lq_ref[...]
    lk = lk_ref[...]
    e0 = jnp.exp(jnp.sum(lq[0:1] * lk[0:1], axis=-1, keepdims=True))
    e1 = jnp.exp(jnp.sum(lq[1:2] * lk[1:2], axis=-1, keepdims=True))
    return e0 - e1 + lam_init


def _diff_out_norm(d, sub, lam_init):
    y = d * lax.rsqrt(jnp.mean(d * d, axis=-1, keepdims=True) + RMS_EPS)
    return (y * sub) * (1.0 - lam_init)


def _bias_tiles_kernel(rb_ref, o_ref, *, tq):
    h = pl.program_id(0)
    blk = pl.program_id(1)
    row = lax.broadcasted_iota(jnp.int32, (tq, tq), 0)
    col = lax.broadcasted_iota(jnp.int32, (tq, tq), 1)
    dist = row - col + blk * tq
    far = rb_ref[h, REL_BUCKETS - 1]
    val = jnp.full((tq, tq), rb_ref[h, 0] - far, F32)
    for b, start in enumerate(_bucket_starts(), start=1):
        val = jnp.where(dist >= start, rb_ref[h, b] - far, val)
    o_ref[0, 0] = jnp.where(dist >= 0, val, NEG)


def _bias_tiles(rel_bias, tq):
    assert tq + 1 >= _bucket_starts()[-1]
    return pl.pallas_call(
        functools.partial(_bias_tiles_kernel, tq=tq),
        grid=(DIFF_HEADS, 2),
        in_specs=[pl.BlockSpec(memory_space=pltpu.SMEM)],
        out_specs=pl.BlockSpec((1, 1, tq, tq), lambda h, blk: (h, blk, 0, 0)),
        out_shape=jax.ShapeDtypeStruct((DIFF_HEADS, 2, tq, tq), F32),
        compiler_params=_cparams(("parallel", "parallel")),
        name="bias_tiles",
    )(jnp.transpose(rel_bias.astype(F32)))


def _attn_prompt_kernel(qi_ref, ki_ref, q_ref, k_ref, v_ref, bias_ref, lq_ref, lk_ref, sub_ref, o_ref,
                        m_sc, l_sc, acc_sc, *, lam_init):
    qi = qi_ref[pl.program_id(2)]
    ki = ki_ref[pl.program_id(2)]

    @pl.when(ki == 0)
    def _():
        m_sc[...] = jnp.full(m_sc.shape, NEG, F32)
        l_sc[...] = jnp.zeros(l_sc.shape, F32)
        acc_sc[...] = jnp.zeros(acc_sc.shape, F32)

    def tile(bias):
        v = v_ref[0]
        for mp in range(2):
            sl = slice(mp * DIFF_DQK, (mp + 1) * DIFF_DQK)
            s = lax.dot_general(q_ref[0, :, sl], k_ref[0, :, sl], (((1,), (1,)), ((), ())),
                                preferred_element_type=F32)
            if bias is not None:
                s = s + bias
            m_prev = m_sc[mp]
            m_new = jnp.maximum(m_prev, jnp.max(s, axis=-1, keepdims=True))
            alpha = jnp.exp(m_prev - m_new)
            p = jnp.exp(s - m_new)
            l_sc[mp] = alpha * l_sc[mp] + jnp.sum(p, axis=-1, keepdims=True)
            acc_sc[mp] = alpha * acc_sc[mp] + jnp.dot(p.astype(BF16), v, preferred_element_type=F32)
            m_sc[mp] = m_new

    @pl.when(ki < qi - 1)
    def _():
        tile(None)

    @pl.when((ki >= qi - 1) & (ki <= qi))
    def _():
        tile(bias_ref[0, 0])

    @pl.when(ki == qi)
    def _():
        lam = _lambda(lq_ref, lk_ref, lam_init)
        d = acc_sc[0] / l_sc[0] - lam * (acc_sc[1] / l_sc[1])
        o_ref[0] = _diff_out_norm(d, sub_ref[...], lam_init).astype(o_ref.dtype)


def _attn_prompt(q, k, v, bias, lam_q, lam_k, subln, *, batch, lam_init):
    m = q.shape[0]
    seq = m // batch
    tq = bias.shape[-1]
    assert seq % tq == 0
    nq = seq // tq
    hw = 2 * DIFF_DQK
    q3, k3, v3 = (a.reshape(batch, seq, -1) for a in (q, k, v))
    pairs = [(qi, ki) for qi in range(nq) for ki in range(qi + 1)]
    qi_tab = jnp.asarray([p[0] for p in pairs], jnp.int32)
    ki_tab = jnp.asarray([p[1] for p in pairs], jnp.int32)
    q_spec = pl.BlockSpec((1, tq, hw), lambda b, h, t, qt, kt: (b, qt[t], h))
    kv_spec = pl.BlockSpec((1, tq, hw), lambda b, h, t, qt, kt: (b, kt[t], h))
    const = lambda shape: pl.BlockSpec(shape, lambda b, h, t, qt, kt: (0,) * len(shape))
    out = pl.pallas_call(
        functools.partial(_attn_prompt_kernel, lam_init=lam_init),
        grid_spec=pltpu.PrefetchScalarGridSpec(
            num_scalar_prefetch=2,
            grid=(batch, DIFF_HEADS, len(pairs)),
            in_specs=[q_spec, kv_spec, kv_spec,
                      pl.BlockSpec((1, 1, tq, tq),
                                   lambda b, h, t, qt, kt: (h, jnp.minimum(qt[t] - kt[t], 1), 0, 0)),
                      const((2, DIFF_DQK)), const((2, DIFF_DQK)), const((1, DIFF_DV))],
            out_specs=pl.BlockSpec((1, tq, DIFF_DV), lambda b, h, t, qt, kt: (b, qt[t], h)),
            scratch_shapes=[pltpu.VMEM((2, tq, 1), F32), pltpu.VMEM((2, tq, 1), F32),
                            pltpu.VMEM((2, tq, DIFF_DV), F32)]),
        out_shape=jax.ShapeDtypeStruct((batch, seq, DIFF_V_W), BF16),
        compiler_params=_cparams(("parallel", "parallel", "arbitrary")),
        name="attn_prompt",
    )(qi_tab, ki_tab, q3, k3, v3, bias, lam_q, lam_k, subln.reshape(1, DIFF_DV))
    return out.reshape(m, DIFF_V_W)


N_MAPS = 2 * DIFF_HEADS
PAGE_ROWS = PAGE_SIZE * DIFF_HEADS
NEW_TOKENS = 16
DECODE_PAGES_PER_STEP = 8


def _attn_sample_kernel(pt_ref, qm_ref, kn_ref, vn_ref, *rest, n_steps, pages, lam_init):
    kc_refs, vc_refs = rest[:pages], rest[pages:2 * pages]
    (bfar_ref, blast_ref, bnew_ref, lq_ref, lk_ref, sub_ref, o_ref,
     qx_sc, m_sc, l_sc, acc_sc) = rest[2 * pages:]
    p = pl.program_id(1)
    sub_i = lax.broadcasted_iota(jnp.int32, (DIFF_HEADS, LANES), 0)
    lane_i = lax.broadcasted_iota(jnp.int32, (DIFF_HEADS, LANES), 1)
    own = (lane_i == 2 * sub_i) | (lane_i == 2 * sub_i + 1)

    @pl.when(p == 0)
    def _():
        qm = qm_ref[0]
        r = lax.broadcasted_iota(jnp.int32, qm.shape, 0)
        first = (r & 1) == 0
        qx = jnp.concatenate([jnp.where(first, qm, 0.0), jnp.where(first, 0.0, qm)], axis=1)
        qx_sc[...] = jnp.zeros(qx_sc.shape, BF16)
        qx_sc[0:N_MAPS, :] = qx.astype(BF16)
        m_sc[...] = jnp.where(own, NEG, 0.0)
        l_sc[...] = jnp.zeros(l_sc.shape, F32)
        acc_sc[...] = jnp.zeros(acc_sc.shape, F32)

    row = lax.broadcasted_iota(jnp.int32, (LANES, LANES), 0)
    col = lax.broadcasted_iota(jnp.int32, (LANES, LANES), 1)
    eye = row == col

    def per_lane_column(x):
        x_row = jnp.sum(jnp.where(own, x, 0.0), axis=0, keepdims=True)
        return jnp.sum(jnp.where(eye, x_row, 0.0), axis=1, keepdims=True)

    def group_update(blocks):
        scores = []
        m_prev = m_sc[...]
        m_new = m_prev
        for kf, _, bias3 in blocks:
            s_all = lax.dot_general(kf.astype(BF16), qx_sc[...], (((1,), (1,)), ((), ())),
                                    preferred_element_type=F32)
            s3 = s_all.reshape(kf.shape[0] // DIFF_HEADS, DIFF_HEADS, LANES) + bias3
            m_new = jnp.maximum(m_new, jnp.max(s3, axis=0))
            scores.append(s3)
        alpha = jnp.exp(m_prev - m_new)
        l_new = alpha * l_sc[...]
        pv = None
        for s3, (_, vf, _) in zip(scores, blocks):
            p3 = jnp.exp(s3 - m_new[None])
            l_new = l_new + jnp.sum(p3, axis=0)
            p_t = p3.reshape(vf.shape[0], LANES).T.astype(BF16)
            part = jnp.dot(p_t, vf.astype(BF16), preferred_element_type=F32)
            pv = part if pv is None else pv + part
        l_sc[...] = l_new
        m_sc[...] = m_new
        acc_sc[...] = per_lane_column(alpha) * acc_sc[...] + pv

    far = bfar_ref[...][None]

    @pl.when(p < n_steps - 1)
    def _():
        group_update([(kc[...], vc[...], far) for kc, vc in zip(kc_refs, vc_refs)])

    @pl.when(p == n_steps - 1)
    def _():
        biases = [far] * (pages - 1) + [blast_ref[...].reshape(PAGE_SIZE, DIFF_HEADS, LANES)]
        blocks = [(kc[...], vc[...], b3) for kc, vc, b3 in zip(kc_refs, vc_refs, biases)]
        blocks.append((kn_ref[0], vn_ref[0], bnew_ref[...].reshape(NEW_TOKENS, DIFF_HEADS, LANES)))
        group_update(blocks)
        lam = _lambda(lq_ref, lk_ref, lam_init)
        l_col = per_lane_column(l_sc[...])
        out = acc_sc[0:N_MAPS, :] / l_col[0:N_MAPS]
        for h in range(DIFF_HEADS):
            d = out[2 * h:2 * h + 1] - lam * out[2 * h + 1:2 * h + 2]
            sl = slice(h * DIFF_DV, (h + 1) * DIFF_DV)
            o_ref[0, :, sl] = _diff_out_norm(d, sub_ref[...], lam_init).astype(o_ref.dtype)


def _attn_sample(q, k_new, v_new, cache_k, cache_v, layer, page_table, bias_tab, lam_q, lam_k, subln,
                 *, lam_init):
    b, n_pages = page_table.shape
    assert PAGE_SIZE >= BIAS_SPAN
    pages = DECODE_PAGES_PER_STEP if n_pages % DECODE_PAGES_PER_STEP == 0 else 1
    n_steps = n_pages // pages
    hw = 2 * DIFF_DQK
    sub_i = lax.broadcasted_iota(jnp.int32, (DIFF_HEADS, LANES), 0)
    lane_i = lax.broadcasted_iota(jnp.int32, (DIFF_HEADS, LANES), 1)
    own = (lane_i == 2 * sub_i) | (lane_i == 2 * sub_i + 1)

    def rows_bias(per_token):
        t = per_token.shape[0]
        return jnp.where(own[None], per_token[:, :, None], NEG).reshape(t * DIFF_HEADS, LANES)

    b_far = jnp.where(own, 0.0, NEG)
    b_last = rows_bias(jnp.transpose(bias_tab[:, :0:-1][:, :PAGE_SIZE]))
    b_new = rows_bias(jnp.full((NEW_TOKENS, DIFF_HEADS), NEG, F32).at[0].set(bias_tab[:, 0]))
    pad_page = lambda a: jnp.pad(a.reshape(b, DIFF_HEADS, hw), ((0, 0), (0, (NEW_TOKENS - 1) * DIFF_HEADS), (0, 0)))
    new_spec = pl.BlockSpec((1, NEW_TOKENS * DIFF_HEADS, hw), lambda i, p, pt: (i, 0, 0))
    page = lambda r: pl.BlockSpec((None, None, PAGE_ROWS, hw),
                                  lambda i, p, pt: (layer, pt[i * n_pages + p * pages + r], 0, 0))
    const = lambda shape: pl.BlockSpec(shape, lambda i, p, pt: (0,) * len(shape))
    out = pl.pallas_call(
        functools.partial(_attn_sample_kernel, n_steps=n_steps, pages=pages, lam_init=lam_init),
        grid_spec=pltpu.PrefetchScalarGridSpec(
            num_scalar_prefetch=1,
            grid=(b, n_steps),
            in_specs=[pl.BlockSpec((1, N_MAPS, DIFF_DQK), lambda i, p, pt: (i, 0, 0)),
                      new_spec, new_spec, *[page(r) for r in range(pages)], *[page(r) for r in range(pages)],
                      const((DIFF_HEADS, LANES)), const((PAGE_ROWS, LANES)),
                      const((NEW_TOKENS * DIFF_HEADS, LANES)),
                      const((2, DIFF_DQK)), const((2, DIFF_DQK)), const((1, DIFF_DV))],
            out_specs=pl.BlockSpec((1, 1, DIFF_V_W), lambda i, p, pt: (i, 0, 0)),
            scratch_shapes=[pltpu.VMEM((LANES, hw), BF16),
                            pltpu.VMEM((DIFF_HEADS, LANES), F32), pltpu.VMEM((DIFF_HEADS, LANES), F32),
                            pltpu.VMEM((LANES, DIFF_DV), F32)]),
        out_shape=jax.ShapeDtypeStruct((b, 1, DIFF_V_W), F32),
        compiler_params=_cparams(("parallel", "arbitrary")),
        name="attn_sample",
    )(page_table.reshape(-1), q.reshape(b, N_MAPS, DIFF_DQK), pad_page(k_new), pad_page(v_new),
      *[cache_k] * pages, *[cache_v] * pages, b_far, b_last, b_new, lam_q, lam_k, subln.reshape(1, DIFF_DV))
    return out.reshape(b, DIFF_V_W)


HIST = 16


def _window_sum(ext, w, n):
    s = ext
    d = 1
    while d < w:
        s = s + pltpu.roll(s, d, 0)
        d *= 2
    return s[HIST:HIST + n]


def _pool_prompt_kernel(u_ref, w_ref, sc_ref, o_ref, hist_out, hist_sc, *, tm):
    t = pl.program_id(1)

    @pl.when(t == 0)
    def _():
        hist_sc[...] = jnp.zeros(hist_sc.shape, F32)

    pos = t * tm + lax.broadcasted_iota(jnp.int32, (tm, 1), 0)
    for gi, w in enumerate(POOL_WINDOWS):
        sl = slice(gi * POOL_GC, (gi + 1) * POOL_GC)
        u = u_ref[:, sl]
        ext = jnp.concatenate([hist_sc[:, sl], u], axis=0)
        cnt = jnp.minimum(w, pos + 1).astype(F32)
        pooled = _window_sum(ext, w, tm) / cnt - u
        mixed = jnp.dot(pooled.astype(BF16), w_ref[gi], preferred_element_type=F32)
        o_ref[:, sl] = (mixed * sc_ref[:, sl]).astype(o_ref.dtype)
        new_hist = u[tm - HIST:]
        hist_sc[:, sl] = new_hist
        hist_out[0, :, sl] = new_hist


def _pool_prompt(u, pool_w, pool_scale, *, batch, tm):
    m = u.shape[0]
    seq = m // batch
    tm = min(tm, seq)
    assert tm >= HIST
    tps = seq // tm
    hp, hist = pl.pallas_call(
        functools.partial(_pool_prompt_kernel, tm=tm),
        grid=(batch, tps),
        in_specs=[pl.BlockSpec((tm, POOL_CH), lambda b, t: (b * tps + t, 0)),
                  pl.BlockSpec((POOL_GROUPS, POOL_GC, POOL_GC), lambda b, t: (0, 0, 0)),
                  pl.BlockSpec((1, POOL_CH), lambda b, t: (0, 0))],
        out_specs=[pl.BlockSpec((tm, POOL_CH), lambda b, t: (b * tps + t, 0)),
                   pl.BlockSpec((1, HIST, POOL_CH), lambda b, t: (b, 0, 0))],
        out_shape=[jax.ShapeDtypeStruct((m, POOL_CH), BF16),
                   jax.ShapeDtypeStruct((batch, HIST, POOL_CH), F32)],
        scratch_shapes=[pltpu.VMEM((HIST, POOL_CH), F32)],
        compiler_params=_cparams(("parallel", "arbitrary")),
        name="pool_prompt",
    )(u, pool_w, pool_scale.reshape(1, POOL_CH))
    return hp, hist[:, HIST - POOL_BUF:]


def _pool_sample_kernel(ucat_ref, w_ref, sc_ref, o_ref):
    for gi, w in enumerate(POOL_WINDOWS):
        sl = slice(gi * POOL_GC, (gi + 1) * POOL_GC)
        new = ucat_ref[HIST - 1, :, sl]
        win = new
        for r in range(HIST - w, HIST - 1):
            win = win + ucat_ref[r, :, sl]
        pooled = win / float(w) - new
        mixed = jnp.dot(pooled.astype(BF16), w_ref[gi], preferred_element_type=F32)
        o_ref[:, sl] = (mixed * sc_ref[:, sl]).astype(o_ref.dtype)


def _pool_sample(u, buf, pool_w, pool_scale):
    b = u.shape[0]
    assert buf.shape[1] == HIST - 1
    ucat = jnp.concatenate([buf, u[:, None, :]], axis=1)
    hp = pl.pallas_call(
        _pool_sample_kernel,
        out_shape=jax.ShapeDtypeStruct((b, POOL_CH), F32),
        compiler_params=pltpu.CompilerParams(vmem_limit_bytes=VMEM_LIMIT),
        name="pool_sample",
    )(jnp.transpose(ucat, (1, 0, 2)), pool_w, pool_scale.reshape(1, POOL_CH))
    return hp, ucat[:, 1:]


IN_SPLITS = (MLSTM_QK_W, MLSTM_QK_W, MLSTM_V_W, 2 * MLSTM_HEADS, MLSTM_V_W,
             DIFF_QK_W, DIFF_QK_W, DIFF_V_W, POOL_CH, N_BRANCH * D_MODEL)
IN_OFFS = tuple(int(o) for o in np.cumsum((0,) + IN_SPLITS))
GATE_COLS = (IN_OFFS[3], IN_OFFS[4])


def _prep_weights(p):
    w_in = p["w_in"]
    g0, g1 = GATE_COLS
    pad = ((0, 0), (0, 0), (0, LANES - (g1 - g0)))
    rows = lambda w: w.reshape(-1, w.shape[-1])
    return dict(w_head=rows(w_in[:, :, :g0]), w_tail=rows(w_in[:, :, g1:].astype(BF16)),
                w_mif=rows(jnp.pad(w_in[:, :, g0:g1], pad)),
                w_down=rows(p["w_down"].astype(BF16)))


def _prep_layer(p, l):
    b_in = p["b_in"][l]
    g0, g1 = GATE_COLS
    pad = ((0, 0), (0, LANES - (g1 - g0)))
    return dict(
        b_tail=b_in[g1:].reshape(1, -1), b_mif=jnp.pad(b_in[g0:g1].reshape(1, -1), pad),
        b_head=b_in[:g0].reshape(1, -1),
        q_norm=jnp.tile(p["q_norm"][l].reshape(1, -1), (1, DIFF_HEADS)),
        k_norm=jnp.tile(p["k_norm"][l].reshape(1, -1), (1, DIFF_HEADS)),
        norm_mix=p["norm_mix"][l], norm_ffn=p["norm_ffn"][l], mlstm_norm=p["mlstm_norm"][l],
        lambda_q=p["lambda_q"][l], lambda_k=p["lambda_k"][l], diff_subln=p["diff_subln"][l],
        pool_w=p["pool_w"][l].astype(BF16), pool_scale=p["pool_scale"][l],
        conv_w=p["conv_w"][l], conv_b=p["conv_b"][l],
    )


def _projections(h, wts, lw, l, *, tm, tn):
    g1 = GATE_COLS[1]

    def tail(seg, epilogue, extras, out_dtypes, name):
        c0, c1 = IN_OFFS[seg] - g1, IN_OFFS[seg + 1] - g1
        return _matmul(h, wts["w_tail"], (lw["b_tail"][:, c0:c1],) + extras, epilogue, out_dtypes,
                       layer=l, tm=tm, tn=tn, col0=c0, n=c1 - c0, name=name)

    out = {}
    (out["mqkv"],) = _matmul(h, wts["w_head"], (lw["b_head"],), _ep_bias, (F32,), layer=l, tm=tm, tn=tn,
                             name="proj_mqkv")
    (out["mif"],) = _matmul(h, wts["w_mif"], (lw["b_mif"],), _ep_bias, (F32,), layer=l, tm=tm, tn=tn,
                            name="proj_mif")
    (out["mo"],) = tail(4, _ep_bias, (), (F32,), "proj_mo")
    out["dq"], out["dq16"] = tail(5, _ep_qnorm_scaled, (lw["q_norm"],), (F32, BF16), "proj_dq")
    out["dk"], out["dk16"] = tail(6, _ep_knorm, (lw["k_norm"],), (F32, BF16), "proj_dk")
    out["dv"], out["dv16"] = tail(7, _ep_bias_bf16copy, (), (F32, BF16), "proj_dv")
    (out["pu"],) = tail(8, _ep_bias, (), (F32,), "proj_pu")
    (out["gates"],) = tail(9, _ep_bias_sigmoid, (), (F32,), "proj_gates")
    return out


ATTN_TQ = 512
TILES_PROMPT = dict(norm=512, proj=(1024, 512), merge=(1024, 256), out=(1024, 512),
                    ffn_up=(1024, 256), ffn_down=(512, 256), pool=512)
TILES_SAMPLE = dict(proj=512, merge=256, out=512, ffn_up=256, ffn_down=256)


def _layer_prompt(x, p, wts, lw, bias_tiles, l, *, batch):
    m = x.shape[0]
    t = TILES_PROMPT
    lam_init = 0.8 - 0.6 * math.exp(-0.3 * l)
    h = _rmsnorm(x, lw["norm_mix"], tm=t["norm"])
    pr = _projections(h, wts, lw, l, tm=t["proj"][0], tn=t["proj"][1])
    hm, c1, n1, m1 = _mlstm_prompt(pr["mqkv"], pr["mif"], pr["mo"], lw["mlstm_norm"], batch=batch)
    hd = _attn_prompt(pr["dq16"], pr["dk16"], pr["dv16"], bias_tiles, lw["lambda_q"], lw["lambda_k"],
                      lw["diff_subln"], batch=batch, lam_init=lam_init)
    hp, pool_new = _pool_prompt(pr["pu"], lw["pool_w"], lw["pool_scale"], batch=batch, tm=t["pool"])
    merged = _merge(hm, hd, hp, p["w_br_mlstm"], p["w_br_diff"], p["w_br_pool"], pr["gates"],
                    layer=l, tm=t["merge"][0], tn=t["merge"][1])
    (x1,) = _matmul(merged, p["w_out"], (x,), _ep_residual, (F32,), layer=l, tm=t["out"][0], tn=t["out"][1],
                    name="out_proj")
    h2 = _rmsnorm(x1, lw["norm_ffn"], tm=t["norm"])
    act, conv_new = _ffn_up_prompt(h2, p["w_up"], lw["conv_w"], lw["conv_b"], layer=l, batch=batch,
                                   tm=t["ffn_up"][0], tn=t["ffn_up"][1])
    (x2,) = _matmul(act, wts["w_down"], (x1,), _ep_residual, (F32,), layer=l, tm=t["ffn_down"][0],
                    tn=t["ffn_down"][1], name="ffn_down")
    seq = m // batch
    kv_shape = (batch, seq, DIFF_HEADS, DIFF_DV)
    return x2, pr["dk"].reshape(kv_shape), pr["dv"].reshape(kv_shape), c1, n1, m1, pool_new, conv_new


def _layer_sample(x, p, wts, lw, bias_tab, l, cache_k, cache_v, page_table, c0, n0, m0, pool_buf, conv_buf):
    b = x.shape[0]
    t = TILES_SAMPLE
    lam_init = 0.8 - 0.6 * math.exp(-0.3 * l)
    h = _rmsnorm(x, lw["norm_mix"], tm=b)
    pr = _projections(h, wts, lw, l, tm=b, tn=t["proj"])
    hm, c1, n1, m1 = _mlstm_sample(pr["mqkv"], pr["mif"], pr["mo"], lw["mlstm_norm"], c0, n0, m0)
    hd = _attn_sample(pr["dq"], pr["dk"], pr["dv"], cache_k, cache_v, l, page_table, bias_tab,
                      lw["lambda_q"], lw["lambda_k"], lw["diff_subln"], lam_init=lam_init)
    hp, pool_new = _pool_sample(pr["pu"], pool_buf, lw["pool_w"], lw["pool_scale"])
    merged = _merge(hm, hd, hp, p["w_br_mlstm"], p["w_br_diff"], p["w_br_pool"], pr["gates"],
                    layer=l, tm=b, tn=t["merge"])
    (x1,) = _matmul(merged, p["w_out"], (x,), _ep_residual, (F32,), layer=l, tm=b, tn=t["out"],
                    name="out_proj_s")
    h2 = _rmsnorm(x1, lw["norm_ffn"], tm=b)
    act, conv_new = _ffn_up_sample(h2, p["w_up"], lw["conv_w"], lw["conv_b"], conv_buf, layer=l, tn=t["ffn_up"])
    (x2,) = _matmul(act, wts["w_down"], (x1,), _ep_residual, (F32,), layer=l, tm=b, tn=t["ffn_down"],
                    name="ffn_down_s")
    kv_shape = (b, 1, DIFF_HEADS, DIFF_DV)
    return x2, pr["dk"].reshape(kv_shape), pr["dv"].reshape(kv_shape), c1, n1, m1, pool_new, conv_new


def kernel(x_prompt, x_sample, cache_k, cache_v, page_table, state_mlstm_c, state_mlstm_n, state_mlstm_m,
           state_pool, state_conv, rel_bias, norm_mix, w_in, b_in, mlstm_norm, q_norm, k_norm, lambda_q,
           lambda_k, diff_subln, pool_w, pool_scale, w_br_mlstm, w_br_diff, w_br_pool, w_out, norm_ffn,
           w_up, conv_w, conv_b, w_down):
    params = dict(norm_mix=norm_mix, w_in=w_in, b_in=b_in, mlstm_norm=mlstm_norm, q_norm=q_norm,
                  k_norm=k_norm, lambda_q=lambda_q, lambda_k=lambda_k, diff_subln=diff_subln,
                  pool_w=pool_w, pool_scale=pool_scale, w_br_mlstm=w_br_mlstm, w_br_diff=w_br_diff,
                  w_br_pool=w_br_pool, w_out=w_out, norm_ffn=norm_ffn, w_up=w_up, conv_w=conv_w,
                  conv_b=conv_b, w_down=w_down)
    depth = w_in.shape[0]
    n_prompt, seq, d = x_prompt.shape
    n_dec = x_sample.shape[0]
    assert x_sample.shape[1] == 1, "sample group decodes one token per sequence"
    bias_tab = _bias_by_distance(rel_bias)
    bias_tiles = _bias_tiles(rel_bias, min(ATTN_TQ, seq))
    ck = cache_k.reshape(cache_k.shape[:2] + (PAGE_ROWS, 2 * DIFF_DQK))
    cv = cache_v.reshape(cache_v.shape[:2] + (PAGE_ROWS, DIFF_DV))
    xp = x_prompt.reshape(n_prompt * seq, d)
    xs = x_sample.reshape(n_dec, d)
    outs_p, outs_s = [], []
    wts = _prep_weights(params)
    for l in range(depth):
        lw = _prep_layer(params, l)
        xp, *rest_p = _layer_prompt(xp, params, wts, lw, bias_tiles, l, batch=n_prompt)
        xs, *rest_s = _layer_sample(xs, params, wts, lw, bias_tab, l, ck, cv, page_table, state_mlstm_c[l],
                                    state_mlstm_n[l], state_mlstm_m[l], state_pool[l], state_conv[l])
        outs_p.append(rest_p)
        outs_s.append(rest_s)
    stack = lambda outs, idx: jnp.stack([o[idx] for o in outs])
    kp, vp, cp, np_, mp, pp, cvp = (stack(outs_p, i) for i in range(7))
    ks, vs, cs_, ns, ms, ps, cvs = (stack(outs_s, i) for i in range(7))
    return (xp.reshape(n_prompt, seq, d), xs.reshape(n_dec, 1, d),
            kp, vp, ks, vs, cp, np_, mp, cs_, ns, ms, pp, ps, cvp, cvs)
```

```python
import functools
import math

import jax
import jax.numpy as jnp
import numpy as np
from jax import lax
from jax.experimental import pallas as pl
from jax.experimental.pallas import tpu as pltpu

F32 = jnp.float32
BF16 = jnp.bfloat16

D_MODEL = 4096
PAGE_SIZE = 128
MLSTM_HEADS = 4
MLSTM_DV = 256
MLSTM_DQK = 128
MLSTM_CHUNK = 64
DIFF_HEADS = 8
DIFF_DV = 256
DIFF_DQK = 128
POOL_GROUPS = 4
POOL_WINDOWS = (2, 4, 8, 16)
POOL_GC = 256
POOL_CH = POOL_GROUPS * POOL_GC
POOL_BUF = 15
N_BRANCH = 3
D_FF = 11008
CONV_W = 3
REL_BUCKETS = 32
REL_MAX_DIST = 128
RMS_EPS = 1e-6

MLSTM_QK_W = MLSTM_HEADS * MLSTM_DQK
MLSTM_V_W = MLSTM_HEADS * MLSTM_DV
DIFF_QK_W = DIFF_HEADS * 2 * DIFF_DQK
DIFF_V_W = DIFF_HEADS * DIFF_DV

LANES = 128
SUBLANES = 8
VMEM_LIMIT = 56 * 1024 * 1024
NEG = -1e30
ROW_CHUNK = 512

BIAS_SPAN = 128


def _cparams(sem, flags=None):
    return pltpu.CompilerParams(dimension_semantics=sem, vmem_limit_bytes=VMEM_LIMIT, flags=flags)


def _rmsnorm_kernel(x_ref, g_ref, o_ref):
    x = x_ref[...]
    y = x * lax.rsqrt(jnp.mean(x * x, axis=-1, keepdims=True) + RMS_EPS)
    o_ref[...] = (y * g_ref[...]).astype(o_ref.dtype)


def _rmsnorm(x, g, tm):
    m, d = x.shape
    return pl.pallas_call(
        _rmsnorm_kernel,
        grid=(m // tm,),
        in_specs=[pl.BlockSpec((tm, d), lambda i: (i, 0)),
                  pl.BlockSpec((1, d), lambda i: (0, 0))],
        out_specs=pl.BlockSpec((tm, d), lambda i: (i, 0)),
        out_shape=jax.ShapeDtypeStruct((m, d), BF16),
        compiler_params=_cparams(("parallel",)),
        name="rmsnorm",
    )(x, g.reshape(1, d))


def _cast_kernel(x_ref, o_ref):
    o_ref[...] = x_ref[...].astype(o_ref.dtype)


def _cast_rows(x, dtype, tm):
    m, d = x.shape
    assert m % tm == 0
    return pl.pallas_call(
        _cast_kernel,
        grid=(m // tm,),
        in_specs=[pl.BlockSpec((tm, d), lambda i: (i, 0))],
        out_specs=pl.BlockSpec((tm, d), lambda i: (i, 0)),
        out_shape=jax.ShapeDtypeStruct((m, d), dtype),
        compiler_params=_cparams(("parallel",)),
        name="cast_rows",
    )(x)


def _sigmoid(x):
    return 0.5 * (jnp.tanh(0.5 * x) + 1.0)


def _group_rmsnorm(y, g, width):
    outs = []
    for c in range(0, y.shape[1], width):
        yc = y[:, c:c + width]
        yn = yc * lax.rsqrt(jnp.mean(yc * yc, axis=-1, keepdims=True) + RMS_EPS)
        outs.append(yn * g[:, c:c + width])
    return outs[0] if len(outs) == 1 else jnp.concatenate(outs, axis=1)


def _ep_bias(acc, b):
    return (acc + b,)


def _ep_bias_bf16copy(acc, b):
    y = acc + b
    return (y, y.astype(BF16))


def _ep_bias_sigmoid(acc, b):
    return (_sigmoid(acc + b),)


def _ep_knorm(acc, b, g):
    y = _group_rmsnorm(acc + b, g, DIFF_DQK)
    return (y, y.astype(BF16))


def _ep_qnorm_scaled(acc, b, g):
    y = _group_rmsnorm(acc + b, g, DIFF_DQK) * (DIFF_DQK ** -0.5)
    return (y, y.astype(BF16))


def _ep_residual(acc, r):
    return (r + acc,)


def _row_chunks(tm):
    step = ROW_CHUNK if tm % ROW_CHUNK == 0 else tm
    return [slice(r, r + step) for r in range(0, tm, step)]


def _mm_kernel(a_ref, w_ref, *rest, n_extra, n_out, epilogue, tile_extra):
    extras = rest[:n_extra]
    outs = rest[n_extra:n_extra + n_out]
    w = w_ref[...].astype(BF16)
    for rows in _row_chunks(a_ref.shape[0]):
        acc = jnp.dot(a_ref[rows, :].astype(BF16), w, preferred_element_type=F32)
        res = epilogue(acc, *[e[rows, :] if t else e[...] for e, t in zip(extras, tile_extra)])
        for o, r in zip(outs, res):
            o[rows, :] = r.astype(o.dtype)


def _matmul(a, w, extras, epilogue, out_dtypes, *, tm, tn, name, layer=None, col0=0, n=None):
    m, k = a.shape
    n = w.shape[-1] - col0 if n is None else n
    tm = min(tm, m)
    tn = min(tn, n)
    assert m % tm == 0 and n % tn == 0 and col0 % tn == 0, (m, n, tm, tn, col0)
    jb = col0 // tn
    if w.ndim == 2:
        w_spec = pl.BlockSpec((k, tn), lambda i, j: (layer or 0, jb + j))
    else:
        w_spec = pl.BlockSpec((None, k, tn), lambda i, j: (layer, 0, jb + j))
    in_specs = [pl.BlockSpec((tm, k), lambda i, j: (i, 0)), w_spec]
    tile_extra = []
    for e in extras:
        tile_extra.append(e.shape[0] != 1)
        if e.shape[0] == 1:
            in_specs.append(pl.BlockSpec((1, tn), lambda i, j: (0, j)))
        else:
            in_specs.append(pl.BlockSpec((tm, tn), lambda i, j: (i, j)))
    out_specs = [pl.BlockSpec((tm, tn), lambda i, j: (i, j)) for _ in out_dtypes]
    out_shape = [jax.ShapeDtypeStruct((m, n), dt) for dt in out_dtypes]
    return pl.pallas_call(
        functools.partial(_mm_kernel, n_extra=len(extras), n_out=len(out_dtypes), epilogue=epilogue,
                          tile_extra=tuple(tile_extra)),
        grid=(m // tm, n // tn),
        in_specs=in_specs,
        out_specs=out_specs,
        out_shape=out_shape,
        compiler_params=_cparams(("parallel", "arbitrary")),
        name=name,
    )(a, w, *extras)


def _merge_kernel(hm_ref, hd_ref, hp_ref, wm_ref, wd_ref, wp_ref, g0_ref, g1_ref, g2_ref, o_ref):
    wm, wd, wp = (w[...].astype(BF16) for w in (wm_ref, wd_ref, wp_ref))
    for rows in _row_chunks(o_ref.shape[0]):
        proj = lambda h_ref, w: jnp.dot(h_ref[rows, :].astype(BF16), w, preferred_element_type=F32)
        merged = (g0_ref[rows, :] * proj(hm_ref, wm)
                  + g1_ref[rows, :] * proj(hd_ref, wd)
                  + g2_ref[rows, :] * proj(hp_ref, wp))
        o_ref[rows, :] = merged.astype(o_ref.dtype)


def _merge(hm, hd, hp, wm, wd, wp, gates, *, layer, tm, tn):
    m = hm.shape[0]
    tm = min(tm, m)
    nj = D_MODEL // tn
    row = lambda width: pl.BlockSpec((tm, width), lambda i, j: (i, 0))
    col = lambda depth: pl.BlockSpec((depth, tn), lambda i, j: (layer, j))
    gate = lambda br: pl.BlockSpec((tm, tn), lambda i, j: (i, br * nj + j))
    return pl.pallas_call(
        _merge_kernel,
        grid=(m // tm, nj),
        in_specs=[row(MLSTM_V_W), row(DIFF_V_W), row(POOL_CH),
                  col(MLSTM_V_W), col(DIFF_V_W), col(POOL_CH),
                  gate(0), gate(1), gate(2)],
        out_specs=pl.BlockSpec((tm, tn), lambda i, j: (i, j)),
        out_shape=jax.ShapeDtypeStruct((m, D_MODEL), BF16),
        compiler_params=_cparams(("parallel", "arbitrary")),
        name="merge",
    )(hm, hd, hp, wm, wd, wp, gates, gates, gates)


def _conv_gate(ug, uv, pg, pv, cw_g, cw_v, cb_g, cb_v):
    cg = cb_g + (cw_g[0:1] * pg[0] + cw_g[1:2] * pg[1] + cw_g[2:3] * ug)
    cv = cb_v + (cw_v[0:1] * pv[0] + cw_v[1:2] * pv[1] + cw_v[2:3] * uv)
    half = 0.5 * cg
    return (half * (jnp.tanh(half) + 1.0)) * cv


def _ffn_up_prompt_kernel(a_ref, wg_ref, wv_ref, cwg_ref, cwv_ref, cbg_ref, cbv_ref,
                          act_ref, tail_ref, carry_ref, *, tiles_per_seq):
    i = pl.program_id(0)
    j = pl.program_id(1)
    tm = a_ref.shape[0]

    @pl.when(i % tiles_per_seq == 0)
    def _():
        carry_ref[:, pl.ds(j, 1)] = jnp.zeros((2, 1) + carry_ref.shape[2:], F32)

    wg = wg_ref[...].astype(BF16)
    wv = wv_ref[...].astype(BF16)
    tails = [carry_ref[0, j], carry_ref[1, j]]
    r8 = lax.broadcasted_iota(jnp.int32, (SUBLANES, wg.shape[1]), 0)

    def back(u, tail8):
        out = []
        for k in (2, 1):
            rolled = pltpu.roll(u, k, 0)
            head = rolled[:SUBLANES]
            for r in range(k):
                head = jnp.where(r8 == r, tail8[SUBLANES - k + r:SUBLANES - k + r + 1], head)
            out.append(jnp.concatenate([head, rolled[SUBLANES:]], axis=0))
        return tuple(out)

    for rows in _row_chunks(tm):
        a = a_ref[rows, :]
        ug = jnp.dot(a, wg, preferred_element_type=F32)
        uv = jnp.dot(a, wv, preferred_element_type=F32)
        act = _conv_gate(ug, uv, back(ug, tails[0]), back(uv, tails[1]),
                         cwg_ref[...], cwv_ref[...], cbg_ref[...], cbv_ref[...])
        act_ref[rows, :] = act.astype(act_ref.dtype)
        tails = [ug[ug.shape[0] - SUBLANES:], uv[uv.shape[0] - SUBLANES:]]
    for half in range(2):
        carry_ref[half, j] = tails[half]
        tail_ref[0, half] = tails[half]


def _ffn_up_prompt(h, w_up, conv_w, conv_b, *, layer, batch, tm, tn):
    m, d = h.shape
    nj = D_FF // tn
    tiles_per_seq = (m // batch) // tm
    conv_b = conv_b.reshape(1, 2 * D_FF)
    act, tail = pl.pallas_call(
        functools.partial(_ffn_up_prompt_kernel, tiles_per_seq=tiles_per_seq),
        grid=(m // tm, nj),
        in_specs=[pl.BlockSpec((tm, d), lambda i, j: (i, 0)),
                  pl.BlockSpec((None, d, tn), lambda i, j: (layer, 0, j)),
                  pl.BlockSpec((None, d, tn), lambda i, j: (layer, 0, nj + j)),
                  pl.BlockSpec((CONV_W, tn), lambda i, j: (0, j)),
                  pl.BlockSpec((CONV_W, tn), lambda i, j: (0, nj + j)),
                  pl.BlockSpec((1, tn), lambda i, j: (0, j)),
                  pl.BlockSpec((1, tn), lambda i, j: (0, nj + j))],
        out_specs=[pl.BlockSpec((tm, tn), lambda i, j: (i, j)),
                   pl.BlockSpec((1, 2, SUBLANES, tn), lambda i, j: (i, 0, 0, j))],
        out_shape=[jax.ShapeDtypeStruct((m, D_FF), BF16),
                   jax.ShapeDtypeStruct((m // tm, 2, SUBLANES, D_FF), F32)],
        scratch_shapes=[pltpu.VMEM((2, nj, SUBLANES, tn), F32)],
        compiler_params=_cparams(("arbitrary", "arbitrary")),
        name="ffn_up_prompt",
    )(h, w_up, w_up, conv_w, conv_w, conv_b, conv_b)
    tail = tail[tiles_per_seq - 1::tiles_per_seq]
    conv_new = jnp.transpose(tail[:, :, SUBLANES - (CONV_W - 1):, :], (0, 2, 1, 3))
    return act, conv_new.reshape(batch, CONV_W - 1, 2 * D_FF)


def _ffn_up_sample_kernel(a_ref, wg_ref, wv_ref, cwg_ref, cwv_ref, cbg_ref, cbv_ref,
                          p2g_ref, p1g_ref, p2v_ref, p1v_ref, act_ref, ug_ref, uv_ref):
    a = a_ref[...].astype(BF16)
    ug = jnp.dot(a, wg_ref[...].astype(BF16), preferred_element_type=F32)
    uv = jnp.dot(a, wv_ref[...].astype(BF16), preferred_element_type=F32)
    act = _conv_gate(ug, uv, (p2g_ref[...], p1g_ref[...]), (p2v_ref[...], p1v_ref[...]),
                     cwg_ref[...], cwv_ref[...], cbg_ref[...], cbv_ref[...])
    act_ref[...] = act.astype(act_ref.dtype)
    ug_ref[...] = ug
    uv_ref[...] = uv


def _ffn_up_sample(h, w_up, conv_w, conv_b, conv_buf, *, layer, tn):
    b, d = h.shape
    nj = D_FF // tn
    conv_b = conv_b.reshape(1, 2 * D_FF)
    prev2 = conv_buf[:, 0, :]
    prev1 = conv_buf[:, 1, :]
    lo = lambda rows: pl.BlockSpec((rows, tn), lambda j: (0, j))
    hi = lambda rows: pl.BlockSpec((rows, tn), lambda j: (0, nj + j))
    act, ug, uv = pl.pallas_call(
        _ffn_up_sample_kernel,
        grid=(nj,),
        in_specs=[pl.BlockSpec((b, d), lambda j: (0, 0)),
                  pl.BlockSpec((None, d, tn), lambda j: (layer, 0, j)),
                  pl.BlockSpec((None, d, tn), lambda j: (layer, 0, nj + j)),
                  lo(CONV_W), hi(CONV_W), lo(1), hi(1),
                  lo(b), lo(b), hi(b), hi(b)],
        out_specs=[lo(b), lo(b), lo(b)],
        out_shape=[jax.ShapeDtypeStruct((b, D_FF), BF16),
                   jax.ShapeDtypeStruct((b, D_FF), F32),
                   jax.ShapeDtypeStruct((b, D_FF), F32)],
        compiler_params=_cparams(("arbitrary",)),
        name="ffn_up_sample",
    )(h, w_up, w_up, conv_w, conv_w, conv_b, conv_b, prev2, prev1, prev2, prev1)
    u = jnp.concatenate([ug, uv], axis=1)
    return act, jnp.concatenate([conv_buf[:, 1:], u[:, None, :]], axis=1)


def _log_sigmoid(x):
    return -(jnp.maximum(-x, 0.0) + jnp.log1p(jnp.exp(-jnp.abs(x))))


def _cumsum_rows(x):
    n = x.shape[0]
    row = lax.broadcasted_iota(jnp.int32, x.shape, 0)
    d = 1
    while d < n:
        x = x + jnp.where(row >= d, pltpu.roll(x, d, 0), 0.0)
        d *= 2
    return x


def _head_out_norm(h, g, o_gate):
    y = h * lax.rsqrt(jnp.mean(h * h, axis=-1, keepdims=True) + RMS_EPS)
    return (y * g) * _sigmoid(o_gate)


def _mlstm_prompt_kernel(qkv_ref, mif_ref, mo_ref, nw_ref, hm_ref, c_out, n_out, m_out,
                         c_sc, n_sc, m_sc, *, cs, nc):
    ci = pl.program_id(0)

    @pl.when(ci == 0)
    def _():
        c_sc[...] = jnp.zeros(c_sc.shape, F32)
        n_sc[...] = jnp.zeros(n_sc.shape, F32)
        m_sc[...] = jnp.zeros(m_sc.shape, F32)

    row = lax.broadcasted_iota(jnp.int32, (cs, cs), 0)
    col = lax.broadcasted_iota(jnp.int32, (cs, cs), 1)
    causal = col <= row
    eye = col == row
    lane = lax.broadcasted_iota(jnp.int32, (1, LANES), 1)
    for bi in range(qkv_ref.shape[0]):
        m_vec = _mlstm_chunk(bi, qkv_ref, mif_ref, mo_ref, nw_ref, hm_ref, c_sc, n_sc, m_sc,
                             causal, eye, lane, cs)

        @pl.when(ci == nc - 1)
        def _():
            m_out[bi] = m_vec

    @pl.when(ci == nc - 1)
    def _():
        c_out[...] = c_sc[...]
        n_out[...] = n_sc[...][:, :, 0, :]


def _mlstm_chunk(bi, qkv_ref, mif_ref, mo_ref, nw_ref, hm_ref, c_sc, n_sc, m_sc, causal, eye, lane, cs):
    mif = mif_ref[bi]
    b_all = _cumsum_rows(_log_sigmoid(mif))
    m_vec = jnp.zeros((1, LANES), F32)
    for h in range(MLSTM_HEADS):
        q = qkv_ref[bi, :, h * MLSTM_DQK:(h + 1) * MLSTM_DQK]
        k = qkv_ref[bi, :, MLSTM_QK_W + h * MLSTM_DQK:MLSTM_QK_W + (h + 1) * MLSTM_DQK] * (MLSTM_DQK ** -0.5)
        v = qkv_ref[bi, :, 2 * MLSTM_QK_W + h * MLSTM_DV:2 * MLSTM_QK_W + (h + 1) * MLSTM_DV]
        li = mif[:, h:h + 1]
        b = b_all[:, MLSTM_HEADS + h:MLSTM_HEADS + h + 1]
        g_row = jnp.sum(jnp.where(eye, li - b, 0.0), axis=0, keepdims=True)
        dmat = jnp.where(causal, b + g_row, NEG)
        m_prev = m_sc[bi, h]
        inter = b + m_prev
        m_t = jnp.maximum(inter, jnp.max(dmat, axis=-1, keepdims=True))
        w_inter = jnp.exp(inter - m_t)
        qb, kb, vb = q.astype(BF16), k.astype(BF16), v.astype(BF16)
        s = lax.dot_general(qb, kb, (((1,), (1,)), ((), ())), preferred_element_type=F32)
        sc = s * jnp.exp(dmat - m_t)
        c_prev = c_sc[bi, h]
        n_prev = n_sc[bi, h]
        num = (w_inter * jnp.dot(qb, c_prev.astype(BF16), preferred_element_type=F32)
               + jnp.dot(sc.astype(BF16), vb, preferred_element_type=F32))
        den = w_inter * jnp.sum(q * n_prev, axis=-1, keepdims=True) + jnp.sum(sc, axis=-1, keepdims=True)
        hh = num / jnp.maximum(jnp.abs(den), jnp.exp(-m_t))
        m_new = m_t[cs - 1:cs]
        b_last = b[cs - 1:cs]
        g_state = jnp.exp(b_last + m_prev - m_new)
        g_tok = jnp.exp(b_last - b + li - m_new)
        kg = k * g_tok
        c_new = g_state * c_prev + lax.dot_general(kg.astype(BF16), vb, (((0,), (0,)), ((), ())),
                                                   preferred_element_type=F32)
        n_new = g_state * n_prev + jnp.sum(kg, axis=0, keepdims=True)
        c_sc[bi, h] = c_new
        n_sc[bi, h] = n_new
        m_sc[bi, h] = m_new
        m_vec = m_vec + jnp.where(lane == h, m_new, 0.0)
        sl = slice(h * MLSTM_DV, (h + 1) * MLSTM_DV)
        hm_ref[bi, :, sl] = _head_out_norm(hh, nw_ref[:, sl], mo_ref[bi, :, sl]).astype(hm_ref.dtype)
    return m_vec


def _mlstm_prompt(mqkv, mif, mo, norm_w, *, batch):
    m = mqkv.shape[0]
    seq = m // batch
    cs = MLSTM_CHUNK if seq % MLSTM_CHUNK == 0 else seq
    nc = seq // cs
    per_seq = lambda a: a.reshape(batch, seq, a.shape[-1])
    rows = lambda width: pl.BlockSpec((batch, cs, width), lambda c: (0, c, 0))
    whole = lambda shape: pl.BlockSpec(shape, lambda c: (0,) * len(shape))
    state_shapes = [(batch, MLSTM_HEADS, MLSTM_DQK, MLSTM_DV), (batch, MLSTM_HEADS, MLSTM_DQK),
                    (batch, 1, LANES)]
    hm, c, n, mm = pl.pallas_call(
        functools.partial(_mlstm_prompt_kernel, cs=cs, nc=nc),
        grid=(nc,),
        in_specs=[rows(mqkv.shape[1]), rows(LANES), rows(MLSTM_V_W), whole((1, MLSTM_V_W))],
        out_specs=[rows(MLSTM_V_W)] + [whole(s) for s in state_shapes],
        out_shape=[jax.ShapeDtypeStruct((batch, seq, MLSTM_V_W), BF16)]
        + [jax.ShapeDtypeStruct(s, F32) for s in state_shapes],
        scratch_shapes=[pltpu.VMEM((batch, MLSTM_HEADS, MLSTM_DQK, MLSTM_DV), F32),
                        pltpu.VMEM((batch, MLSTM_HEADS, 1, MLSTM_DQK), F32),
                        pltpu.VMEM((batch, MLSTM_HEADS, 1, 1), F32)],
        compiler_params=_cparams(("arbitrary",)),
        name="mlstm_prompt",
    )(per_seq(mqkv), per_seq(mif), per_seq(mo), norm_w.reshape(1, MLSTM_V_W))
    return hm.reshape(m, MLSTM_V_W), c, n, mm[:, 0, :MLSTM_HEADS]


def _lanes_to_rows(x_row, eye):
    return jnp.sum(jnp.where(eye, x_row, 0.0), axis=1, keepdims=True)


def _mlstm_sample_kernel(qkv_ref, mif_ref, mo_ref, nw_ref, c_ref, n_ref, m_ref,
                         hm_ref, c_out, n_out, m_out):
    mif = mif_ref[0]
    lf_all = _log_sigmoid(mif)
    m_all = m_ref[0]
    row = lax.broadcasted_iota(jnp.int32, (MLSTM_DQK, MLSTM_DQK), 0)
    col = lax.broadcasted_iota(jnp.int32, (MLSTM_DQK, MLSTM_DQK), 1)
    eye = row == col
    lane = lax.broadcasted_iota(jnp.int32, (1, LANES), 1)
    m_vec = jnp.zeros((1, LANES), F32)
    for h in range(MLSTM_HEADS):
        q = qkv_ref[0, :, h * MLSTM_DQK:(h + 1) * MLSTM_DQK]
        k = qkv_ref[0, :, MLSTM_QK_W + h * MLSTM_DQK:MLSTM_QK_W + (h + 1) * MLSTM_DQK] * (MLSTM_DQK ** -0.5)
        v = qkv_ref[0, :, 2 * MLSTM_QK_W + h * MLSTM_DV:2 * MLSTM_QK_W + (h + 1) * MLSTM_DV]
        li = mif[:, h:h + 1]
        lf = lf_all[:, MLSTM_HEADS + h:MLSTM_HEADS + h + 1]
        m_prev = m_all[:, h:h + 1]
        inter = lf + m_prev
        m_t = jnp.maximum(inter, li)
        w_inter = jnp.exp(inter - m_t)
        sc = jnp.sum(q * k, axis=-1, keepdims=True) * jnp.exp(li - m_t)
        c_prev = c_ref[0, h]
        n_prev = n_ref[0, h:h + 1, :]
        q_col = _lanes_to_rows(q, eye)
        k_col = _lanes_to_rows(k, eye)
        num = w_inter * jnp.sum(q_col * c_prev, axis=0, keepdims=True) + sc * v
        den = w_inter * jnp.sum(q * n_prev, axis=-1, keepdims=True) + sc
        hh = num / jnp.maximum(jnp.abs(den), jnp.exp(-m_t))
        g_tok = jnp.exp(li - m_t)
        c_out[0, h] = w_inter * c_prev + (g_tok * k_col) * v
        n_out[0, h:h + 1, :] = w_inter * n_prev + g_tok * k
        m_vec = m_vec + jnp.where(lane == h, m_t, 0.0)
        sl = slice(h * MLSTM_DV, (h + 1) * MLSTM_DV)
        hm_ref[0, :, sl] = _head_out_norm(hh, nw_ref[:, sl], mo_ref[0, :, sl]).astype(hm_ref.dtype)
    m_out[0] = m_vec


def _mlstm_sample(mqkv, mif, mo, norm_w, c0, n0, m0):
    b = mqkv.shape[0]
    m0p = jnp.pad(m0, ((0, 0), (0, LANES - MLSTM_HEADS))).reshape(b, 1, LANES)
    per_seq = lambda width: pl.BlockSpec((1, 1, width), lambda i: (i, 0, 0))
    c_spec = pl.BlockSpec((1, MLSTM_HEADS, MLSTM_DQK, MLSTM_DV), lambda i: (i, 0, 0, 0))
    n_spec = pl.BlockSpec((1, MLSTM_HEADS, MLSTM_DQK), lambda i: (i, 0, 0))
    hm, c, n, mm = pl.pallas_call(
        _mlstm_sample_kernel,
        grid=(b,),
        in_specs=[per_seq(mqkv.shape[1]), per_seq(LANES), per_seq(MLSTM_V_W),
                  pl.BlockSpec((1, MLSTM_V_W), lambda i: (0, 0)),
                  c_spec, n_spec, per_seq(LANES)],
        out_specs=[per_seq(MLSTM_V_W), c_spec, n_spec, per_seq(LANES)],
        out_shape=[jax.ShapeDtypeStruct((b, 1, MLSTM_V_W), F32),
                   jax.ShapeDtypeStruct(c0.shape, F32),
                   jax.ShapeDtypeStruct(n0.shape, F32),
                   jax.ShapeDtypeStruct((b, 1, LANES), F32)],
        compiler_params=_cparams(("parallel",)),
        name="mlstm_sample",
    )(mqkv.reshape(b, 1, -1), mif.reshape(b, 1, LANES), mo.reshape(b, 1, -1),
      norm_w.reshape(1, MLSTM_V_W), c0, n0, m0p)
    return hm.reshape(b, MLSTM_V_W), c, n, mm[:, 0, :MLSTM_HEADS]


def _bucket_by_distance():
    max_exact = REL_BUCKETS // 2
    n = np.arange(BIAS_SPAN + 1)
    large = max_exact + np.floor(np.log(np.maximum(n, 1) / max_exact) / math.log(REL_MAX_DIST / max_exact)
                                 * (REL_BUCKETS - max_exact)).astype(np.int64)
    return np.where(n < max_exact, n, np.minimum(large, REL_BUCKETS - 1))


def _bucket_starts():
    bucket = _bucket_by_distance()
    return [int(np.argmax(bucket >= b)) for b in range(1, REL_BUCKETS)]


def _bias_by_distance(rel_bias):
    tab = jnp.transpose(rel_bias.astype(F32)[_bucket_by_distance()], (1, 0))
    return tab - tab[:, BIAS_SPAN:]


def _lambda(lq_ref, lk_ref, lam_init):
    lq = lq_ref[...]
    lk = lk_ref[...]
    e0 = jnp.exp(jnp.sum(lq[0:1] * lk[0:1], axis=-1, keepdims=True))
    e1 = jnp.exp(jnp.sum(lq[1:2] * lk[1:2], axis=-1, keepdims=True))
    return e0 - e1 + lam_init


def _diff_out_norm(d, sub, lam_init):
    y = d * lax.rsqrt(jnp.mean(d * d, axis=-1, keepdims=True) + RMS_EPS)
    return (y * sub) * (1.0 - lam_init)


def _bias_tiles_kernel(rb_ref, o_ref, *, tq):
    h = pl.program_id(0)
    blk = pl.program_id(1)
    row = lax.broadcasted_iota(jnp.int32, (tq, tq), 0)
    col = lax.broadcasted_iota(jnp.int32, (tq, tq), 1)
    dist = row - col + blk * tq
    far = rb_ref[h, REL_BUCKETS - 1]
    val = jnp.full((tq, tq), rb_ref[h, 0] - far, F32)
    for b, start in enumerate(_bucket_starts(), start=1):
        val = jnp.where(dist >= start, rb_ref[h, b] - far, val)
    o_ref[0, 0] = jnp.where(dist >= 0, val, NEG)


def _bias_tiles(rel_bias, tq):
    assert tq + 1 >= _bucket_starts()[-1]
    return pl.pallas_call(
        functools.partial(_bias_tiles_kernel, tq=tq),
        grid=(DIFF_HEADS, 2),
        in_specs=[pl.BlockSpec(memory_space=pltpu.SMEM)],
        out_specs=pl.BlockSpec((1, 1, tq, tq), lambda h, blk: (h, blk, 0, 0)),
        out_shape=jax.ShapeDtypeStruct((DIFF_HEADS, 2, tq, tq), F32),
        compiler_params=_cparams(("parallel", "parallel")),
        name="bias_tiles",
    )(jnp.transpose(rel_bias.astype(F32)))


def _attn_prompt_kernel(qi_ref, ki_ref, q_ref, k_ref, v_ref, bias_ref, lq_ref, lk_ref, sub_ref, o_ref,
                        m_sc, l_sc, acc_sc, *, lam_init):
    qi = qi_ref[pl.program_id(2)]
    ki = ki_ref[pl.program_id(2)]

    @pl.when(ki == 0)
    def _():
        m_sc[...] = jnp.full(m_sc.shape, NEG, F32)
        l_sc[...] = jnp.zeros(l_sc.shape, F32)
        acc_sc[...] = jnp.zeros(acc_sc.shape, F32)

    def tile(bias):
        v = v_ref[0]
        for mp in range(2):
            sl = slice(mp * DIFF_DQK, (mp + 1) * DIFF_DQK)
            s = lax.dot_general(q_ref[0, :, sl], k_ref[0, :, sl], (((1,), (1,)), ((), ())),
                                preferred_element_type=F32)
            if bias is not None:
                s = s + bias
            m_prev = m_sc[mp]
            m_new = jnp.maximum(m_prev, jnp.max(s, axis=-1, keepdims=True))
            alpha = jnp.exp(m_prev - m_new)
            p = jnp.exp(s - m_new)
            l_sc[mp] = alpha * l_sc[mp] + jnp.sum(p, axis=-1, keepdims=True)
            acc_sc[mp] = alpha * acc_sc[mp] + jnp.dot(p.astype(BF16), v, preferred_element_type=F32)
            m_sc[mp] = m_new

    @pl.when(ki < qi - 1)
    def _():
        tile(None)

    @pl.when((ki >= qi - 1) & (ki <= qi))
    def _():
        tile(bias_ref[0, 0])

    @pl.when(ki == qi)
    def _():
        lam = _lambda(lq_ref, lk_ref, lam_init)
        d = acc_sc[0] / l_sc[0] - lam * (acc_sc[1] / l_sc[1])
        o_ref[0] = _diff_out_norm(d, sub_ref[...], lam_init).astype(o_ref.dtype)


def _attn_prompt(q, k, v, bias, lam_q, lam_k, subln, *, batch, lam_init):
    m = q.shape[0]
    seq = m // batch
    tq = bias.shape[-1]
    assert seq % tq == 0
    nq = seq // tq
    hw = 2 * DIFF_DQK
    q3, k3, v3 = (a.reshape(batch, seq, -1) for a in (q, k, v))
    pairs = [(qi, ki) for qi in range(nq) for ki in range(qi + 1)]
    qi_tab = jnp.asarray([p[0] for p in pairs], jnp.int32)
    ki_tab = jnp.asarray([p[1] for p in pairs], jnp.int32)
    q_spec = pl.BlockSpec((1, tq, hw), lambda b, h, t, qt, kt: (b, qt[t], h))
    kv_spec = pl.BlockSpec((1, tq, hw), lambda b, h, t, qt, kt: (b, kt[t], h))
    const = lambda shape: pl.BlockSpec(shape, lambda b, h, t, qt, kt: (0,) * len(shape))
    out = pl.pallas_call(
        functools.partial(_attn_prompt_kernel, lam_init=lam_init),
        grid_spec=pltpu.PrefetchScalarGridSpec(
            num_scalar_prefetch=2,
            grid=(batch, DIFF_HEADS, len(pairs)),
            in_specs=[q_spec, kv_spec, kv_spec,
                      pl.BlockSpec((1, 1, tq, tq),
                                   lambda b, h, t, qt, kt: (h, jnp.minimum(qt[t] - kt[t], 1), 0, 0)),
                      const((2, DIFF_DQK)), const((2, DIFF_DQK)), const((1, DIFF_DV))],
            out_specs=pl.BlockSpec((1, tq, DIFF_DV), lambda b, h, t, qt, kt: (b, qt[t], h)),
            scratch_shapes=[pltpu.VMEM((2, tq, 1), F32), pltpu.VMEM((2, tq, 1), F32),
                            pltpu.VMEM((2, tq, DIFF_DV), F32)]),
        out_shape=jax.ShapeDtypeStruct((batch, seq, DIFF_V_W), BF16),
        compiler_params=_cparams(("parallel", "parallel", "arbitrary")),
        name="attn_prompt",
    )(qi_tab, ki_tab, q3, k3, v3, bias, lam_q, lam_k, subln.reshape(1, DIFF_DV))
    return out.reshape(m, DIFF_V_W)


N_MAPS = 2 * DIFF_HEADS
PAGE_ROWS = PAGE_SIZE * DIFF_HEADS
NEW_TOKENS = 16
DECODE_PAGES_PER_STEP = 8


def _attn_sample_kernel(pt_ref, qm_ref, kn_ref, vn_ref, *rest, n_steps, pages, lam_init):
    kc_refs, vc_refs = rest[:pages], rest[pages:2 * pages]
    (bfar_ref, blast_ref, bnew_ref, lq_ref, lk_ref, sub_ref, o_ref,
     qx_sc, m_sc, l_sc, acc_sc) = rest[2 * pages:]
    p = pl.program_id(1)
    sub_i = lax.broadcasted_iota(jnp.int32, (DIFF_HEADS, LANES), 0)
    lane_i = lax.broadcasted_iota(jnp.int32, (DIFF_HEADS, LANES), 1)
    own = (lane_i == 2 * sub_i) | (lane_i == 2 * sub_i + 1)

    @pl.when(p == 0)
    def _():
        qm = qm_ref[0]
        r = lax.broadcasted_iota(jnp.int32, qm.shape, 0)
        first = (r & 1) == 0
        qx = jnp.concatenate([jnp.where(first, qm, 0.0), jnp.where(first, 0.0, qm)], axis=1)
        qx_sc[...] = jnp.zeros(qx_sc.shape, BF16)
        qx_sc[0:N_MAPS, :] = qx.astype(BF16)
        m_sc[...] = jnp.where(own, NEG, 0.0)
        l_sc[...] = jnp.zeros(l_sc.shape, F32)
        acc_sc[...] = jnp.zeros(acc_sc.shape, F32)

    row = lax.broadcasted_iota(jnp.int32, (LANES, LANES), 0)
    col = lax.broadcasted_iota(jnp.int32, (LANES, LANES), 1)
    eye = row == col

    def per_lane_column(x):
        x_row = jnp.sum(jnp.where(own, x, 0.0), axis=0, keepdims=True)
        return jnp.sum(jnp.where(eye, x_row, 0.0), axis=1, keepdims=True)

    def group_update(blocks):
        scores = []
        m_prev = m_sc[...]
        m_new = m_prev
        for kf, _, bias3 in blocks:
            s_all = lax.dot_general(kf.astype(BF16), qx_sc[...], (((1,), (1,)), ((), ())),
                                    preferred_element_type=F32)
            s3 = s_all.reshape(kf.shape[0] // DIFF_HEADS, DIFF_HEADS, LANES) + bias3
            m_new = jnp.maximum(m_new, jnp.max(s3, axis=0))
            scores.append(s3)
        alpha = jnp.exp(m_prev - m_new)
        l_new = alpha * l_sc[...]
        pv = None
        for s3, (_, vf, _) in zip(scores, blocks):
            p3 = jnp.exp(s3 - m_new[None])
            l_new = l_new + jnp.sum(p3, axis=0)
            p_t = p3.reshape(vf.shape[0], LANES).T.astype(BF16)
            part = jnp.dot(p_t, vf.astype(BF16), preferred_element_type=F32)
            pv = part if pv is None else pv + part
        l_sc[...] = l_new
        m_sc[...] = m_new
        acc_sc[...] = per_lane_column(alpha) * acc_sc[...] + pv

    far = bfar_ref[...][None]

    @pl.when(p < n_steps - 1)
    def _():
        group_update([(kc[...], vc[...], far) for kc, vc in zip(kc_refs, vc_refs)])

    @pl.when(p == n_steps - 1)
    def _():
        biases = [far] * (pages - 1) + [blast_ref[...].reshape(PAGE_SIZE, DIFF_HEADS, LANES)]
        blocks = [(kc[...], vc[...], b3) for kc, vc, b3 in zip(kc_refs, vc_refs, biases)]
        blocks.append((kn_ref[0], vn_ref[0], bnew_ref[...].reshape(NEW_TOKENS, DIFF_HEADS, LANES)))
        group_update(blocks)
        lam = _lambda(lq_ref, lk_ref, lam_init)
        l_col = per_lane_column(l_sc[...])
        out = acc_sc[0:N_MAPS, :] / l_col[0:N_MAPS]
        for h in range(DIFF_HEADS):
            d = out[2 * h:2 * h + 1] - lam * out[2 * h + 1:2 * h + 2]
            sl = slice(h * DIFF_DV, (h + 1) * DIFF_DV)
            o_ref[0, :, sl] = _diff_out_norm(d, sub_ref[...], lam_init).astype(o_ref.dtype)


def _attn_sample(q, k_new, v_new, cache_k, cache_v, layer, page_table, bias_tab, lam_q, lam_k, subln,
                 *, lam_init):
    b, n_pages = page_table.shape
    assert PAGE_SIZE >= BIAS_SPAN
    pages = DECODE_PAGES_PER_STEP if n_pages % DECODE_PAGES_PER_STEP == 0 else 1
    n_steps = n_pages // pages
    hw = 2 * DIFF_DQK
    sub_i = lax.broadcasted_iota(jnp.int32, (DIFF_HEADS, LANES), 0)
    lane_i = lax.broadcasted_iota(jnp.int32, (DIFF_HEADS, LANES), 1)
    own = (lane_i == 2 * sub_i) | (lane_i == 2 * sub_i + 1)

    def rows_bias(per_token):
        t = per_token.shape[0]
        return jnp.where(own[None], per_token[:, :, None], NEG).reshape(t * DIFF_HEADS, LANES)

    b_far = jnp.where(own, 0.0, NEG)
    b_last = rows_bias(jnp.transpose(bias_tab[:, :0:-1][:, :PAGE_SIZE]))
    b_new = rows_bias(jnp.full((NEW_TOKENS, DIFF_HEADS), NEG, F32).at[0].set(bias_tab[:, 0]))
    pad_page = lambda a: jnp.pad(a.reshape(b, DIFF_HEADS, hw), ((0, 0), (0, (NEW_TOKENS - 1) * DIFF_HEADS), (0, 0)))
    new_spec = pl.BlockSpec((1, NEW_TOKENS * DIFF_HEADS, hw), lambda i, p, pt: (i, 0, 0))
    page = lambda r: pl.BlockSpec((None, None, PAGE_ROWS, hw),
                                  lambda i, p, pt: (layer, pt[i * n_pages + p * pages + r], 0, 0))
    const = lambda shape: pl.BlockSpec(shape, lambda i, p, pt: (0,) * len(shape))
    out = pl.pallas_call(
        functools.partial(_attn_sample_kernel, n_steps=n_steps, pages=pages, lam_init=lam_init),
        grid_spec=pltpu.PrefetchScalarGridSpec(
            num_scalar_prefetch=1,
            grid=(b, n_steps),
            in_specs=[pl.BlockSpec((1, N_MAPS, DIFF_DQK), lambda i, p, pt: (i, 0, 0)),
                      new_spec, new_spec, *[page(r) for r in range(pages)], *[page(r) for r in range(pages)],
                      const((DIFF_HEADS, LANES)), const((PAGE_ROWS, LANES)),
                      const((NEW_TOKENS * DIFF_HEADS, LANES)),
                      const((2, DIFF_DQK)), const((2, DIFF_DQK)), const((1, DIFF_DV))],
            out_specs=pl.BlockSpec((1, 1, DIFF_V_W), lambda i, p, pt: (i, 0, 0)),
            scratch_shapes=[pltpu.VMEM((LANES, hw), BF16),
                            pltpu.VMEM((DIFF_HEADS, LANES), F32), pltpu.VMEM((DIFF_HEADS, LANES), F32),
                            pltpu.VMEM((LANES, DIFF_DV), F32)]),
        out_shape=jax.ShapeDtypeStruct((b, 1, DIFF_V_W), F32),
        compiler_params=_cparams(("parallel", "arbitrary")),
        name="attn_sample",
    )(page_table.reshape(-1), q.reshape(b, N_MAPS, DIFF_DQK), pad_page(k_new), pad_page(v_new),
      *[cache_k] * pages, *[cache_v] * pages, b_far, b_last, b_new, lam_q, lam_k, subln.reshape(1, DIFF_DV))
    return out.reshape(b, DIFF_V_W)


HIST = 16


def _window_sum(ext, w, n):
    s = ext
    d = 1
    while d < w:
        s = s + pltpu.roll(s, d, 0)
        d *= 2
    return s[HIST:HIST + n]


def _pool_prompt_kernel(u_ref, w_ref, sc_ref, o_ref, hist_out, hist_sc, *, tm):
    t = pl.program_id(1)

    @pl.when(t == 0)
    def _():
        hist_sc[...] = jnp.zeros(hist_sc.shape, F32)

    pos = t * tm + lax.broadcasted_iota(jnp.int32, (tm, 1), 0)
    for gi, w in enumerate(POOL_WINDOWS):
        sl = slice(gi * POOL_GC, (gi + 1) * POOL_GC)
        u = u_ref[:, sl]
        ext = jnp.concatenate([hist_sc[:, sl], u], axis=0)
        cnt = jnp.minimum(w, pos + 1).astype(F32)
        pooled = _window_sum(ext, w, tm) / cnt - u
        mixed = jnp.dot(pooled.astype(BF16), w_ref[gi], preferred_element_type=F32)
        o_ref[:, sl] = (mixed * sc_ref[:, sl]).astype(o_ref.dtype)
        new_hist = u[tm - HIST:]
        hist_sc[:, sl] = new_hist
        hist_out[0, :, sl] = new_hist


def _pool_prompt(u, pool_w, pool_scale, *, batch, tm):
    m = u.shape[0]
    seq = m // batch
    tm = min(tm, seq)
    assert tm >= HIST
    tps = seq // tm
    hp, hist = pl.pallas_call(
        functools.partial(_pool_prompt_kernel, tm=tm),
        grid=(batch, tps),
        in_specs=[pl.BlockSpec((tm, POOL_CH), lambda b, t: (b * tps + t, 0)),
                  pl.BlockSpec((POOL_GROUPS, POOL_GC, POOL_GC), lambda b, t: (0, 0, 0)),
                  pl.BlockSpec((1, POOL_CH), lambda b, t: (0, 0))],
        out_specs=[pl.BlockSpec((tm, POOL_CH), lambda b, t: (b * tps + t, 0)),
                   pl.BlockSpec((1, HIST, POOL_CH), lambda b, t: (b, 0, 0))],
        out_shape=[jax.ShapeDtypeStruct((m, POOL_CH), BF16),
                   jax.ShapeDtypeStruct((batch, HIST, POOL_CH), F32)],
        scratch_shapes=[pltpu.VMEM((HIST, POOL_CH), F32)],
        compiler_params=_cparams(("parallel", "arbitrary")),
        name="pool_prompt",
    )(u, pool_w, pool_scale.reshape(1, POOL_CH))
    return hp, hist[:, HIST - POOL_BUF:]


def _pool_sample_kernel(ucat_ref, w_ref, sc_ref, o_ref):
    for gi, w in enumerate(POOL_WINDOWS):
        sl = slice(gi * POOL_GC, (gi + 1) * POOL_GC)
        new = ucat_ref[HIST - 1, :, sl]
        win = new
        for r in range(HIST - w, HIST - 1):
            win = win + ucat_ref[r, :, sl]
        pooled = win / float(w) - new
        mixed = jnp.dot(pooled.astype(BF16), w_ref[gi], preferred_element_type=F32)
        o_ref[:, sl] = (mixed * sc_ref[:, sl]).astype(o_ref.dtype)


def _pool_sample(u, buf, pool_w, pool_scale):
    b = u.shape[0]
    assert buf.shape[1] == HIST - 1
    ucat = jnp.concatenate([buf, u[:, None, :]], axis=1)
    hp = pl.pallas_call(
        _pool_sample_kernel,
        out_shape=jax.ShapeDtypeStruct((b, POOL_CH), F32),
        compiler_params=pltpu.CompilerParams(vmem_limit_bytes=VMEM_LIMIT),
        name="pool_sample",
    )(jnp.transpose(ucat, (1, 0, 2)), pool_w, pool_scale.reshape(1, POOL_CH))
    return hp, ucat[:, 1:]


IN_SPLITS = (MLSTM_QK_W, MLSTM_QK_W, MLSTM_V_W, 2 * MLSTM_HEADS, MLSTM_V_W,
             DIFF_QK_W, DIFF_QK_W, DIFF_V_W, POOL_CH, N_BRANCH * D_MODEL)
IN_OFFS = tuple(int(o) for o in np.cumsum((0,) + IN_SPLITS))
GATE_COLS = (IN_OFFS[3], IN_OFFS[4])


def _prep_weights(p):
    w_in = p["w_in"]
    g0, g1 = GATE_COLS
    pad = ((0, 0), (0, 0), (0, LANES - (g1 - g0)))
    rows = lambda w: w.reshape(-1, w.shape[-1])
    return dict(w_head=rows(w_in[:, :, :g0]), w_tail=rows(w_in[:, :, g1:].astype(BF16)),
                w_mif=rows(jnp.pad(w_in[:, :, g0:g1], pad)),
                w_down=_cast_rows(rows(p["w_down"]), BF16, tm=D_FF // 16),
                **{k: _cast_rows(rows(p[k]), BF16, tm=1024) for k in ("w_br_mlstm", "w_br_diff", "w_br_pool")})


def _prep_layer(p, l):
    b_in = p["b_in"][l]
    g0, g1 = GATE_COLS
    pad = ((0, 0), (0, LANES - (g1 - g0)))
    return dict(
        b_tail=b_in[g1:].reshape(1, -1), b_mif=jnp.pad(b_in[g0:g1].reshape(1, -1), pad),
        b_head=b_in[:g0].reshape(1, -1),
        q_norm=jnp.tile(p["q_norm"][l].reshape(1, -1), (1, DIFF_HEADS)),
        k_norm=jnp.tile(p["k_norm"][l].reshape(1, -1), (1, DIFF_HEADS)),
        norm_mix=p["norm_mix"][l], norm_ffn=p["norm_ffn"][l], mlstm_norm=p["mlstm_norm"][l],
        lambda_q=p["lambda_q"][l], lambda_k=p["lambda_k"][l], diff_subln=p["diff_subln"][l],
        pool_w=p["pool_w"][l].astype(BF16), pool_scale=p["pool_scale"][l],
        conv_w=p["conv_w"][l], conv_b=p["conv_b"][l],
    )


def _projections(h, wts, lw, l, *, tm, tn):
    g1 = GATE_COLS[1]

    def tail(seg, epilogue, extras, out_dtypes, name):
        c0, c1 = IN_OFFS[seg] - g1, IN_OFFS[seg + 1] - g1
        return _matmul(h, wts["w_tail"], (lw["b_tail"][:, c0:c1],) + extras, epilogue, out_dtypes,
                       layer=l, tm=tm, tn=tn, col0=c0, n=c1 - c0, name=name)

    out = {}
    (out["mqkv"],) = _matmul(h, wts["w_head"], (lw["b_head"],), _ep_bias, (F32,), layer=l, tm=tm, tn=tn,
                             name="proj_mqkv")
    (out["mif"],) = _matmul(h, wts["w_mif"], (lw["b_mif"],), _ep_bias, (F32,), layer=l, tm=tm, tn=tn,
                            name="proj_mif")
    (out["mo"],) = tail(4, _ep_bias, (), (F32,), "proj_mo")
    out["dq"], out["dq16"] = tail(5, _ep_qnorm_scaled, (lw["q_norm"],), (F32, BF16), "proj_dq")
    out["dk"], out["dk16"] = tail(6, _ep_knorm, (lw["k_norm"],), (F32, BF16), "proj_dk")
    out["dv"], out["dv16"] = tail(7, _ep_bias_bf16copy, (), (F32, BF16), "proj_dv")
    (out["pu"],) = tail(8, _ep_bias, (), (F32,), "proj_pu")
    (out["gates"],) = tail(9, _ep_bias_sigmoid, (), (F32,), "proj_gates")
    return out


ATTN_TQ = 512
TILES_PROMPT = dict(norm=512, proj=(1024, 512), merge=(1024, 256), out=(1024, 512),
                    ffn_up=(1024, 256), ffn_down=(512, 256), pool=512)
TILES_SAMPLE = dict(proj=512, merge=256, out=512, ffn_up=256, ffn_down=256)


def _layer_prompt(x, p, wts, lw, bias_tiles, l, *, batch):
    m = x.shape[0]
    t = TILES_PROMPT
    lam_init = 0.8 - 0.6 * math.exp(-0.3 * l)
    h = _rmsnorm(x, lw["norm_mix"], tm=t["norm"])
    pr = _projections(h, wts, lw, l, tm=t["proj"][0], tn=t["proj"][1])
    hm, c1, n1, m1 = _mlstm_prompt(pr["mqkv"], pr["mif"], pr["mo"], lw["mlstm_norm"], batch=batch)
    hd = _attn_prompt(pr["dq16"], pr["dk16"], pr["dv16"], bias_tiles, lw["lambda_q"], lw["lambda_k"],
                      lw["diff_subln"], batch=batch, lam_init=lam_init)
    hp, pool_new = _pool_prompt(pr["pu"], lw["pool_w"], lw["pool_scale"], batch=batch, tm=t["pool"])
    merged = _merge(hm, hd, hp, wts["w_br_mlstm"], wts["w_br_diff"], wts["w_br_pool"], pr["gates"],
                    layer=l, tm=t["merge"][0], tn=t["merge"][1])
    (x1,) = _matmul(merged, p["w_out"], (x,), _ep_residual, (F32,), layer=l, tm=t["out"][0], tn=t["out"][1],
                    name="out_proj")
    h2 = _rmsnorm(x1, lw["norm_ffn"], tm=t["norm"])
    act, conv_new = _ffn_up_prompt(h2, p["w_up"], lw["conv_w"], lw["conv_b"], layer=l, batch=batch,
                                   tm=t["ffn_up"][0], tn=t["ffn_up"][1])
    (x2,) = _matmul(act, wts["w_down"], (x1,), _ep_residual, (F32,), layer=l, tm=t["ffn_down"][0],
                    tn=t["ffn_down"][1], name="ffn_down")
    seq = m // batch
    kv_shape = (batch, seq, DIFF_HEADS, DIFF_DV)
    return x2, pr["dk"].reshape(kv_shape), pr["dv"].reshape(kv_shape), c1, n1, m1, pool_new, conv_new


def _layer_sample(x, p, wts, lw, bias_tab, l, cache_k, cache_v, page_table, c0, n0, m0, pool_buf, conv_buf):
    b = x.shape[0]
    t = TILES_SAMPLE
    lam_init = 0.8 - 0.6 * math.exp(-0.3 * l)
    h = _rmsnorm(x, lw["norm_mix"], tm=b)
    pr = _projections(h, wts, lw, l, tm=b, tn=t["proj"])
    hm, c1, n1, m1 = _mlstm_sample(pr["mqkv"], pr["mif"], pr["mo"], lw["mlstm_norm"], c0, n0, m0)
    hd = _attn_sample(pr["dq"], pr["dk"], pr["dv"], cache_k, cache_v, l, page_table, bias_tab,
                      lw["lambda_q"], lw["lambda_k"], lw["diff_subln"], lam_init=lam_init)
    hp, pool_new = _pool_sample(pr["pu"], pool_buf, lw["pool_w"], lw["pool_scale"])
    merged = _merge(hm, hd, hp, wts["w_br_mlstm"], wts["w_br_diff"], wts["w_br_pool"], pr["gates"],
                    layer=l, tm=b, tn=t["merge"])
    (x1,) = _matmul(merged, p["w_out"], (x,), _ep_residual, (F32,), layer=l, tm=b, tn=t["out"],
                    name="out_proj_s")
    h2 = _rmsnorm(x1, lw["norm_ffn"], tm=b)
    act, conv_new = _ffn_up_sample(h2, p["w_up"], lw["conv_w"], lw["conv_b"], conv_buf, layer=l, tn=t["ffn_up"])
    (x2,) = _matmul(act, wts["w_down"], (x1,), _ep_residual, (F32,), layer=l, tm=b, tn=t["ffn_down"],
                    name="ffn_down_s")
    kv_shape = (b, 1, DIFF_HEADS, DIFF_DV)
    return x2, pr["dk"].reshape(kv_shape), pr["dv"].reshape(kv_shape), c1, n1, m1, pool_new, conv_new


def kernel(x_prompt, x_sample, cache_k, cache_v, page_table, state_mlstm_c, state_mlstm_n, state_mlstm_m,
           state_pool, state_conv, rel_bias, norm_mix, w_in, b_in, mlstm_norm, q_norm, k_norm, lambda_q,
           lambda_k, diff_subln, pool_w, pool_scale, w_br_mlstm, w_br_diff, w_br_pool, w_out, norm_ffn,
           w_up, conv_w, conv_b, w_down):
    params = dict(norm_mix=norm_mix, w_in=w_in, b_in=b_in, mlstm_norm=mlstm_norm, q_norm=q_norm,
                  k_norm=k_norm, lambda_q=lambda_q, lambda_k=lambda_k, diff_subln=diff_subln,
                  pool_w=pool_w, pool_scale=pool_scale, w_br_mlstm=w_br_mlstm, w_br_diff=w_br_diff,
                  w_br_pool=w_br_pool, w_out=w_out, norm_ffn=norm_ffn, w_up=w_up, conv_w=conv_w,
                  conv_b=conv_b, w_down=w_down)
    depth = w_in.shape[0]
    n_prompt, seq, d = x_prompt.shape
    n_dec = x_sample.shape[0]
    assert x_sample.shape[1] == 1, "sample group decodes one token per sequence"
    bias_tab = _bias_by_distance(rel_bias)
    bias_tiles = _bias_tiles(rel_bias, min(ATTN_TQ, seq))
    ck = cache_k.reshape(cache_k.shape[:2] + (PAGE_ROWS, 2 * DIFF_DQK))
    cv = cache_v.reshape(cache_v.shape[:2] + (PAGE_ROWS, DIFF_DV))
    xp = x_prompt.reshape(n_prompt * seq, d)
    xs = x_sample.reshape(n_dec, d)
    outs_p, outs_s = [], []
    wts = _prep_weights(params)
    for l in range(depth):
        lw = _prep_layer(params, l)
        xp, *rest_p = _layer_prompt(xp, params, wts, lw, bias_tiles, l, batch=n_prompt)
        xs, *rest_s = _layer_sample(xs, params, wts, lw, bias_tab, l, ck, cv, page_table, state_mlstm_c[l],
                                    state_mlstm_n[l], state_mlstm_m[l], state_pool[l], state_conv[l])
        outs_p.append(rest_p)
        outs_s.append(rest_s)
    stack = lambda outs, idx: jnp.stack([o[idx] for o in outs])
    kp, vp, cp, np_, mp, pp, cvp = (stack(outs_p, i) for i in range(7))
    ks, vs, cs_, ns, ms, ps, cvs = (stack(outs_s, i) for i in range(7))
    return (xp.reshape(n_prompt, seq, d), xs.reshape(n_dec, 1, d),
            kp, vp, ks, vs, cp, np_, mp, cs_, ns, ms, pp, ps, cvp, cvs)
```

```python
import functools
import math

import jax
import jax.numpy as jnp
import numpy as np
from jax import lax
from jax.experimental import pallas as pl
from jax.experimental.pallas import tpu as pltpu

F32 = jnp.float32
BF16 = jnp.bfloat16

D_MODEL = 4096
PAGE_SIZE = 128
MLSTM_HEADS = 4
MLSTM_DV = 256
MLSTM_DQK = 128
MLSTM_CHUNK = 64
DIFF_HEADS = 8
DIFF_DV = 256
DIFF_DQK = 128
POOL_GROUPS = 4
POOL_WINDOWS = (2, 4, 8, 16)
POOL_GC = 256
POOL_CH = POOL_GROUPS * POOL_GC
POOL_BUF = 15
N_BRANCH = 3
D_FF = 11008
CONV_W = 3
REL_BUCKETS = 32
REL_MAX_DIST = 128
RMS_EPS = 1e-6

MLSTM_QK_W = MLSTM_HEADS * MLSTM_DQK
MLSTM_V_W = MLSTM_HEADS * MLSTM_DV
DIFF_QK_W = DIFF_HEADS * 2 * DIFF_DQK
DIFF_V_W = DIFF_HEADS * DIFF_DV

LANES = 128
SUBLANES = 8
VMEM_LIMIT = 56 * 1024 * 1024
NEG = -1e30
ROW_CHUNK = 512

BIAS_SPAN = 128


def _cparams(sem, flags=None):
    return pltpu.CompilerParams(dimension_semantics=sem, vmem_limit_bytes=VMEM_LIMIT, flags=flags)


def _rmsnorm_kernel(x_ref, g_ref, o_ref):
    x = x_ref[...]
    y = x * lax.rsqrt(jnp.mean(x * x, axis=-1, keepdims=True) + RMS_EPS)
    o_ref[...] = (y * g_ref[...]).astype(o_ref.dtype)


def _rmsnorm(x, g, tm):
    m, d = x.shape
    return pl.pallas_call(
        _rmsnorm_kernel,
        grid=(m // tm,),
        in_specs=[pl.BlockSpec((tm, d), lambda i: (i, 0)),
                  pl.BlockSpec((1, d), lambda i: (0, 0))],
        out_specs=pl.BlockSpec((tm, d), lambda i: (i, 0)),
        out_shape=jax.ShapeDtypeStruct((m, d), BF16),
        compiler_params=_cparams(("parallel",)),
        name="rmsnorm",
    )(x, g.reshape(1, d))


def _cast_kernel(x_ref, o_ref):
    o_ref[...] = x_ref[...].astype(o_ref.dtype)


def _cast_rows(x, dtype, tm):
    m, d = x.shape
    assert m % tm == 0
    return pl.pallas_call(
        _cast_kernel,
        grid=(m // tm,),
        in_specs=[pl.BlockSpec((tm, d), lambda i: (i, 0))],
        out_specs=pl.BlockSpec((tm, d), lambda i: (i, 0)),
        out_shape=jax.ShapeDtypeStruct((m, d), dtype),
        compiler_params=_cparams(("parallel",)),
        name="cast_rows",
    )(x)


def _sigmoid(x):
    return 0.5 * (jnp.tanh(0.5 * x) + 1.0)


def _group_rmsnorm(y, g, width):
    outs = []
    for c in range(0, y.shape[1], width):
        yc = y[:, c:c + width]
        yn = yc * lax.rsqrt(jnp.mean(yc * yc, axis=-1, keepdims=True) + RMS_EPS)
        outs.append(yn * g[:, c:c + width])
    return outs[0] if len(outs) == 1 else jnp.concatenate(outs, axis=1)


def _ep_bias(acc, b):
    return (acc + b,)


def _ep_bias_bf16copy(acc, b):
    y = acc + b
    return (y, y.astype(BF16))


def _ep_bias_sigmoid(acc, b):
    return (_sigmoid(acc + b),)


def _ep_knorm(acc, b, g):
    y = _group_rmsnorm(acc + b, g, DIFF_DQK)
    return (y, y.astype(BF16))


def _ep_qnorm_scaled(acc, b, g):
    y = _group_rmsnorm(acc + b, g, DIFF_DQK) * (DIFF_DQK ** -0.5)
    return (y, y.astype(BF16))


def _ep_residual(acc, r):
    return (r + acc,)


def _row_chunks(tm):
    step = ROW_CHUNK if tm % ROW_CHUNK == 0 else tm
    return [slice(r, r + step) for r in range(0, tm, step)]


def _mm_kernel(a_ref, w_ref, *rest, n_extra, n_out, epilogue, tile_extra):
    extras = rest[:n_extra]
    outs = rest[n_extra:n_extra + n_out]
    w = w_ref[...].astype(BF16)
    for rows in _row_chunks(a_ref.shape[0]):
        acc = jnp.dot(a_ref[rows, :].astype(BF16), w, preferred_element_type=F32)
        res = epilogue(acc, *[e[rows, :] if t else e[...] for e, t in zip(extras, tile_extra)])
        for o, r in zip(outs, res):
            o[rows, :] = r.astype(o.dtype)


def _matmul(a, w, extras, epilogue, out_dtypes, *, tm, tn, name, layer=None, col0=0, n=None):
    m, k = a.shape
    n = w.shape[-1] - col0 if n is None else n
    tm = min(tm, m)
    tn = min(tn, n)
    assert m % tm == 0 and n % tn == 0 and col0 % tn == 0, (m, n, tm, tn, col0)
    jb = col0 // tn
    if w.ndim == 2:
        w_spec = pl.BlockSpec((k, tn), lambda i, j: (layer or 0, jb + j))
    else:
        w_spec = pl.BlockSpec((None, k, tn), lambda i, j: (layer, 0, jb + j))
    in_specs = [pl.BlockSpec((tm, k), lambda i, j: (i, 0)), w_spec]
    tile_extra = []
    for e in extras:
        tile_extra.append(e.shape[0] != 1)
        if e.shape[0] == 1:
            in_specs.append(pl.BlockSpec((1, tn), lambda i, j: (0, j)))
        else:
            in_specs.append(pl.BlockSpec((tm, tn), lambda i, j: (i, j)))
    out_specs = [pl.BlockSpec((tm, tn), lambda i, j: (i, j)) for _ in out_dtypes]
    out_shape = [jax.ShapeDtypeStruct((m, n), dt) for dt in out_dtypes]
    return pl.pallas_call(
        functools.partial(_mm_kernel, n_extra=len(extras), n_out=len(out_dtypes), epilogue=epilogue,
                          tile_extra=tuple(tile_extra)),
        grid=(m // tm, n // tn),
        in_specs=in_specs,
        out_specs=out_specs,
        out_shape=out_shape,
        compiler_params=_cparams(("parallel", "arbitrary")),
        name=name,
    )(a, w, *extras)


def _merge_kernel(hm_ref, hd_ref, hp_ref, wm_ref, wd_ref, wp_ref, g0_ref, g1_ref, g2_ref, o_ref):
    wm, wd, wp = (w[...].astype(BF16) for w in (wm_ref, wd_ref, wp_ref))
    for rows in _row_chunks(o_ref.shape[0]):
        proj = lambda h_ref, w: jnp.dot(h_ref[rows, :].astype(BF16), w, preferred_element_type=F32)
        merged = (g0_ref[rows, :] * proj(hm_ref, wm)
                  + g1_ref[rows, :] * proj(hd_ref, wd)
                  + g2_ref[rows, :] * proj(hp_ref, wp))
        o_ref[rows, :] = merged.astype(o_ref.dtype)


def _merge(hm, hd, hp, wm, wd, wp, gates, *, layer, tm, tn):
    m = hm.shape[0]
    tm = min(tm, m)
    nj = D_MODEL // tn
    row = lambda width: pl.BlockSpec((tm, width), lambda i, j: (i, 0))
    col = lambda depth: pl.BlockSpec((depth, tn), lambda i, j: (layer, j))
    gate = lambda br: pl.BlockSpec((tm, tn), lambda i, j: (i, br * nj + j))
    return pl.pallas_call(
        _merge_kernel,
        grid=(m // tm, nj),
        in_specs=[row(MLSTM_V_W), row(DIFF_V_W), row(POOL_CH),
                  col(MLSTM_V_W), col(DIFF_V_W), col(POOL_CH),
                  gate(0), gate(1), gate(2)],
        out_specs=pl.BlockSpec((tm, tn), lambda i, j: (i, j)),
        out_shape=jax.ShapeDtypeStruct((m, D_MODEL), BF16),
        compiler_params=_cparams(("parallel", "arbitrary")),
        name="merge",
    )(hm, hd, hp, wm, wd, wp, gates, gates, gates)


def _conv_gate(ug, uv, pg, pv, cw_g, cw_v, cb_g, cb_v):
    cg = cb_g + (cw_g[0:1] * pg[0] + cw_g[1:2] * pg[1] + cw_g[2:3] * ug)
    cv = cb_v + (cw_v[0:1] * pv[0] + cw_v[1:2] * pv[1] + cw_v[2:3] * uv)
    half = 0.5 * cg
    return (half * (jnp.tanh(half) + 1.0)) * cv


def _ffn_up_prompt_kernel(a_ref, wg_ref, wv_ref, cwg_ref, cwv_ref, cbg_ref, cbv_ref,
                          act_ref, tail_ref, carry_ref, *, tiles_per_seq):
    i = pl.program_id(0)
    j = pl.program_id(1)
    tm = a_ref.shape[0]

    @pl.when(i % tiles_per_seq == 0)
    def _():
        carry_ref[:, pl.ds(j, 1)] = jnp.zeros((2, 1) + carry_ref.shape[2:], F32)

    wg = wg_ref[...].astype(BF16)
    wv = wv_ref[...].astype(BF16)
    tails = [carry_ref[0, j], carry_ref[1, j]]
    r8 = lax.broadcasted_iota(jnp.int32, (SUBLANES, wg.shape[1]), 0)

    def back(u, tail8):
        out = []
        for k in (2, 1):
            rolled = pltpu.roll(u, k, 0)
            head = rolled[:SUBLANES]
            for r in range(k):
                head = jnp.where(r8 == r, tail8[SUBLANES - k + r:SUBLANES - k + r + 1], head)
            out.append(jnp.concatenate([head, rolled[SUBLANES:]], axis=0))
        return tuple(out)

    for rows in _row_chunks(tm):
        a = a_ref[rows, :]
        ug = jnp.dot(a, wg, preferred_element_type=F32)
        uv = jnp.dot(a, wv, preferred_element_type=F32)
        act = _conv_gate(ug, uv, back(ug, tails[0]), back(uv, tails[1]),
                         cwg_ref[...], cwv_ref[...], cbg_ref[...], cbv_ref[...])
        act_ref[rows, :] = act.astype(act_ref.dtype)
        tails = [ug[ug.shape[0] - SUBLANES:], uv[uv.shape[0] - SUBLANES:]]
    for half in range(2):
        carry_ref[half, j] = tails[half]
        tail_ref[0, half] = tails[half]


def _ffn_up_prompt(h, w_up, conv_w, conv_b, *, layer, batch, tm, tn):
    m, d = h.shape
    nj = D_FF // tn
    tiles_per_seq = (m // batch) // tm
    conv_b = conv_b.reshape(1, 2 * D_FF)
    act, tail = pl.pallas_call(
        functools.partial(_ffn_up_prompt_kernel, tiles_per_seq=tiles_per_seq),
        grid=(m // tm, nj),
        in_specs=[pl.BlockSpec((tm, d), lambda i, j: (i, 0)),
                  pl.BlockSpec((None, d, tn), lambda i, j: (layer, 0, j)),
                  pl.BlockSpec((None, d, tn), lambda i, j: (layer, 0, nj + j)),
                  pl.BlockSpec((CONV_W, tn), lambda i, j: (0, j)),
                  pl.BlockSpec((CONV_W, tn), lambda i, j: (0, nj + j)),
                  pl.BlockSpec((1, tn), lambda i, j: (0, j)),
                  pl.BlockSpec((1, tn), lambda i, j: (0, nj + j))],
        out_specs=[pl.BlockSpec((tm, tn), lambda i, j: (i, j)),
                   pl.BlockSpec((1, 2, SUBLANES, tn), lambda i, j: (i, 0, 0, j))],
        out_shape=[jax.ShapeDtypeStruct((m, D_FF), BF16),
                   jax.ShapeDtypeStruct((m // tm, 2, SUBLANES, D_FF), F32)],
        scratch_shapes=[pltpu.VMEM((2, nj, SUBLANES, tn), F32)],
        compiler_params=_cparams(("arbitrary", "arbitrary")),
        name="ffn_up_prompt",
    )(h, w_up, w_up, conv_w, conv_w, conv_b, conv_b)
    tail = tail[tiles_per_seq - 1::tiles_per_seq]
    conv_new = jnp.transpose(tail[:, :, SUBLANES - (CONV_W - 1):, :], (0, 2, 1, 3))
    return act, conv_new.reshape(batch, CONV_W - 1, 2 * D_FF)


def _ffn_up_sample_kernel(a_ref, wg_ref, wv_ref, cwg_ref, cwv_ref, cbg_ref, cbv_ref,
                          p2g_ref, p1g_ref, p2v_ref, p1v_ref, act_ref, ug_ref, uv_ref):
    a = a_ref[...].astype(BF16)
    ug = jnp.dot(a, wg_ref[...].astype(BF16), preferred_element_type=F32)
    uv = jnp.dot(a, wv_ref[...].astype(BF16), preferred_element_type=F32)
    act = _conv_gate(ug, uv, (p2g_ref[...], p1g_ref[...]), (p2v_ref[...], p1v_ref[...]),
                     cwg_ref[...], cwv_ref[...], cbg_ref[...], cbv_ref[...])
    act_ref[...] = act.astype(act_ref.dtype)
    ug_ref[...] = ug
    uv_ref[...] = uv


def _ffn_up_sample(h, w_up, conv_w, conv_b, conv_buf, *, layer, tn):
    b, d = h.shape
    nj = D_FF // tn
    conv_b = conv_b.reshape(1, 2 * D_FF)
    prev2 = conv_buf[:, 0, :]
    prev1 = conv_buf[:, 1, :]
    lo = lambda rows: pl.BlockSpec((rows, tn), lambda j: (0, j))
    hi = lambda rows: pl.BlockSpec((rows, tn), lambda j: (0, nj + j))
    act, ug, uv = pl.pallas_call(
        _ffn_up_sample_kernel,
        grid=(nj,),
        in_specs=[pl.BlockSpec((b, d), lambda j: (0, 0)),
                  pl.BlockSpec((None, d, tn), lambda j: (layer, 0, j)),
                  pl.BlockSpec((None, d, tn), lambda j: (layer, 0, nj + j)),
                  lo(CONV_W), hi(CONV_W), lo(1), hi(1),
                  lo(b), lo(b), hi(b), hi(b)],
        out_specs=[lo(b), lo(b), lo(b)],
        out_shape=[jax.ShapeDtypeStruct((b, D_FF), BF16),
                   jax.ShapeDtypeStruct((b, D_FF), F32),
                   jax.ShapeDtypeStruct((b, D_FF), F32)],
        compiler_params=_cparams(("arbitrary",)),
        name="ffn_up_sample",
    )(h, w_up, w_up, conv_w, conv_w, conv_b, conv_b, prev2, prev1, prev2, prev1)
    u = jnp.concatenate([ug, uv], axis=1)
    return act, jnp.concatenate([conv_buf[:, 1:], u[:, None, :]], axis=1)


def _log_sigmoid(x):
    return -(jnp.maximum(-x, 0.0) + jnp.log1p(jnp.exp(-jnp.abs(x))))


def _cumsum_rows(x):
    n = x.shape[0]
    row = lax.broadcasted_iota(jnp.int32, x.shape, 0)
    d = 1
    while d < n:
        x = x + jnp.where(row >= d, pltpu.roll(x, d, 0), 0.0)
        d *= 2
    return x


def _head_out_norm(h, g, o_gate):
    y = h * lax.rsqrt(jnp.mean(h * h, axis=-1, keepdims=True) + RMS_EPS)
    return (y * g) * _sigmoid(o_gate)


def _mlstm_prompt_kernel(qkv_ref, mif_ref, mo_ref, nw_ref, hm_ref, c_out, n_out, m_out,
                         c_sc, n_sc, m_sc, *, cs, nc):
    ci = pl.program_id(0)

    @pl.when(ci == 0)
    def _():
        c_sc[...] = jnp.zeros(c_sc.shape, F32)
        n_sc[...] = jnp.zeros(n_sc.shape, F32)
        m_sc[...] = jnp.zeros(m_sc.shape, F32)

    row = lax.broadcasted_iota(jnp.int32, (cs, cs), 0)
    col = lax.broadcasted_iota(jnp.int32, (cs, cs), 1)
    causal = col <= row
    eye = col == row
    lane = lax.broadcasted_iota(jnp.int32, (1, LANES), 1)
    for bi in range(qkv_ref.shape[0]):
        m_vec = _mlstm_chunk(bi, qkv_ref, mif_ref, mo_ref, nw_ref, hm_ref, c_sc, n_sc, m_sc,
                             causal, eye, lane, cs)

        @pl.when(ci == nc - 1)
        def _():
            m_out[bi] = m_vec

    @pl.when(ci == nc - 1)
    def _():
        c_out[...] = c_sc[...]
        n_out[...] = n_sc[...][:, :, 0, :]


def _mlstm_chunk(bi, qkv_ref, mif_ref, mo_ref, nw_ref, hm_ref, c_sc, n_sc, m_sc, causal, eye, lane, cs):
    mif = mif_ref[bi]
    b_all = _cumsum_rows(_log_sigmoid(mif))
    m_vec = jnp.zeros((1, LANES), F32)
    for h in range(MLSTM_HEADS):
        q = qkv_ref[bi, :, h * MLSTM_DQK:(h + 1) * MLSTM_DQK]
        k = qkv_ref[bi, :, MLSTM_QK_W + h * MLSTM_DQK:MLSTM_QK_W + (h + 1) * MLSTM_DQK] * (MLSTM_DQK ** -0.5)
        v = qkv_ref[bi, :, 2 * MLSTM_QK_W + h * MLSTM_DV:2 * MLSTM_QK_W + (h + 1) * MLSTM_DV]
        li = mif[:, h:h + 1]
        b = b_all[:, MLSTM_HEADS + h:MLSTM_HEADS + h + 1]
        g_row = jnp.sum(jnp.where(eye, li - b, 0.0), axis=0, keepdims=True)
        dmat = jnp.where(causal, b + g_row, NEG)
        m_prev = m_sc[bi, h]
        inter = b + m_prev
        m_t = jnp.maximum(inter, jnp.max(dmat, axis=-1, keepdims=True))
        w_inter = jnp.exp(inter - m_t)
        qb, kb, vb = q.astype(BF16), k.astype(BF16), v.astype(BF16)
        s = lax.dot_general(qb, kb, (((1,), (1,)), ((), ())), preferred_element_type=F32)
        sc = s * jnp.exp(dmat - m_t)
        c_prev = c_sc[bi, h]
        n_prev = n_sc[bi, h]
        num = (w_inter * jnp.dot(qb, c_prev.astype(BF16), preferred_element_type=F32)
               + jnp.dot(sc.astype(BF16), vb, preferred_element_type=F32))
        den = w_inter * jnp.sum(q * n_prev, axis=-1, keepdims=True) + jnp.sum(sc, axis=-1, keepdims=True)
        hh = num / jnp.maximum(jnp.abs(den), jnp.exp(-m_t))
        m_new = m_t[cs - 1:cs]
        b_last = b[cs - 1:cs]
        g_state = jnp.exp(b_last + m_prev - m_new)
        g_tok = jnp.exp(b_last - b + li - m_new)
        kg = k * g_tok
        c_new = g_state * c_prev + lax.dot_general(kg.astype(BF16), vb, (((0,), (0,)), ((), ())),
                                                   preferred_element_type=F32)
        n_new = g_state * n_prev + jnp.sum(kg, axis=0, keepdims=True)
        c_sc[bi, h] = c_new
        n_sc[bi, h] = n_new
        m_sc[bi, h] = m_new
        m_vec = m_vec + jnp.where(lane == h, m_new, 0.0)
        sl = slice(h * MLSTM_DV, (h + 1) * MLSTM_DV)
        hm_ref[bi, :, sl] = _head_out_norm(hh, nw_ref[:, sl], mo_ref[bi, :, sl]).astype(hm_ref.dtype)
    return m_vec


def _mlstm_prompt(mqkv, mif, mo, norm_w, *, batch):
    m = mqkv.shape[0]
    seq = m // batch
    cs = MLSTM_CHUNK if seq % MLSTM_CHUNK == 0 else seq
    nc = seq // cs
    per_seq = lambda a: a.reshape(batch, seq, a.shape[-1])
    rows = lambda width: pl.BlockSpec((batch, cs, width), lambda c: (0, c, 0))
    whole = lambda shape: pl.BlockSpec(shape, lambda c: (0,) * len(shape))
    state_shapes = [(batch, MLSTM_HEADS, MLSTM_DQK, MLSTM_DV), (batch, MLSTM_HEADS, MLSTM_DQK),
                    (batch, 1, LANES)]
    hm, c, n, mm = pl.pallas_call(
        functools.partial(_mlstm_prompt_kernel, cs=cs, nc=nc),
        grid=(nc,),
        in_specs=[rows(mqkv.shape[1]), rows(LANES), rows(MLSTM_V_W), whole((1, MLSTM_V_W))],
        out_specs=[rows(MLSTM_V_W)] + [whole(s) for s in state_shapes],
        out_shape=[jax.ShapeDtypeStruct((batch, seq, MLSTM_V_W), BF16)]
        + [jax.ShapeDtypeStruct(s, F32) for s in state_shapes],
        scratch_shapes=[pltpu.VMEM((batch, MLSTM_HEADS, MLSTM_DQK, MLSTM_DV), F32),
                        pltpu.VMEM((batch, MLSTM_HEADS, 1, MLSTM_DQK), F32),
                        pltpu.VMEM((batch, MLSTM_HEADS, 1, 1), F32)],
        compiler_params=_cparams(("arbitrary",)),
        name="mlstm_prompt",
    )(per_seq(mqkv), per_seq(mif), per_seq(mo), norm_w.reshape(1, MLSTM_V_W))
    return hm.reshape(m, MLSTM_V_W), c, n, mm[:, 0, :MLSTM_HEADS]


def _lanes_to_rows(x_row, eye):
    return jnp.sum(jnp.where(eye, x_row, 0.0), axis=1, keepdims=True)


def _mlstm_sample_kernel(qkv_ref, mif_ref, mo_ref, nw_ref, c_ref, n_ref, m_ref,
                         hm_ref, c_out, n_out, m_out):
    mif = mif_ref[0]
    lf_all = _log_sigmoid(mif)
    m_all = m_ref[0]
    row = lax.broadcasted_iota(jnp.int32, (MLSTM_DQK, MLSTM_DQK), 0)
    col = lax.broadcasted_iota(jnp.int32, (MLSTM_DQK, MLSTM_DQK), 1)
    eye = row == col
    lane = lax.broadcasted_iota(jnp.int32, (1, LANES), 1)
    m_vec = jnp.zeros((1, LANES), F32)
    for h in range(MLSTM_HEADS):
        q = qkv_ref[0, :, h * MLSTM_DQK:(h + 1) * MLSTM_DQK]
        k = qkv_ref[0, :, MLSTM_QK_W + h * MLSTM_DQK:MLSTM_QK_W + (h + 1) * MLSTM_DQK] * (MLSTM_DQK ** -0.5)
        v = qkv_ref[0, :, 2 * MLSTM_QK_W + h * MLSTM_DV:2 * MLSTM_QK_W + (h + 1) * MLSTM_DV]
        li = mif[:, h:h + 1]
        lf = lf_all[:, MLSTM_HEADS + h:MLSTM_HEADS + h + 1]
        m_prev = m_all[:, h:h + 1]
        inter = lf + m_prev
        m_t = jnp.maximum(inter, li)
        w_inter = jnp.exp(inter - m_t)
        sc = jnp.sum(q * k, axis=-1, keepdims=True) * jnp.exp(li - m_t)
        c_prev = c_ref[0, h]
        n_prev = n_ref[0, h:h + 1, :]
        q_col = _lanes_to_rows(q, eye)
        k_col = _lanes_to_rows(k, eye)
        num = w_inter * jnp.sum(q_col * c_prev, axis=0, keepdims=True) + sc * v
        den = w_inter * jnp.sum(q * n_prev, axis=-1, keepdims=True) + sc
        hh = num / jnp.maximum(jnp.abs(den), jnp.exp(-m_t))
        g_tok = jnp.exp(li - m_t)
        c_out[0, h] = w_inter * c_prev + (g_tok * k_col) * v
        n_out[0, h:h + 1, :] = w_inter * n_prev + g_tok * k
        m_vec = m_vec + jnp.where(lane == h, m_t, 0.0)
        sl = slice(h * MLSTM_DV, (h + 1) * MLSTM_DV)
        hm_ref[0, :, sl] = _head_out_norm(hh, nw_ref[:, sl], mo_ref[0, :, sl]).astype(hm_ref.dtype)
    m_out[0] = m_vec


def _mlstm_sample(mqkv, mif, mo, norm_w, c0, n0, m0):
    b = mqkv.shape[0]
    m0p = jnp.pad(m0, ((0, 0), (0, LANES - MLSTM_HEADS))).reshape(b, 1, LANES)
    per_seq = lambda width: pl.BlockSpec((1, 1, width), lambda i: (i, 0, 0))
    c_spec = pl.BlockSpec((1, MLSTM_HEADS, MLSTM_DQK, MLSTM_DV), lambda i: (i, 0, 0, 0))
    n_spec = pl.BlockSpec((1, MLSTM_HEADS, MLSTM_DQK), lambda i: (i, 0, 0))
    hm, c, n, mm = pl.pallas_call(
        _mlstm_sample_kernel,
        grid=(b,),
        in_specs=[per_seq(mqkv.shape[1]), per_seq(LANES), per_seq(MLSTM_V_W),
                  pl.BlockSpec((1, MLSTM_V_W), lambda i: (0, 0)),
                  c_spec, n_spec, per_seq(LANES)],
        out_specs=[per_seq(MLSTM_V_W), c_spec, n_spec, per_seq(LANES)],
        out_shape=[jax.ShapeDtypeStruct((b, 1, MLSTM_V_W), F32),
                   jax.ShapeDtypeStruct(c0.shape, F32),
                   jax.ShapeDtypeStruct(n0.shape, F32),
                   jax.ShapeDtypeStruct((b, 1, LANES), F32)],
        compiler_params=_cparams(("parallel",)),
        name="mlstm_sample",
    )(mqkv.reshape(b, 1, -1), mif.reshape(b, 1, LANES), mo.reshape(b, 1, -1),
      norm_w.reshape(1, MLSTM_V_W), c0, n0, m0p)
    return hm.reshape(b, MLSTM_V_W), c, n, mm[:, 0, :MLSTM_HEADS]


def _bucket_by_distance():
    max_exact = REL_BUCKETS // 2
    n = np.arange(BIAS_SPAN + 1)
    large = max_exact + np.floor(np.log(np.maximum(n, 1) / max_exact) / math.log(REL_MAX_DIST / max_exact)
                                 * (REL_BUCKETS - max_exact)).astype(np.int64)
    return np.where(n < max_exact, n, np.minimum(large, REL_BUCKETS - 1))


def _bucket_starts():
    bucket = _bucket_by_distance()
    return [int(np.argmax(bucket >= b)) for b in range(1, REL_BUCKETS)]


def _bias_by_distance(rel_bias):
    tab = jnp.transpose(rel_bias.astype(F32)[_bucket_by_distance()], (1, 0))
    return tab - tab[:, BIAS_SPAN:]


def _lambda(lq_ref, lk_ref, lam_init):
    lq = lq_ref[...]
    lk = lk_ref[...]
    e0 = jnp.exp(jnp.sum(lq[0:1] * lk[0:1], axis=-1, keepdims=True))
    e1 = jnp.exp(jnp.sum(lq[1:2] * lk[1:2], axis=-1, keepdims=True))
    return e0 - e1 + lam_init


def _diff_out_norm(d, sub, lam_init):
    y = d * lax.rsqrt(jnp.mean(d * d, axis=-1, keepdims=True) + RMS_EPS)
    return (y * sub) * (1.0 - lam_init)


def _bias_tiles_kernel(rb_ref, o_ref, *, tq):
    h = pl.program_id(0)
    blk = pl.program_id(1)
    row = lax.broadcasted_iota(jnp.int32, (tq, tq), 0)
    col = lax.broadcasted_iota(jnp.int32, (tq, tq), 1)
    dist = row - col + blk * tq
    far = rb_ref[h, REL_BUCKETS - 1]
    val = jnp.full((tq, tq), rb_ref[h, 0] - far, F32)
    for b, start in enumerate(_bucket_starts(), start=1):
        val = jnp.where(dist >= start, rb_ref[h, b] - far, val)
    o_ref[0, 0] = jnp.where(dist >= 0, val, NEG)


def _bias_tiles(rel_bias, tq):
    assert tq + 1 >= _bucket_starts()[-1]
    return pl.pallas_call(
        functools.partial(_bias_tiles_kernel, tq=tq),
        grid=(DIFF_HEADS, 2),
        in_specs=[pl.BlockSpec(memory_space=pltpu.SMEM)],
        out_specs=pl.BlockSpec((1, 1, tq, tq), lambda h, blk: (h, blk, 0, 0)),
        out_shape=jax.ShapeDtypeStruct((DIFF_HEADS, 2, tq, tq), F32),
        compiler_params=_cparams(("parallel", "parallel")),
        name="bias_tiles",
    )(jnp.transpose(rel_bias.astype(F32)))


def _attn_prompt_kernel(qi_ref, ki_ref, q_ref, k_ref, v_ref, bias_ref, lq_ref, lk_ref, sub_ref, o_ref,
                        m_sc, l_sc, acc_sc, *, lam_init):
    qi = qi_ref[pl.program_id(2)]
    ki = ki_ref[pl.program_id(2)]

    @pl.when(ki == 0)
    def _():
        m_sc[...] = jnp.full(m_sc.shape, NEG, F32)
        l_sc[...] = jnp.zeros(l_sc.shape, F32)
        acc_sc[...] = jnp.zeros(acc_sc.shape, F32)

    def tile(bias):
        v = v_ref[0]
        for mp in range(2):
            sl = slice(mp * DIFF_DQK, (mp + 1) * DIFF_DQK)
            s = lax.dot_general(q_ref[0, :, sl], k_ref[0, :, sl], (((1,), (1,)), ((), ())),
                                preferred_element_type=F32)
            if bias is not None:
                s = s + bias
            m_prev = m_sc[mp]
            m_new = jnp.maximum(m_prev, jnp.max(s, axis=-1, keepdims=True))
            alpha = jnp.exp(m_prev - m_new)
            p = jnp.exp(s - m_new)
            l_sc[mp] = alpha * l_sc[mp] + jnp.sum(p, axis=-1, keepdims=True)
            acc_sc[mp] = alpha * acc_sc[mp] + jnp.dot(p.astype(BF16), v, preferred_element_type=F32)
            m_sc[mp] = m_new

    @pl.when(ki < qi - 1)
    def _():
        tile(None)

    @pl.when((ki >= qi - 1) & (ki <= qi))
    def _():
        tile(bias_ref[0, 0])

    @pl.when(ki == qi)
    def _():
        lam = _lambda(lq_ref, lk_ref, lam_init)
        d = acc_sc[0] / l_sc[0] - lam * (acc_sc[1] / l_sc[1])
        o_ref[0] = _diff_out_norm(d, sub_ref[...], lam_init).astype(o_ref.dtype)


def _attn_prompt(q, k, v, bias, lam_q, lam_k, subln, *, batch, lam_init):
    m = q.shape[0]
    seq = m // batch
    tq = bias.shape[-1]
    assert seq % tq == 0
    nq = seq // tq
    hw = 2 * DIFF_DQK
    q3, k3, v3 = (a.reshape(batch, seq, -1) for a in (q, k, v))
    pairs = [(qi, ki) for qi in range(nq) for ki in range(qi + 1)]
    qi_tab = jnp.asarray([p[0] for p in pairs], jnp.int32)
    ki_tab = jnp.asarray([p[1] for p in pairs], jnp.int32)
    q_spec = pl.BlockSpec((1, tq, hw), lambda b, h, t, qt, kt: (b, qt[t], h))
    kv_spec = pl.BlockSpec((1, tq, hw), lambda b, h, t, qt, kt: (b, kt[t], h))
    const = lambda shape: pl.BlockSpec(shape, lambda b, h, t, qt, kt: (0,) * len(shape))
    out = pl.pallas_call(
        functools.partial(_attn_prompt_kernel, lam_init=lam_init),
        grid_spec=pltpu.PrefetchScalarGridSpec(
            num_scalar_prefetch=2,
            grid=(batch, DIFF_HEADS, len(pairs)),
            in_specs=[q_spec, kv_spec, kv_spec,
                      pl.BlockSpec((1, 1, tq, tq),
                                   lambda b, h, t, qt, kt: (h, jnp.minimum(qt[t] - kt[t], 1), 0, 0)),
                      const((2, DIFF_DQK)), const((2, DIFF_DQK)), const((1, DIFF_DV))],
            out_specs=pl.BlockSpec((1, tq, DIFF_DV), lambda b, h, t, qt, kt: (b, qt[t], h)),
            scratch_shapes=[pltpu.VMEM((2, tq, 1), F32), pltpu.VMEM((2, tq, 1), F32),
                            pltpu.VMEM((2, tq, DIFF_DV), F32)]),
        out_shape=jax.ShapeDtypeStruct((batch, seq, DIFF_V_W), BF16),
        compiler_params=_cparams(("parallel", "parallel", "arbitrary")),
        name="attn_prompt",
    )(qi_tab, ki_tab, q3, k3, v3, bias, lam_q, lam_k, subln.reshape(1, DIFF_DV))
    return out.reshape(m, DIFF_V_W)


N_MAPS = 2 * DIFF_HEADS
PAGE_ROWS = PAGE_SIZE * DIFF_HEADS
NEW_TOKENS = 16
DECODE_PAGES_PER_STEP = 8


def _attn_sample_kernel(pt_ref, qm_ref, kn_ref, vn_ref, *rest, n_steps, pages, lam_init):
    kc_refs, vc_refs = rest[:pages], rest[pages:2 * pages]
    (bfar_ref, blast_ref, bnew_ref, lq_ref, lk_ref, sub_ref, o_ref,
     qx_sc, m_sc, l_sc, acc_sc) = rest[2 * pages:]
    p = pl.program_id(1)
    sub_i = lax.broadcasted_iota(jnp.int32, (DIFF_HEADS, LANES), 0)
    lane_i = lax.broadcasted_iota(jnp.int32, (DIFF_HEADS, LANES), 1)
    own = (lane_i == 2 * sub_i) | (lane_i == 2 * sub_i + 1)

    @pl.when(p == 0)
    def _():
        qm = qm_ref[0]
        r = lax.broadcasted_iota(jnp.int32, qm.shape, 0)
        first = (r & 1) == 0
        qx = jnp.concatenate([jnp.where(first, qm, 0.0), jnp.where(first, 0.0, qm)], axis=1)
        qx_sc[...] = jnp.zeros(qx_sc.shape, BF16)
        qx_sc[0:N_MAPS, :] = qx.astype(BF16)
        m_sc[...] = jnp.where(own, NEG, 0.0)
        l_sc[...] = jnp.zeros(l_sc.shape, F32)
        acc_sc[...] = jnp.zeros(acc_sc.shape, F32)

    row = lax.broadcasted_iota(jnp.int32, (LANES, LANES), 0)
    col = lax.broadcasted_iota(jnp.int32, (LANES, LANES), 1)
    eye = row == col

    def per_lane_column(x):
        x_row = jnp.sum(jnp.where(own, x, 0.0), axis=0, keepdims=True)
        return jnp.sum(jnp.where(eye, x_row, 0.0), axis=1, keepdims=True)

    def group_update(blocks):
        scores = []
        m_prev = m_sc[...]
        m_new = m_prev
        for kf, _, bias3 in blocks:
            s_all = lax.dot_general(kf.astype(BF16), qx_sc[...], (((1,), (1,)), ((), ())),
                                    preferred_element_type=F32)
            s3 = s_all.reshape(kf.shape[0] // DIFF_HEADS, DIFF_HEADS, LANES) + bias3
            m_new = jnp.maximum(m_new, jnp.max(s3, axis=0))
            scores.append(s3)
        alpha = jnp.exp(m_prev - m_new)
        l_new = alpha * l_sc[...]
        pv = None
        for s3, (_, vf, _) in zip(scores, blocks):
            p3 = jnp.exp(s3 - m_new[None])
            l_new = l_new + jnp.sum(p3, axis=0)
            p_t = p3.reshape(vf.shape[0], LANES).T.astype(BF16)
            part = jnp.dot(p_t, vf.astype(BF16), preferred_element_type=F32)
            pv = part if pv is None else pv + part
        l_sc[...] = l_new
        m_sc[...] = m_new
        acc_sc[...] = per_lane_column(alpha) * acc_sc[...] + pv

    far = bfar_ref[...][None]

    @pl.when(p < n_steps - 1)
    def _():
        group_update([(kc[...], vc[...], far) for kc, vc in zip(kc_refs, vc_refs)])

    @pl.when(p == n_steps - 1)
    def _():
        biases = [far] * (pages - 1) + [blast_ref[...].reshape(PAGE_SIZE, DIFF_HEADS, LANES)]
        blocks = [(kc[...], vc[...], b3) for kc, vc, b3 in zip(kc_refs, vc_refs, biases)]
        blocks.append((kn_ref[0], vn_ref[0], bnew_ref[...].reshape(NEW_TOKENS, DIFF_HEADS, LANES)))
        group_update(blocks)
        lam = _lambda(lq_ref, lk_ref, lam_init)
        l_col = per_lane_column(l_sc[...])
        out = acc_sc[0:N_MAPS, :] / l_col[0:N_MAPS]
        for h in range(DIFF_HEADS):
            d = out[2 * h:2 * h + 1] - lam * out[2 * h + 1:2 * h + 2]
            sl = slice(h * DIFF_DV, (h + 1) * DIFF_DV)
            o_ref[0, :, sl] = _diff_out_norm(d, sub_ref[...], lam_init).astype(o_ref.dtype)


def _attn_sample(q, k_new, v_new, cache_k, cache_v, layer, page_table, bias_tab, lam_q, lam_k, subln,
                 *, lam_init):
    b, n_pages = page_table.shape
    assert PAGE_SIZE >= BIAS_SPAN
    pages = DECODE_PAGES_PER_STEP if n_pages % DECODE_PAGES_PER_STEP == 0 else 1
    n_steps = n_pages // pages
    hw = 2 * DIFF_DQK
    sub_i = lax.broadcasted_iota(jnp.int32, (DIFF_HEADS, LANES), 0)
    lane_i = lax.broadcasted_iota(jnp.int32, (DIFF_HEADS, LANES), 1)
    own = (lane_i == 2 * sub_i) | (lane_i == 2 * sub_i + 1)

    def rows_bias(per_token):
        t = per_token.shape[0]
        return jnp.where(own[None], per_token[:, :, None], NEG).reshape(t * DIFF_HEADS, LANES)

    b_far = jnp.where(own, 0.0, NEG)
    b_last = rows_bias(jnp.transpose(bias_tab[:, :0:-1][:, :PAGE_SIZE]))
    b_new = rows_bias(jnp.full((NEW_TOKENS, DIFF_HEADS), NEG, F32).at[0].set(bias_tab[:, 0]))
    pad_page = lambda a: jnp.pad(a.reshape(b, DIFF_HEADS, hw), ((0, 0), (0, (NEW_TOKENS - 1) * DIFF_HEADS), (0, 0)))
    new_spec = pl.BlockSpec((1, NEW_TOKENS * DIFF_HEADS, hw), lambda i, p, pt: (i, 0, 0))
    page = lambda r: pl.BlockSpec((None, None, PAGE_ROWS, hw),
                                  lambda i, p, pt: (layer, pt[i * n_pages + p * pages + r], 0, 0))
    const = lambda shape: pl.BlockSpec(shape, lambda i, p, pt: (0,) * len(shape))
    out = pl.pallas_call(
        functools.partial(_attn_sample_kernel, n_steps=n_steps, pages=pages, lam_init=lam_init),
        grid_spec=pltpu.PrefetchScalarGridSpec(
            num_scalar_prefetch=1,
            grid=(b, n_steps),
            in_specs=[pl.BlockSpec((1, N_MAPS, DIFF_DQK), lambda i, p, pt: (i, 0, 0)),
                      new_spec, new_spec, *[page(r) for r in range(pages)], *[page(r) for r in range(pages)],
                      const((DIFF_HEADS, LANES)), const((PAGE_ROWS, LANES)),
                      const((NEW_TOKENS * DIFF_HEADS, LANES)),
                      const((2, DIFF_DQK)), const((2, DIFF_DQK)), const((1, DIFF_DV))],
            out_specs=pl.BlockSpec((1, 1, DIFF_V_W), lambda i, p, pt: (i, 0, 0)),
            scratch_shapes=[pltpu.VMEM((LANES, hw), BF16),
                            pltpu.VMEM((DIFF_HEADS, LANES), F32), pltpu.VMEM((DIFF_HEADS, LANES), F32),
                            pltpu.VMEM((LANES, DIFF_DV), F32)]),
        out_shape=jax.ShapeDtypeStruct((b, 1, DIFF_V_W), F32),
        compiler_params=_cparams(("parallel", "arbitrary")),
        name="attn_sample",
    )(page_table.reshape(-1), q.reshape(b, N_MAPS, DIFF_DQK), pad_page(k_new), pad_page(v_new),
      *[cache_k] * pages, *[cache_v] * pages, b_far, b_last, b_new, lam_q, lam_k, subln.reshape(1, DIFF_DV))
    return out.reshape(b, DIFF_V_W)


HIST = 16


def _window_sum(ext, w, n):
    s = ext
    d = 1
    while d < w:
        s = s + pltpu.roll(s, d, 0)
        d *= 2
    return s[HIST:HIST + n]


def _pool_prompt_kernel(u_ref, w_ref, sc_ref, o_ref, hist_out, hist_sc, *, tm):
    t = pl.program_id(1)

    @pl.when(t == 0)
    def _():
        hist_sc[...] = jnp.zeros(hist_sc.shape, F32)

    pos = t * tm + lax.broadcasted_iota(jnp.int32, (tm, 1), 0)
    for gi, w in enumerate(POOL_WINDOWS):
        sl = slice(gi * POOL_GC, (gi + 1) * POOL_GC)
        u = u_ref[:, sl]
        ext = jnp.concatenate([hist_sc[:, sl], u], axis=0)
        cnt = jnp.minimum(w, pos + 1).astype(F32)
        pooled = _window_sum(ext, w, tm) / cnt - u
        mixed = jnp.dot(pooled.astype(BF16), w_ref[gi], preferred_element_type=F32)
        o_ref[:, sl] = (mixed * sc_ref[:, sl]).astype(o_ref.dtype)
        new_hist = u[tm - HIST:]
        hist_sc[:, sl] = new_hist
        hist_out[0, :, sl] = new_hist


def _pool_prompt(u, pool_w, pool_scale, *, batch, tm):
    m = u.shape[0]
    seq = m // batch
    tm = min(tm, seq)
    assert tm >= HIST
    tps = seq // tm
    hp, hist = pl.pallas_call(
        functools.partial(_pool_prompt_kernel, tm=tm),
        grid=(batch, tps),
        in_specs=[pl.BlockSpec((tm, POOL_CH), lambda b, t: (b * tps + t, 0)),
                  pl.BlockSpec((POOL_GROUPS, POOL_GC, POOL_GC), lambda b, t: (0, 0, 0)),
                  pl.BlockSpec((1, POOL_CH), lambda b, t: (0, 0))],
        out_specs=[pl.BlockSpec((tm, POOL_CH), lambda b, t: (b * tps + t, 0)),
                   pl.BlockSpec((1, HIST, POOL_CH), lambda b, t: (b, 0, 0))],
        out_shape=[jax.ShapeDtypeStruct((m, POOL_CH), BF16),
                   jax.ShapeDtypeStruct((batch, HIST, POOL_CH), F32)],
        scratch_shapes=[pltpu.VMEM((HIST, POOL_CH), F32)],
        compiler_params=_cparams(("parallel", "arbitrary")),
        name="pool_prompt",
    )(u, pool_w, pool_scale.reshape(1, POOL_CH))
    return hp, hist[:, HIST - POOL_BUF:]


def _pool_sample_kernel(ucat_ref, w_ref, sc_ref, o_ref):
    for gi, w in enumerate(POOL_WINDOWS):
        sl = slice(gi * POOL_GC, (gi + 1) * POOL_GC)
        new = ucat_ref[HIST - 1, :, sl]
        win = new
        for r in range(HIST - w, HIST - 1):
            win = win + ucat_ref[r, :, sl]
        pooled = win / float(w) - new
        mixed = jnp.dot(pooled.astype(BF16), w_ref[gi], preferred_element_type=F32)
        o_ref[:, sl] = (mixed * sc_ref[:, sl]).astype(o_ref.dtype)


def _pool_sample(u, buf, pool_w, pool_scale):
    b = u.shape[0]
    assert buf.shape[1] == HIST - 1
    ucat = jnp.concatenate([buf, u[:, None, :]], axis=1)
    hp = pl.pallas_call(
        _pool_sample_kernel,
        out_shape=jax.ShapeDtypeStruct((b, POOL_CH), F32),
        compiler_params=pltpu.CompilerParams(vmem_limit_bytes=VMEM_LIMIT),
        name="pool_sample",
    )(jnp.transpose(ucat, (1, 0, 2)), pool_w, pool_scale.reshape(1, POOL_CH))
    return hp, ucat[:, 1:]


IN_SPLITS = (MLSTM_QK_W, MLSTM_QK_W, MLSTM_V_W, 2 * MLSTM_HEADS, MLSTM_V_W,
             DIFF_QK_W, DIFF_QK_W, DIFF_V_W, POOL_CH, N_BRANCH * D_MODEL)
IN_OFFS = tuple(int(o) for o in np.cumsum((0,) + IN_SPLITS))
GATE_COLS = (IN_OFFS[3], IN_OFFS[4])


def _prep_weights(p):
    w_in = p["w_in"]
    g0, g1 = GATE_COLS
    pad = ((0, 0), (0, 0), (0, LANES - (g1 - g0)))
    rows = lambda w: w.reshape(-1, w.shape[-1])
    return dict(w_head=rows(w_in[:, :, :g0]), w_tail=rows(w_in[:, :, g1:].astype(BF16)),
                w_mif=rows(jnp.pad(w_in[:, :, g0:g1], pad)),
                w_down=_cast_rows(rows(p["w_down"]), BF16, tm=D_FF // 16),
                **{k: _cast_rows(rows(p[k]), BF16, tm=256) for k in ("w_br_mlstm", "w_br_diff", "w_br_pool")})


def _prep_layer(p, l):
    b_in = p["b_in"][l]
    g0, g1 = GATE_COLS
    pad = ((0, 0), (0, LANES - (g1 - g0)))
    return dict(
        b_tail=b_in[g1:].reshape(1, -1), b_mif=jnp.pad(b_in[g0:g1].reshape(1, -1), pad),
        b_head=b_in[:g0].reshape(1, -1),
        q_norm=jnp.tile(p["q_norm"][l].reshape(1, -1), (1, DIFF_HEADS)),
        k_norm=jnp.tile(p["k_norm"][l].reshape(1, -1), (1, DIFF_HEADS)),
        norm_mix=p["norm_mix"][l], norm_ffn=p["norm_ffn"][l], mlstm_norm=p["mlstm_norm"][l],
        lambda_q=p["lambda_q"][l], lambda_k=p["lambda_k"][l], diff_subln=p["diff_subln"][l],
        pool_w=p["pool_w"][l].astype(BF16), pool_scale=p["pool_scale"][l],
        conv_w=p["conv_w"][l], conv_b=p["conv_b"][l],
    )


def _projections(h, wts, lw, l, *, tm, tn):
    g1 = GATE_COLS[1]

    def tail(seg, epilogue, extras, out_dtypes, name):
        c0, c1 = IN_OFFS[seg] - g1, IN_OFFS[seg + 1] - g1
        return _matmul(h, wts["w_tail"], (lw["b_tail"][:, c0:c1],) + extras, epilogue, out_dtypes,
                       layer=l, tm=tm, tn=tn, col0=c0, n=c1 - c0, name=name)

    out = {}
    (out["mqkv"],) = _matmul(h, wts["w_head"], (lw["b_head"],), _ep_bias, (F32,), layer=l, tm=tm, tn=tn,
                             name="proj_mqkv")
    (out["mif"],) = _matmul(h, wts["w_mif"], (lw["b_mif"],), _ep_bias, (F32,), layer=l, tm=tm, tn=tn,
                            name="proj_mif")
    (out["mo"],) = tail(4, _ep_bias, (), (F32,), "proj_mo")
    out["dq"], out["dq16"] = tail(5, _ep_qnorm_scaled, (lw["q_norm"],), (F32, BF16), "proj_dq")
    out["dk"], out["dk16"] = tail(6, _ep_knorm, (lw["k_norm"],), (F32, BF16), "proj_dk")
    out["dv"], out["dv16"] = tail(7, _ep_bias_bf16copy, (), (F32, BF16), "proj_dv")
    (out["pu"],) = tail(8, _ep_bias, (), (F32,), "proj_pu")
    (out["gates"],) = tail(9, _ep_bias_sigmoid, (), (F32,), "proj_gates")
    return out


ATTN_TQ = 512
TILES_PROMPT = dict(norm=512, proj=(1024, 512), merge=(1024, 256), out=(1024, 512),
                    ffn_up=(1024, 256), ffn_down=(512, 512), pool=512)
TILES_SAMPLE = dict(proj=512, merge=256, out=512, ffn_up=256, ffn_down=256)


def _layer_prompt(x, p, wts, lw, bias_tiles, l, *, batch):
    m = x.shape[0]
    t = TILES_PROMPT
    lam_init = 0.8 - 0.6 * math.exp(-0.3 * l)
    h = _rmsnorm(x, lw["norm_mix"], tm=t["norm"])
    pr = _projections(h, wts, lw, l, tm=t["proj"][0], tn=t["proj"][1])
    hm, c1, n1, m1 = _mlstm_prompt(pr["mqkv"], pr["mif"], pr["mo"], lw["mlstm_norm"], batch=batch)
    hd = _attn_prompt(pr["dq16"], pr["dk16"], pr["dv16"], bias_tiles, lw["lambda_q"], lw["lambda_k"],
                      lw["diff_subln"], batch=batch, lam_init=lam_init)
    hp, pool_new = _pool_prompt(pr["pu"], lw["pool_w"], lw["pool_scale"], batch=batch, tm=t["pool"])
    merged = _merge(hm, hd, hp, wts["w_br_mlstm"], wts["w_br_diff"], wts["w_br_pool"], pr["gates"],
                    layer=l, tm=t["merge"][0], tn=t["merge"][1])
    (x1,) = _matmul(merged, p["w_out"], (x,), _ep_residual, (F32,), layer=l, tm=t["out"][0], tn=t["out"][1],
                    name="out_proj")
    h2 = _rmsnorm(x1, lw["norm_ffn"], tm=t["norm"])
    act, conv_new = _ffn_up_prompt(h2, p["w_up"], lw["conv_w"], lw["conv_b"], layer=l, batch=batch,
                                   tm=t["ffn_up"][0], tn=t["ffn_up"][1])
    (x2,) = _matmul(act, wts["w_down"], (x1,), _ep_residual, (F32,), layer=l, tm=t["ffn_down"][0],
                    tn=t["ffn_down"][1], name="ffn_down")
    seq = m // batch
    kv_shape = (batch, seq, DIFF_HEADS, DIFF_DV)
    return x2, pr["dk"].reshape(kv_shape), pr["dv"].reshape(kv_shape), c1, n1, m1, pool_new, conv_new


def _layer_sample(x, p, wts, lw, bias_tab, l, cache_k, cache_v, page_table, c0, n0, m0, pool_buf, conv_buf):
    b = x.shape[0]
    t = TILES_SAMPLE
    lam_init = 0.8 - 0.6 * math.exp(-0.3 * l)
    h = _rmsnorm(x, lw["norm_mix"], tm=b)
    pr = _projections(h, wts, lw, l, tm=b, tn=t["proj"])
    hm, c1, n1, m1 = _mlstm_sample(pr["mqkv"], pr["mif"], pr["mo"], lw["mlstm_norm"], c0, n0, m0)
    hd = _attn_sample(pr["dq"], pr["dk"], pr["dv"], cache_k, cache_v, l, page_table, bias_tab,
                      lw["lambda_q"], lw["lambda_k"], lw["diff_subln"], lam_init=lam_init)
    hp, pool_new = _pool_sample(pr["pu"], pool_buf, lw["pool_w"], lw["pool_scale"])
    merged = _merge(hm, hd, hp, wts["w_br_mlstm"], wts["w_br_diff"], wts["w_br_pool"], pr["gates"],
                    layer=l, tm=b, tn=t["merge"])
    (x1,) = _matmul(merged, p["w_out"], (x,), _ep_residual, (F32,), layer=l, tm=b, tn=t["out"],
                    name="out_proj_s")
    h2 = _rmsnorm(x1, lw["norm_ffn"], tm=b)
    act, conv_new = _ffn_up_sample(h2, p["w_up"], lw["conv_w"], lw["conv_b"], conv_buf, layer=l, tn=t["ffn_up"])
    (x2,) = _matmul(act, wts["w_down"], (x1,), _ep_residual, (F32,), layer=l, tm=b, tn=t["ffn_down"],
                    name="ffn_down_s")
    kv_shape = (b, 1, DIFF_HEADS, DIFF_DV)
    return x2, pr["dk"].reshape(kv_shape), pr["dv"].reshape(kv_shape), c1, n1, m1, pool_new, conv_new


def kernel(x_prompt, x_sample, cache_k, cache_v, page_table, state_mlstm_c, state_mlstm_n, state_mlstm_m,
           state_pool, state_conv, rel_bias, norm_mix, w_in, b_in, mlstm_norm, q_norm, k_norm, lambda_q,
           lambda_k, diff_subln, pool_w, pool_scale, w_br_mlstm, w_br_diff, w_br_pool, w_out, norm_ffn,
           w_up, conv_w, conv_b, w_down):
    params = dict(norm_mix=norm_mix, w_in=w_in, b_in=b_in, mlstm_norm=mlstm_norm, q_norm=q_norm,
                  k_norm=k_norm, lambda_q=lambda_q, lambda_k=lambda_k, diff_subln=diff_subln,
                  pool_w=pool_w, pool_scale=pool_scale, w_br_mlstm=w_br_mlstm, w_br_diff=w_br_diff,
                  w_br_pool=w_br_pool, w_out=w_out, norm_ffn=norm_ffn, w_up=w_up, conv_w=conv_w,
                  conv_b=conv_b, w_down=w_down)
    depth = w_in.shape[0]
    n_prompt, seq, d = x_prompt.shape
    n_dec = x_sample.shape[0]
    assert x_sample.shape[1] == 1, "sample group decodes one token per sequence"
    bias_tab = _bias_by_distance(rel_bias)
    bias_tiles = _bias_tiles(rel_bias, min(ATTN_TQ, seq))
    ck = cache_k.reshape(cache_k.shape[:2] + (PAGE_ROWS, 2 * DIFF_DQK))
    cv = cache_v.reshape(cache_v.shape[:2] + (PAGE_ROWS, DIFF_DV))
    xp = x_prompt.reshape(n_prompt * seq, d)
    xs = x_sample.reshape(n_dec, d)
    outs_p, outs_s = [], []
    wts = _prep_weights(params)
    for l in range(depth):
        lw = _prep_layer(params, l)
        xp, *rest_p = _layer_prompt(xp, params, wts, lw, bias_tiles, l, batch=n_prompt)
        xs, *rest_s = _layer_sample(xs, params, wts, lw, bias_tab, l, ck, cv, page_table, state_mlstm_c[l],
                                    state_mlstm_n[l], state_mlstm_m[l], state_pool[l], state_conv[l])
        outs_p.append(rest_p)
        outs_s.append(rest_s)
    stack = lambda outs, idx: jnp.stack([o[idx] for o in outs])
    kp, vp, cp, np_, mp, pp, cvp = (stack(outs_p, i) for i in range(7))
    ks, vs, cs_, ns, ms, ps, cvs = (stack(outs_s, i) for i in range(7))
    return (xp.reshape(n_prompt, seq, d), xs.reshape(n_dec, 1, d),
            kp, vp, ks, vs, cp, np_, mp, cs_, ns, ms, pp, ps, cvp, cvs)
```

```python
import functools
import math

import jax
import jax.numpy as jnp
import numpy as np
from jax import lax
from jax.experimental import pallas as pl
from jax.experimental.pallas import tpu as pltpu

F32 = jnp.float32
BF16 = jnp.bfloat16

D_MODEL = 4096
PAGE_SIZE = 128
MLSTM_HEADS = 4
MLSTM_DV = 256
MLSTM_DQK = 128
MLSTM_CHUNK = 64
DIFF_HEADS = 8
DIFF_DV = 256
DIFF_DQK = 128
POOL_GROUPS = 4
POOL_WINDOWS = (2, 4, 8, 16)
POOL_GC = 256
POOL_CH = POOL_GROUPS * POOL_GC
POOL_BUF = 15
N_BRANCH = 3
D_FF = 11008
CONV_W = 3
REL_BUCKETS = 32
REL_MAX_DIST = 128
RMS_EPS = 1e-6

MLSTM_QK_W = MLSTM_HEADS * MLSTM_DQK
MLSTM_V_W = MLSTM_HEADS * MLSTM_DV
DIFF_QK_W = DIFF_HEADS * 2 * DIFF_DQK
DIFF_V_W = DIFF_HEADS * DIFF_DV

LANES = 128
SUBLANES = 8
VMEM_LIMIT = 56 * 1024 * 1024
NEG = -1e30
ROW_CHUNK = 512

BIAS_SPAN = 128


def _cparams(sem, flags=None):
    return pltpu.CompilerParams(dimension_semantics=sem, vmem_limit_bytes=VMEM_LIMIT, flags=flags)


def _rmsnorm_kernel(x_ref, g_ref, o_ref):
    x = x_ref[...]
    y = x * lax.rsqrt(jnp.mean(x * x, axis=-1, keepdims=True) + RMS_EPS)
    o_ref[...] = (y * g_ref[...]).astype(o_ref.dtype)


def _rmsnorm(x, g, tm):
    m, d = x.shape
    return pl.pallas_call(
        _rmsnorm_kernel,
        grid=(m // tm,),
        in_specs=[pl.BlockSpec((tm, d), lambda i: (i, 0)),
                  pl.BlockSpec((1, d), lambda i: (0, 0))],
        out_specs=pl.BlockSpec((tm, d), lambda i: (i, 0)),
        out_shape=jax.ShapeDtypeStruct((m, d), BF16),
        compiler_params=_cparams(("parallel",)),
        name="rmsnorm",
    )(x, g.reshape(1, d))


def _cast_kernel(x_ref, o_ref):
    o_ref[...] = x_ref[...].astype(o_ref.dtype)


def _cast_rows(x, dtype, tm):
    m, d = x.shape
    assert m % tm == 0
    return pl.pallas_call(
        _cast_kernel,
        grid=(m // tm,),
        in_specs=[pl.BlockSpec((tm, d), lambda i: (i, 0))],
        out_specs=pl.BlockSpec((tm, d), lambda i: (i, 0)),
        out_shape=jax.ShapeDtypeStruct((m, d), dtype),
        compiler_params=_cparams(("parallel",)),
        name="cast_rows",
    )(x)


def _sigmoid(x):
    return 0.5 * (jnp.tanh(0.5 * x) + 1.0)


def _group_rmsnorm(y, g, width):
    outs = []
    for c in range(0, y.shape[1], width):
        yc = y[:, c:c + width]
        yn = yc * lax.rsqrt(jnp.mean(yc * yc, axis=-1, keepdims=True) + RMS_EPS)
        outs.append(yn * g[:, c:c + width])
    return outs[0] if len(outs) == 1 else jnp.concatenate(outs, axis=1)


def _ep_bias(acc, b):
    return (acc + b,)


def _ep_bias_bf16copy(acc, b):
    y = acc + b
    return (y, y.astype(BF16))


def _ep_bias_sigmoid(acc, b):
    return (_sigmoid(acc + b),)


def _ep_knorm(acc, b, g):
    y = _group_rmsnorm(acc + b, g, DIFF_DQK)
    return (y, y.astype(BF16))


def _ep_qnorm_scaled(acc, b, g):
    y = _group_rmsnorm(acc + b, g, DIFF_DQK) * (DIFF_DQK ** -0.5)
    return (y, y.astype(BF16))


def _ep_residual(acc, r):
    return (r + acc,)


def _row_chunks(tm):
    step = ROW_CHUNK if tm % ROW_CHUNK == 0 else tm
    return [slice(r, r + step) for r in range(0, tm, step)]


def _mm_kernel(a_ref, w_ref, *rest, n_extra, n_out, epilogue, tile_extra):
    extras = rest[:n_extra]
    outs = rest[n_extra:n_extra + n_out]
    w = w_ref[...].astype(BF16)
    for rows in _row_chunks(a_ref.shape[0]):
        acc = jnp.dot(a_ref[rows, :].astype(BF16), w, preferred_element_type=F32)
        res = epilogue(acc, *[e[rows, :] if t else e[...] for e, t in zip(extras, tile_extra)])
        for o, r in zip(outs, res):
            o[rows, :] = r.astype(o.dtype)


def _matmul(a, w, extras, epilogue, out_dtypes, *, tm, tn, name, layer=None, col0=0, n=None):
    m, k = a.shape
    n = w.shape[-1] - col0 if n is None else n
    tm = min(tm, m)
    tn = min(tn, n)
    assert m % tm == 0 and n % tn == 0 and col0 % tn == 0, (m, n, tm, tn, col0)
    jb = col0 // tn
    if w.ndim == 2:
        w_spec = pl.BlockSpec((k, tn), lambda i, j: (layer or 0, jb + j))
    else:
        w_spec = pl.BlockSpec((None, k, tn), lambda i, j: (layer, 0, jb + j))
    in_specs = [pl.BlockSpec((tm, k), lambda i, j: (i, 0)), w_spec]
    tile_extra = []
    for e in extras:
        tile_extra.append(e.shape[0] != 1)
        if e.shape[0] == 1:
            in_specs.append(pl.BlockSpec((1, tn), lambda i, j: (0, j)))
        else:
            in_specs.append(pl.BlockSpec((tm, tn), lambda i, j: (i, j)))
    out_specs = [pl.BlockSpec((tm, tn), lambda i, j: (i, j)) for _ in out_dtypes]
    out_shape = [jax.ShapeDtypeStruct((m, n), dt) for dt in out_dtypes]
    return pl.pallas_call(
        functools.partial(_mm_kernel, n_extra=len(extras), n_out=len(out_dtypes), epilogue=epilogue,
                          tile_extra=tuple(tile_extra)),
        grid=(m // tm, n // tn),
        in_specs=in_specs,
        out_specs=out_specs,
        out_shape=out_shape,
        compiler_params=_cparams(("parallel", "arbitrary")),
        name=name,
    )(a, w, *extras)


def _merge_kernel(hm_ref, hd_ref, hp_ref, wm_ref, wd_ref, wp_ref, g0_ref, g1_ref, g2_ref, o_ref):
    wm, wd, wp = (w[...].astype(BF16) for w in (wm_ref, wd_ref, wp_ref))
    for rows in _row_chunks(o_ref.shape[0]):
        proj = lambda h_ref, w: jnp.dot(h_ref[rows, :].astype(BF16), w, preferred_element_type=F32)
        merged = (g0_ref[rows, :] * proj(hm_ref, wm)
                  + g1_ref[rows, :] * proj(hd_ref, wd)
                  + g2_ref[rows, :] * proj(hp_ref, wp))
        o_ref[rows, :] = merged.astype(o_ref.dtype)


def _merge(hm, hd, hp, wm, wd, wp, gates, *, layer, tm, tn):
    m = hm.shape[0]
    tm = min(tm, m)
    nj = D_MODEL // tn
    row = lambda width: pl.BlockSpec((tm, width), lambda i, j: (i, 0))
    col = lambda depth: pl.BlockSpec((depth, tn), lambda i, j: (layer, j))
    gate = lambda br: pl.BlockSpec((tm, tn), lambda i, j: (i, br * nj + j))
    return pl.pallas_call(
        _merge_kernel,
        grid=(m // tm, nj),
        in_specs=[row(MLSTM_V_W), row(DIFF_V_W), row(POOL_CH),
                  col(MLSTM_V_W), col(DIFF_V_W), col(POOL_CH),
                  gate(0), gate(1), gate(2)],
        out_specs=pl.BlockSpec((tm, tn), lambda i, j: (i, j)),
        out_shape=jax.ShapeDtypeStruct((m, D_MODEL), BF16),
        compiler_params=_cparams(("parallel", "arbitrary")),
        name="merge",
    )(hm, hd, hp, wm, wd, wp, gates, gates, gates)


def _conv_gate(ug, uv, pg, pv, cw_g, cw_v, cb_g, cb_v):
    cg = cb_g + (cw_g[0:1] * pg[0] + cw_g[1:2] * pg[1] + cw_g[2:3] * ug)
    cv = cb_v + (cw_v[0:1] * pv[0] + cw_v[1:2] * pv[1] + cw_v[2:3] * uv)
    half = 0.5 * cg
    return (half * (jnp.tanh(half) + 1.0)) * cv


def _ffn_up_prompt_kernel(a_ref, wg_ref, wv_ref, cwg_ref, cwv_ref, cbg_ref, cbv_ref,
                          act_ref, tail_ref, carry_ref, *, tiles_per_seq):
    i = pl.program_id(0)
    j = pl.program_id(1)
    tm = a_ref.shape[0]

    @pl.when(i % tiles_per_seq == 0)
    def _():
        carry_ref[:, pl.ds(j, 1)] = jnp.zeros((2, 1) + carry_ref.shape[2:], F32)

    wg = wg_ref[...].astype(BF16)
    wv = wv_ref[...].astype(BF16)
    tails = [carry_ref[0, j], carry_ref[1, j]]
    r8 = lax.broadcasted_iota(jnp.int32, (SUBLANES, wg.shape[1]), 0)

    def back(u, tail8):
        out = []
        for k in (2, 1):
            rolled = pltpu.roll(u, k, 0)
            head = rolled[:SUBLANES]
            for r in range(k):
                head = jnp.where(r8 == r, tail8[SUBLANES - k + r:SUBLANES - k + r + 1], head)
            out.append(jnp.concatenate([head, rolled[SUBLANES:]], axis=0))
        return tuple(out)

    for rows in _row_chunks(tm):
        a = a_ref[rows, :]
        ug = jnp.dot(a, wg, preferred_element_type=F32)
        uv = jnp.dot(a, wv, preferred_element_type=F32)
        act = _conv_gate(ug, uv, back(ug, tails[0]), back(uv, tails[1]),
                         cwg_ref[...], cwv_ref[...], cbg_ref[...], cbv_ref[...])
        act_ref[rows, :] = act.astype(act_ref.dtype)
        tails = [ug[ug.shape[0] - SUBLANES:], uv[uv.shape[0] - SUBLANES:]]
    for half in range(2):
        carry_ref[half, j] = tails[half]
        tail_ref[0, half] = tails[half]


def _ffn_up_prompt(h, w_up, conv_w, conv_b, *, layer, batch, tm, tn):
    m, d = h.shape
    nj = D_FF // tn
    tiles_per_seq = (m // batch) // tm
    conv_b = conv_b.reshape(1, 2 * D_FF)
    act, tail = pl.pallas_call(
        functools.partial(_ffn_up_prompt_kernel, tiles_per_seq=tiles_per_seq),
        grid=(m // tm, nj),
        in_specs=[pl.BlockSpec((tm, d), lambda i, j: (i, 0)),
                  pl.BlockSpec((None, d, tn), lambda i, j: (layer, 0, j)),
                  pl.BlockSpec((None, d, tn), lambda i, j: (layer, 0, nj + j)),
                  pl.BlockSpec((CONV_W, tn), lambda i, j: (0, j)),
                  pl.BlockSpec((CONV_W, tn), lambda i, j: (0, nj + j)),
                  pl.BlockSpec((1, tn), lambda i, j: (0, j)),
                  pl.BlockSpec((1, tn), lambda i, j: (0, nj + j))],
        out_specs=[pl.BlockSpec((tm, tn), lambda i, j: (i, j)),
                   pl.BlockSpec((1, 2, SUBLANES, tn), lambda i, j: (i, 0, 0, j))],
        out_shape=[jax.ShapeDtypeStruct((m, D_FF), BF16),
                   jax.ShapeDtypeStruct((m // tm, 2, SUBLANES, D_FF), F32)],
        scratch_shapes=[pltpu.VMEM((2, nj, SUBLANES, tn), F32)],
        compiler_params=_cparams(("arbitrary", "arbitrary")),
        name="ffn_up_prompt",
    )(h, w_up, w_up, conv_w, conv_w, conv_b, conv_b)
    tail = tail[tiles_per_seq - 1::tiles_per_seq]
    conv_new = jnp.transpose(tail[:, :, SUBLANES - (CONV_W - 1):, :], (0, 2, 1, 3))
    return act, conv_new.reshape(batch, CONV_W - 1, 2 * D_FF)


def _ffn_up_sample_kernel(a_ref, wg_ref, wv_ref, cwg_ref, cwv_ref, cbg_ref, cbv_ref,
                          p2g_ref, p1g_ref, p2v_ref, p1v_ref, act_ref, ug_ref, uv_ref):
    a = a_ref[...].astype(BF16)
    ug = jnp.dot(a, wg_ref[...].astype(BF16), preferred_element_type=F32)
    uv = jnp.dot(a, wv_ref[...].astype(BF16), preferred_element_type=F32)
    act = _conv_gate(ug, uv, (p2g_ref[...], p1g_ref[...]), (p2v_ref[...], p1v_ref[...]),
                     cwg_ref[...], cwv_ref[...], cbg_ref[...], cbv_ref[...])
    act_ref[...] = act.astype(act_ref.dtype)
    ug_ref[...] = ug
    uv_ref[...] = uv


def _ffn_up_sample(h, w_up, conv_w, conv_b, conv_buf, *, layer, tn):
    b, d = h.shape
    nj = D_FF // tn
    conv_b = conv_b.reshape(1, 2 * D_FF)
    prev2 = conv_buf[:, 0, :]
    prev1 = conv_buf[:, 1, :]
    lo = lambda rows: pl.BlockSpec((rows, tn), lambda j: (0, j))
    hi = lambda rows: pl.BlockSpec((rows, tn), lambda j: (0, nj + j))
    act, ug, uv = pl.pallas_call(
        _ffn_up_sample_kernel,
        grid=(nj,),
        in_specs=[pl.BlockSpec((b, d), lambda j: (0, 0)),
                  pl.BlockSpec((None, d, tn), lambda j: (layer, 0, j)),
                  pl.BlockSpec((None, d, tn), lambda j: (layer, 0, nj + j)),
                  lo(CONV_W), hi(CONV_W), lo(1), hi(1),
                  lo(b), lo(b), hi(b), hi(b)],
        out_specs=[lo(b), lo(b), lo(b)],
        out_shape=[jax.ShapeDtypeStruct((b, D_FF), BF16),
                   jax.ShapeDtypeStruct((b, D_FF), F32),
                   jax.ShapeDtypeStruct((b, D_FF), F32)],
        compiler_params=_cparams(("arbitrary",)),
        name="ffn_up_sample",
    )(h, w_up, w_up, conv_w, conv_w, conv_b, conv_b, prev2, prev1, prev2, prev1)
    u = jnp.concatenate([ug, uv], axis=1)
    return act, jnp.concatenate([conv_buf[:, 1:], u[:, None, :]], axis=1)


def _log_sigmoid(x):
    return -(jnp.maximum(-x, 0.0) + jnp.log1p(jnp.exp(-jnp.abs(x))))


def _cumsum_rows(x):
    n = x.shape[0]
    row = lax.broadcasted_iota(jnp.int32, x.shape, 0)
    d = 1
    while d < n:
        x = x + jnp.where(row >= d, pltpu.roll(x, d, 0), 0.0)
        d *= 2
    return x


def _head_out_norm(h, g, o_gate):
    y = h * lax.rsqrt(jnp.mean(h * h, axis=-1, keepdims=True) + RMS_EPS)
    return (y * g) * _sigmoid(o_gate)


def _mlstm_prompt_kernel(qkv_ref, mif_ref, mo_ref, nw_ref, hm_ref, c_out, n_out, m_out,
                         c_sc, n_sc, m_sc, *, cs, nc):
    ci = pl.program_id(0)

    @pl.when(ci == 0)
    def _():
        c_sc[...] = jnp.zeros(c_sc.shape, F32)
        n_sc[...] = jnp.zeros(n_sc.shape, F32)
        m_sc[...] = jnp.zeros(m_sc.shape, F32)

    row = lax.broadcasted_iota(jnp.int32, (cs, cs), 0)
    col = lax.broadcasted_iota(jnp.int32, (cs, cs), 1)
    causal = col <= row
    eye = col == row
    lane = lax.broadcasted_iota(jnp.int32, (1, LANES), 1)
    for bi in range(qkv_ref.shape[0]):
        m_vec = _mlstm_chunk(bi, qkv_ref, mif_ref, mo_ref, nw_ref, hm_ref, c_sc, n_sc, m_sc,
                             causal, eye, lane, cs)

        @pl.when(ci == nc - 1)
        def _():
            m_out[bi] = m_vec

    @pl.when(ci == nc - 1)
    def _():
        c_out[...] = c_sc[...]
        n_out[...] = n_sc[...][:, :, 0, :]


def _mlstm_chunk(bi, qkv_ref, mif_ref, mo_ref, nw_ref, hm_ref, c_sc, n_sc, m_sc, causal, eye, lane, cs):
    mif = mif_ref[bi]
    b_all = _cumsum_rows(_log_sigmoid(mif))
    m_vec = jnp.zeros((1, LANES), F32)
    for h in range(MLSTM_HEADS):
        q = qkv_ref[bi, :, h * MLSTM_DQK:(h + 1) * MLSTM_DQK]
        k = qkv_ref[bi, :, MLSTM_QK_W + h * MLSTM_DQK:MLSTM_QK_W + (h + 1) * MLSTM_DQK] * (MLSTM_DQK ** -0.5)
        v = qkv_ref[bi, :, 2 * MLSTM_QK_W + h * MLSTM_DV:2 * MLSTM_QK_W + (h + 1) * MLSTM_DV]
        li = mif[:, h:h + 1]
        b = b_all[:, MLSTM_HEADS + h:MLSTM_HEADS + h + 1]
        g_row = jnp.sum(jnp.where(eye, li - b, 0.0), axis=0, keepdims=True)
        dmat = jnp.where(causal, b + g_row, NEG)
        m_prev = m_sc[bi, h]
        inter = b + m_prev
        m_t = jnp.maximum(inter, jnp.max(dmat, axis=-1, keepdims=True))
        w_inter = jnp.exp(inter - m_t)
        qb, kb, vb = q.astype(BF16), k.astype(BF16), v.astype(BF16)
        s = lax.dot_general(qb, kb, (((1,), (1,)), ((), ())), preferred_element_type=F32)
        sc = s * jnp.exp(dmat - m_t)
        c_prev = c_sc[bi, h]
        n_prev = n_sc[bi, h]
        num = (w_inter * jnp.dot(qb, c_prev.astype(BF16), preferred_element_type=F32)
               + jnp.dot(sc.astype(BF16), vb, preferred_element_type=F32))
        den = w_inter * jnp.sum(q * n_prev, axis=-1, keepdims=True) + jnp.sum(sc, axis=-1, keepdims=True)
        hh = num / jnp.maximum(jnp.abs(den), jnp.exp(-m_t))
        m_new = m_t[cs - 1:cs]
        b_last = b[cs - 1:cs]
        g_state = jnp.exp(b_last + m_prev - m_new)
        g_tok = jnp.exp(b_last - b + li - m_new)
        kg = k * g_tok
        c_new = g_state * c_prev + lax.dot_general(kg.astype(BF16), vb, (((0,), (0,)), ((), ())),
                                                   preferred_element_type=F32)
        n_new = g_state * n_prev + jnp.sum(kg, axis=0, keepdims=True)
        c_sc[bi, h] = c_new
        n_sc[bi, h] = n_new
        m_sc[bi, h] = m_new
        m_vec = m_vec + jnp.where(lane == h, m_new, 0.0)
        sl = slice(h * MLSTM_DV, (h + 1) * MLSTM_DV)
        hm_ref[bi, :, sl] = _head_out_norm(hh, nw_ref[:, sl], mo_ref[bi, :, sl]).astype(hm_ref.dtype)
    return m_vec


def _mlstm_prompt(mqkv, mif, mo, norm_w, *, batch):
    m = mqkv.shape[0]
    seq = m // batch
    cs = MLSTM_CHUNK if seq % MLSTM_CHUNK == 0 else seq
    nc = seq // cs
    per_seq = lambda a: a.reshape(batch, seq, a.shape[-1])
    rows = lambda width: pl.BlockSpec((batch, cs, width), lambda c: (0, c, 0))
    whole = lambda shape: pl.BlockSpec(shape, lambda c: (0,) * len(shape))
    state_shapes = [(batch, MLSTM_HEADS, MLSTM_DQK, MLSTM_DV), (batch, MLSTM_HEADS, MLSTM_DQK),
                    (batch, 1, LANES)]
    hm, c, n, mm = pl.pallas_call(
        functools.partial(_mlstm_prompt_kernel, cs=cs, nc=nc),
        grid=(nc,),
        in_specs=[rows(mqkv.shape[1]), rows(LANES), rows(MLSTM_V_W), whole((1, MLSTM_V_W))],
        out_specs=[rows(MLSTM_V_W)] + [whole(s) for s in state_shapes],
        out_shape=[jax.ShapeDtypeStruct((batch, seq, MLSTM_V_W), BF16)]
        + [jax.ShapeDtypeStruct(s, F32) for s in state_shapes],
        scratch_shapes=[pltpu.VMEM((batch, MLSTM_HEADS, MLSTM_DQK, MLSTM_DV), F32),
                        pltpu.VMEM((batch, MLSTM_HEADS, 1, MLSTM_DQK), F32),
                        pltpu.VMEM((batch, MLSTM_HEADS, 1, 1), F32)],
        compiler_params=_cparams(("arbitrary",)),
        name="mlstm_prompt",
    )(per_seq(mqkv), per_seq(mif), per_seq(mo), norm_w.reshape(1, MLSTM_V_W))
    return hm.reshape(m, MLSTM_V_W), c, n, mm[:, 0, :MLSTM_HEADS]


def _lanes_to_rows(x_row, eye):
    return jnp.sum(jnp.where(eye, x_row, 0.0), axis=1, keepdims=True)


def _mlstm_sample_kernel(qkv_ref, mif_ref, mo_ref, nw_ref, c_ref, n_ref, m_ref,
                         hm_ref, c_out, n_out, m_out):
    mif = mif_ref[0]
    lf_all = _log_sigmoid(mif)
    m_all = m_ref[0]
    row = lax.broadcasted_iota(jnp.int32, (MLSTM_DQK, MLSTM_DQK), 0)
    col = lax.broadcasted_iota(jnp.int32, (MLSTM_DQK, MLSTM_DQK), 1)
    eye = row == col
    lane = lax.broadcasted_iota(jnp.int32, (1, LANES), 1)
    m_vec = jnp.zeros((1, LANES), F32)
    for h in range(MLSTM_HEADS):
        q = qkv_ref[0, :, h * MLSTM_DQK:(h + 1) * MLSTM_DQK]
        k = qkv_ref[0, :, MLSTM_QK_W + h * MLSTM_DQK:MLSTM_QK_W + (h + 1) * MLSTM_DQK] * (MLSTM_DQK ** -0.5)
        v = qkv_ref[0, :, 2 * MLSTM_QK_W + h * MLSTM_DV:2 * MLSTM_QK_W + (h + 1) * MLSTM_DV]
        li = mif[:, h:h + 1]
        lf = lf_all[:, MLSTM_HEADS + h:MLSTM_HEADS + h + 1]
        m_prev = m_all[:, h:h + 1]
        inter = lf + m_prev
        m_t = jnp.maximum(inter, li)
        w_inter = jnp.exp(inter - m_t)
        sc = jnp.sum(q * k, axis=-1, keepdims=True) * jnp.exp(li - m_t)
        c_prev = c_ref[0, h]
        n_prev = n_ref[0, h:h + 1, :]
        q_col = _lanes_to_rows(q, eye)
        k_col = _lanes_to_rows(k, eye)
        num = w_inter * jnp.sum(q_col * c_prev, axis=0, keepdims=True) + sc * v
        den = w_inter * jnp.sum(q * n_prev, axis=-1, keepdims=True) + sc
        hh = num / jnp.maximum(jnp.abs(den), jnp.exp(-m_t))
        g_tok = jnp.exp(li - m_t)
        c_out[0, h] = w_inter * c_prev + (g_tok * k_col) * v
        n_out[0, h:h + 1, :] = w_inter * n_prev + g_tok * k
        m_vec = m_vec + jnp.where(lane == h, m_t, 0.0)
        sl = slice(h * MLSTM_DV, (h + 1) * MLSTM_DV)
        hm_ref[0, :, sl] = _head_out_norm(hh, nw_ref[:, sl], mo_ref[0, :, sl]).astype(hm_ref.dtype)
    m_out[0] = m_vec


def _mlstm_sample(mqkv, mif, mo, norm_w, c0, n0, m0):
    b = mqkv.shape[0]
    m0p = jnp.pad(m0, ((0, 0), (0, LANES - MLSTM_HEADS))).reshape(b, 1, LANES)
    per_seq = lambda width: pl.BlockSpec((1, 1, width), lambda i: (i, 0, 0))
    c_spec = pl.BlockSpec((1, MLSTM_HEADS, MLSTM_DQK, MLSTM_DV), lambda i: (i, 0, 0, 0))
    n_spec = pl.BlockSpec((1, MLSTM_HEADS, MLSTM_DQK), lambda i: (i, 0, 0))
    hm, c, n, mm = pl.pallas_call(
        _mlstm_sample_kernel,
        grid=(b,),
        in_specs=[per_seq(mqkv.shape[1]), per_seq(LANES), per_seq(MLSTM_V_W),
                  pl.BlockSpec((1, MLSTM_V_W), lambda i: (0, 0)),
                  c_spec, n_spec, per_seq(LANES)],
        out_specs=[per_seq(MLSTM_V_W), c_spec, n_spec, per_seq(LANES)],
        out_shape=[jax.ShapeDtypeStruct((b, 1, MLSTM_V_W), F32),
                   jax.ShapeDtypeStruct(c0.shape, F32),
                   jax.ShapeDtypeStruct(n0.shape, F32),
                   jax.ShapeDtypeStruct((b, 1, LANES), F32)],
        compiler_params=_cparams(("parallel",)),
        name="mlstm_sample",
    )(mqkv.reshape(b, 1, -1), mif.reshape(b, 1, LANES), mo.reshape(b, 1, -1),
      norm_w.reshape(1, MLSTM_V_W), c0, n0, m0p)
    return hm.reshape(b, MLSTM_V_W), c, n, mm[:, 0, :MLSTM_HEADS]


def _bucket_by_distance():
    max_exact = REL_BUCKETS // 2
    n = np.arange(BIAS_SPAN + 1)
    large = max_exact + np.floor(np.log(np.maximum(n, 1) / max_exact) / math.log(REL_MAX_DIST / max_exact)
                                 * (REL_BUCKETS - max_exact)).astype(np.int64)
    return np.where(n < max_exact, n, np.minimum(large, REL_BUCKETS - 1))


def _bucket_starts():
    bucket = _bucket_by_distance()
    return [int(np.argmax(bucket >= b)) for b in range(1, REL_BUCKETS)]


def _bias_by_distance(rel_bias):
    tab = jnp.transpose(rel_bias.astype(F32)[_bucket_by_distance()], (1, 0))
    return tab - tab[:, BIAS_SPAN:]


def _lambda(lq_ref, lk_ref, lam_init):
    lq = lq_ref[...]
    lk = lk_ref[...]
    e0 = jnp.exp(jnp.sum(lq[0:1] * lk[0:1], axis=-1, keepdims=True))
    e1 = jnp.exp(jnp.sum(lq[1:2] * lk[1:2], axis=-1, keepdims=True))
    return e0 - e1 + lam_init


def _diff_out_norm(d, sub, lam_init):
    y = d * lax.rsqrt(jnp.mean(d * d, axis=-1, keepdims=True) + RMS_EPS)
    return (y * sub) * (1.0 - lam_init)


def _bias_tiles_kernel(rb_ref, o_ref, *, tq):
    h = pl.program_id(0)
    blk = pl.program_id(1)
    row = lax.broadcasted_iota(jnp.int32, (tq, tq), 0)
    col = lax.broadcasted_iota(jnp.int32, (tq, tq), 1)
    dist = row - col + blk * tq
    far = rb_ref[h, REL_BUCKETS - 1]
    val = jnp.full((tq, tq), rb_ref[h, 0] - far, F32)
    for b, start in enumerate(_bucket_starts(), start=1):
        val = jnp.where(dist >= start, rb_ref[h, b] - far, val)
    o_ref[0, 0] = jnp.where(dist >= 0, val, NEG)


def _bias_tiles(rel_bias, tq):
    assert tq + 1 >= _bucket_starts()[-1]
    return pl.pallas_call(
        functools.partial(_bias_tiles_kernel, tq=tq),
        grid=(DIFF_HEADS, 2),
        in_specs=[pl.BlockSpec(memory_space=pltpu.SMEM)],
        out_specs=pl.BlockSpec((1, 1, tq, tq), lambda h, blk: (h, blk, 0, 0)),
        out_shape=jax.ShapeDtypeStruct((DIFF_HEADS, 2, tq, tq), F32),
        compiler_params=_cparams(("parallel", "parallel")),
        name="bias_tiles",
    )(jnp.transpose(rel_bias.astype(F32)))


def _attn_prompt_kernel(qi_ref, ki_ref, q_ref, k_ref, v_ref, bias_ref, lq_ref, lk_ref, sub_ref, o_ref,
                        m_sc, l_sc, acc_sc, *, lam_init):
    qi = qi_ref[pl.program_id(2)]
    ki = ki_ref[pl.program_id(2)]

    @pl.when(ki == 0)
    def _():
        m_sc[...] = jnp.full(m_sc.shape, NEG, F32)
        l_sc[...] = jnp.zeros(l_sc.shape, F32)
        acc_sc[...] = jnp.zeros(acc_sc.shape, F32)

    def tile(bias):
        v = v_ref[0]
        tq = q_ref.shape[1]
        half = tq // 2 if tq % (2 * SUBLANES * 2) == 0 else tq
        for r0 in range(0, tq, half):
            rows = slice(r0, r0 + half)
            for mp in range(2):
                sl = slice(mp * DIFF_DQK, (mp + 1) * DIFF_DQK)
                s = lax.dot_general(q_ref[0, rows, sl], k_ref[0, :, sl], (((1,), (1,)), ((), ())),
                                    preferred_element_type=F32)
                if bias is not None:
                    s = s + bias[rows]
                m_prev = m_sc[mp, rows]
                m_new = jnp.maximum(m_prev, jnp.max(s, axis=-1, keepdims=True))
                alpha = jnp.exp(m_prev - m_new)
                p = jnp.exp(s - m_new)
                l_sc[mp, rows] = alpha * l_sc[mp, rows] + jnp.sum(p, axis=-1, keepdims=True)
                acc_sc[mp, rows] = alpha * acc_sc[mp, rows] + jnp.dot(p.astype(BF16), v,
                                                                      preferred_element_type=F32)
                m_sc[mp, rows] = m_new

    @pl.when(ki < qi - 1)
    def _():
        tile(None)

    @pl.when((ki >= qi - 1) & (ki <= qi))
    def _():
        tile(bias_ref[0, 0])

    @pl.when(ki == qi)
    def _():
        lam = _lambda(lq_ref, lk_ref, lam_init)
        d = acc_sc[0] / l_sc[0] - lam * (acc_sc[1] / l_sc[1])
        o_ref[0] = _diff_out_norm(d, sub_ref[...], lam_init).astype(o_ref.dtype)


def _attn_prompt(q, k, v, bias, lam_q, lam_k, subln, *, batch, lam_init):
    m = q.shape[0]
    seq = m // batch
    tq = bias.shape[-1]
    assert seq % tq == 0
    nq = seq // tq
    hw = 2 * DIFF_DQK
    q3, k3, v3 = (a.reshape(batch, seq, -1) for a in (q, k, v))
    pairs = [(qi, ki) for qi in range(nq) for ki in range(qi + 1)]
    qi_tab = jnp.asarray([p[0] for p in pairs], jnp.int32)
    ki_tab = jnp.asarray([p[1] for p in pairs], jnp.int32)
    q_spec = pl.BlockSpec((1, tq, hw), lambda b, h, t, qt, kt: (b, qt[t], h))
    kv_spec = pl.BlockSpec((1, tq, hw), lambda b, h, t, qt, kt: (b, kt[t], h))
    const = lambda shape: pl.BlockSpec(shape, lambda b, h, t, qt, kt: (0,) * len(shape))
    out = pl.pallas_call(
        functools.partial(_attn_prompt_kernel, lam_init=lam_init),
        grid_spec=pltpu.PrefetchScalarGridSpec(
            num_scalar_prefetch=2,
            grid=(batch, DIFF_HEADS, len(pairs)),
            in_specs=[q_spec, kv_spec, kv_spec,
                      pl.BlockSpec((1, 1, tq, tq),
                                   lambda b, h, t, qt, kt: (h, jnp.minimum(qt[t] - kt[t], 1), 0, 0)),
                      const((2, DIFF_DQK)), const((2, DIFF_DQK)), const((1, DIFF_DV))],
            out_specs=pl.BlockSpec((1, tq, DIFF_DV), lambda b, h, t, qt, kt: (b, qt[t], h)),
            scratch_shapes=[pltpu.VMEM((2, tq, 1), F32), pltpu.VMEM((2, tq, 1), F32),
                            pltpu.VMEM((2, tq, DIFF_DV), F32)]),
        out_shape=jax.ShapeDtypeStruct((batch, seq, DIFF_V_W), BF16),
        compiler_params=_cparams(("parallel", "parallel", "arbitrary")),
        name="attn_prompt",
    )(qi_tab, ki_tab, q3, k3, v3, bias, lam_q, lam_k, subln.reshape(1, DIFF_DV))
    return out.reshape(m, DIFF_V_W)


N_MAPS = 2 * DIFF_HEADS
PAGE_ROWS = PAGE_SIZE * DIFF_HEADS
NEW_TOKENS = 16
DECODE_PAGES_PER_STEP = 8


def _attn_sample_kernel(pt_ref, qm_ref, kn_ref, vn_ref, *rest, n_steps, pages, lam_init):
    kc_refs, vc_refs = rest[:pages], rest[pages:2 * pages]
    (bfar_ref, blast_ref, bnew_ref, lq_ref, lk_ref, sub_ref, o_ref,
     qx_sc, m_sc, l_sc, acc_sc) = rest[2 * pages:]
    p = pl.program_id(1)
    sub_i = lax.broadcasted_iota(jnp.int32, (DIFF_HEADS, LANES), 0)
    lane_i = lax.broadcasted_iota(jnp.int32, (DIFF_HEADS, LANES), 1)
    own = (lane_i == 2 * sub_i) | (lane_i == 2 * sub_i + 1)

    @pl.when(p == 0)
    def _():
        qm = qm_ref[0]
        r = lax.broadcasted_iota(jnp.int32, qm.shape, 0)
        first = (r & 1) == 0
        qx = jnp.concatenate([jnp.where(first, qm, 0.0), jnp.where(first, 0.0, qm)], axis=1)
        qx_sc[...] = jnp.zeros(qx_sc.shape, BF16)
        qx_sc[0:N_MAPS, :] = qx.astype(BF16)
        m_sc[...] = jnp.where(own, NEG, 0.0)
        l_sc[...] = jnp.zeros(l_sc.shape, F32)
        acc_sc[...] = jnp.zeros(acc_sc.shape, F32)

    row = lax.broadcasted_iota(jnp.int32, (LANES, LANES), 0)
    col = lax.broadcasted_iota(jnp.int32, (LANES, LANES), 1)
    eye = row == col

    def per_lane_column(x):
        x_row = jnp.sum(jnp.where(own, x, 0.0), axis=0, keepdims=True)
        return jnp.sum(jnp.where(eye, x_row, 0.0), axis=1, keepdims=True)

    def group_update(blocks):
        scores = []
        m_prev = m_sc[...]
        m_new = m_prev
        for kf, _, bias3 in blocks:
            s_all = lax.dot_general(kf.astype(BF16), qx_sc[...], (((1,), (1,)), ((), ())),
                                    preferred_element_type=F32)
            s3 = s_all.reshape(kf.shape[0] // DIFF_HEADS, DIFF_HEADS, LANES) + bias3
            m_new = jnp.maximum(m_new, jnp.max(s3, axis=0))
            scores.append(s3)
        alpha = jnp.exp(m_prev - m_new)
        l_new = alpha * l_sc[...]
        pv = None
        for s3, (_, vf, _) in zip(scores, blocks):
            p3 = jnp.exp(s3 - m_new[None])
            l_new = l_new + jnp.sum(p3, axis=0)
            p_t = p3.reshape(vf.shape[0], LANES).T.astype(BF16)
            part = jnp.dot(p_t, vf.astype(BF16), preferred_element_type=F32)
            pv = part if pv is None else pv + part
        l_sc[...] = l_new
        m_sc[...] = m_new
        acc_sc[...] = per_lane_column(alpha) * acc_sc[...] + pv

    far = bfar_ref[...][None]

    @pl.when(p < n_steps - 1)
    def _():
        group_update([(kc[...], vc[...], far) for kc, vc in zip(kc_refs, vc_refs)])

    @pl.when(p == n_steps - 1)
    def _():
        biases = [far] * (pages - 1) + [blast_ref[...].reshape(PAGE_SIZE, DIFF_HEADS, LANES)]
        blocks = [(kc[...], vc[...], b3) for kc, vc, b3 in zip(kc_refs, vc_refs, biases)]
        blocks.append((kn_ref[0], vn_ref[0], bnew_ref[...].reshape(NEW_TOKENS, DIFF_HEADS, LANES)))
        group_update(blocks)
        lam = _lambda(lq_ref, lk_ref, lam_init)
        l_col = per_lane_column(l_sc[...])
        out = acc_sc[0:N_MAPS, :] / l_col[0:N_MAPS]
        for h in range(DIFF_HEADS):
            d = out[2 * h:2 * h + 1] - lam * out[2 * h + 1:2 * h + 2]
            sl = slice(h * DIFF_DV, (h + 1) * DIFF_DV)
            o_ref[0, :, sl] = _diff_out_norm(d, sub_ref[...], lam_init).astype(o_ref.dtype)


def _attn_sample(q, k_new, v_new, cache_k, cache_v, layer, page_table, bias_tab, lam_q, lam_k, subln,
                 *, lam_init):
    b, n_pages = page_table.shape
    assert PAGE_SIZE >= BIAS_SPAN
    pages = DECODE_PAGES_PER_STEP if n_pages % DECODE_PAGES_PER_STEP == 0 else 1
    n_steps = n_pages // pages
    hw = 2 * DIFF_DQK
    sub_i = lax.broadcasted_iota(jnp.int32, (DIFF_HEADS, LANES), 0)
    lane_i = lax.broadcasted_iota(jnp.int32, (DIFF_HEADS, LANES), 1)
    own = (lane_i == 2 * sub_i) | (lane_i == 2 * sub_i + 1)

    def rows_bias(per_token):
        t = per_token.shape[0]
        return jnp.where(own[None], per_token[:, :, None], NEG).reshape(t * DIFF_HEADS, LANES)

    b_far = jnp.where(own, 0.0, NEG)
    b_last = rows_bias(jnp.transpose(bias_tab[:, :0:-1][:, :PAGE_SIZE]))
    b_new = rows_bias(jnp.full((NEW_TOKENS, DIFF_HEADS), NEG, F32).at[0].set(bias_tab[:, 0]))
    pad_page = lambda a: jnp.pad(a.reshape(b, DIFF_HEADS, hw), ((0, 0), (0, (NEW_TOKENS - 1) * DIFF_HEADS), (0, 0)))
    new_spec = pl.BlockSpec((1, NEW_TOKENS * DIFF_HEADS, hw), lambda i, p, pt: (i, 0, 0))
    page = lambda r: pl.BlockSpec((None, None, PAGE_ROWS, hw),
                                  lambda i, p, pt: (layer, pt[i * n_pages + p * pages + r], 0, 0))
    const = lambda shape: pl.BlockSpec(shape, lambda i, p, pt: (0,) * len(shape))
    out = pl.pallas_call(
        functools.partial(_attn_sample_kernel, n_steps=n_steps, pages=pages, lam_init=lam_init),
        grid_spec=pltpu.PrefetchScalarGridSpec(
            num_scalar_prefetch=1,
            grid=(b, n_steps),
            in_specs=[pl.BlockSpec((1, N_MAPS, DIFF_DQK), lambda i, p, pt: (i, 0, 0)),
                      new_spec, new_spec, *[page(r) for r in range(pages)], *[page(r) for r in range(pages)],
                      const((DIFF_HEADS, LANES)), const((PAGE_ROWS, LANES)),
                      const((NEW_TOKENS * DIFF_HEADS, LANES)),
                      const((2, DIFF_DQK)), const((2, DIFF_DQK)), const((1, DIFF_DV))],
            out_specs=pl.BlockSpec((1, 1, DIFF_V_W), lambda i, p, pt: (i, 0, 0)),
            scratch_shapes=[pltpu.VMEM((LANES, hw), BF16),
                            pltpu.VMEM((DIFF_HEADS, LANES), F32), pltpu.VMEM((DIFF_HEADS, LANES), F32),
                            pltpu.VMEM((LANES, DIFF_DV), F32)]),
        out_shape=jax.ShapeDtypeStruct((b, 1, DIFF_V_W), F32),
        compiler_params=_cparams(("parallel", "arbitrary")),
        name="attn_sample",
    )(page_table.reshape(-1), q.reshape(b, N_MAPS, DIFF_DQK), pad_page(k_new), pad_page(v_new),
      *[cache_k] * pages, *[cache_v] * pages, b_far, b_last, b_new, lam_q, lam_k, subln.reshape(1, DIFF_DV))
    return out.reshape(b, DIFF_V_W)


HIST = 16


def _window_sum(ext, w, n):
    s = ext
    d = 1
    while d < w:
        s = s + pltpu.roll(s, d, 0)
        d *= 2
    return s[HIST:HIST + n]


def _pool_prompt_kernel(u_ref, w_ref, sc_ref, o_ref, hist_out, hist_sc, *, tm):
    t = pl.program_id(1)

    @pl.when(t == 0)
    def _():
        hist_sc[...] = jnp.zeros(hist_sc.shape, F32)

    pos = t * tm + lax.broadcasted_iota(jnp.int32, (tm, 1), 0)
    for gi, w in enumerate(POOL_WINDOWS):
        sl = slice(gi * POOL_GC, (gi + 1) * POOL_GC)
        u = u_ref[:, sl]
        ext = jnp.concatenate([hist_sc[:, sl], u], axis=0)
        cnt = jnp.minimum(w, pos + 1).astype(F32)
        pooled = _window_sum(ext, w, tm) / cnt - u
        mixed = jnp.dot(pooled.astype(BF16), w_ref[gi], preferred_element_type=F32)
        o_ref[:, sl] = (mixed * sc_ref[:, sl]).astype(o_ref.dtype)
        new_hist = u[tm - HIST:]
        hist_sc[:, sl] = new_hist
        hist_out[0, :, sl] = new_hist


def _pool_prompt(u, pool_w, pool_scale, *, batch, tm):
    m = u.shape[0]
    seq = m // batch
    tm = min(tm, seq)
    assert tm >= HIST
    tps = seq // tm
    hp, hist = pl.pallas_call(
        functools.partial(_pool_prompt_kernel, tm=tm),
        grid=(batch, tps),
        in_specs=[pl.BlockSpec((tm, POOL_CH), lambda b, t: (b * tps + t, 0)),
                  pl.BlockSpec((POOL_GROUPS, POOL_GC, POOL_GC), lambda b, t: (0, 0, 0)),
                  pl.BlockSpec((1, POOL_CH), lambda b, t: (0, 0))],
        out_specs=[pl.BlockSpec((tm, POOL_CH), lambda b, t: (b * tps + t, 0)),
                   pl.BlockSpec((1, HIST, POOL_CH), lambda b, t: (b, 0, 0))],
        out_shape=[jax.ShapeDtypeStruct((m, POOL_CH), BF16),
                   jax.ShapeDtypeStruct((batch, HIST, POOL_CH), F32)],
        scratch_shapes=[pltpu.VMEM((HIST, POOL_CH), F32)],
        compiler_params=_cparams(("parallel", "arbitrary")),
        name="pool_prompt",
    )(u, pool_w, pool_scale.reshape(1, POOL_CH))
    return hp, hist[:, HIST - POOL_BUF:]


def _pool_sample_kernel(ucat_ref, w_ref, sc_ref, o_ref):
    for gi, w in enumerate(POOL_WINDOWS):
        sl = slice(gi * POOL_GC, (gi + 1) * POOL_GC)
        new = ucat_ref[HIST - 1, :, sl]
        win = new
        for r in range(HIST - w, HIST - 1):
            win = win + ucat_ref[r, :, sl]
        pooled = win / float(w) - new
        mixed = jnp.dot(pooled.astype(BF16), w_ref[gi], preferred_element_type=F32)
        o_ref[:, sl] = (mixed * sc_ref[:, sl]).astype(o_ref.dtype)


def _pool_sample(u, buf, pool_w, pool_scale):
    b = u.shape[0]
    assert buf.shape[1] == HIST - 1
    ucat = jnp.concatenate([buf, u[:, None, :]], axis=1)
    hp = pl.pallas_call(
        _pool_sample_kernel,
        out_shape=jax.ShapeDtypeStruct((b, POOL_CH), F32),
        compiler_params=pltpu.CompilerParams(vmem_limit_bytes=VMEM_LIMIT),
        name="pool_sample",
    )(jnp.transpose(ucat, (1, 0, 2)), pool_w, pool_scale.reshape(1, POOL_CH))
    return hp, ucat[:, 1:]


IN_SPLITS = (MLSTM_QK_W, MLSTM_QK_W, MLSTM_V_W, 2 * MLSTM_HEADS, MLSTM_V_W,
             DIFF_QK_W, DIFF_QK_W, DIFF_V_W, POOL_CH, N_BRANCH * D_MODEL)
IN_OFFS = tuple(int(o) for o in np.cumsum((0,) + IN_SPLITS))
GATE_COLS = (IN_OFFS[3], IN_OFFS[4])


def _prep_weights(p):
    w_in = p["w_in"]
    g0, g1 = GATE_COLS
    pad = ((0, 0), (0, 0), (0, LANES - (g1 - g0)))
    rows = lambda w: w.reshape(-1, w.shape[-1])
    return dict(w_head=rows(w_in[:, :, :g0]), w_tail=rows(w_in[:, :, g1:].astype(BF16)),
                w_mif=rows(jnp.pad(w_in[:, :, g0:g1], pad)),
                w_down=_cast_rows(rows(p["w_down"]), BF16, tm=D_FF // 16),
                **{k: _cast_rows(rows(p[k]), BF16, tm=256) for k in ("w_br_mlstm", "w_br_diff", "w_br_pool")})


def _prep_layer(p, l):
    b_in = p["b_in"][l]
    g0, g1 = GATE_COLS
    pad = ((0, 0), (0, LANES - (g1 - g0)))
    return dict(
        b_tail=b_in[g1:].reshape(1, -1), b_mif=jnp.pad(b_in[g0:g1].reshape(1, -1), pad),
        b_head=b_in[:g0].reshape(1, -1),
        q_norm=jnp.tile(p["q_norm"][l].reshape(1, -1), (1, DIFF_HEADS)),
        k_norm=jnp.tile(p["k_norm"][l].reshape(1, -1), (1, DIFF_HEADS)),
        norm_mix=p["norm_mix"][l], norm_ffn=p["norm_ffn"][l], mlstm_norm=p["mlstm_norm"][l],
        lambda_q=p["lambda_q"][l], lambda_k=p["lambda_k"][l], diff_subln=p["diff_subln"][l],
        pool_w=p["pool_w"][l].astype(BF16), pool_scale=p["pool_scale"][l],
        conv_w=p["conv_w"][l], conv_b=p["conv_b"][l],
    )


def _projections(h, wts, lw, l, *, tm, tn):
    g1 = GATE_COLS[1]

    def tail(seg, epilogue, extras, out_dtypes, name):
        c0, c1 = IN_OFFS[seg] - g1, IN_OFFS[seg + 1] - g1
        return _matmul(h, wts["w_tail"], (lw["b_tail"][:, c0:c1],) + extras, epilogue, out_dtypes,
                       layer=l, tm=tm, tn=tn, col0=c0, n=c1 - c0, name=name)

    out = {}
    (out["mqkv"],) = _matmul(h, wts["w_head"], (lw["b_head"],), _ep_bias, (F32,), layer=l, tm=tm, tn=tn,
                             name="proj_mqkv")
    (out["mif"],) = _matmul(h, wts["w_mif"], (lw["b_mif"],), _ep_bias, (F32,), layer=l, tm=tm, tn=tn,
                            name="proj_mif")
    (out["mo"],) = tail(4, _ep_bias, (), (F32,), "proj_mo")
    out["dq"], out["dq16"] = tail(5, _ep_qnorm_scaled, (lw["q_norm"],), (F32, BF16), "proj_dq")
    out["dk"], out["dk16"] = tail(6, _ep_knorm, (lw["k_norm"],), (F32, BF16), "proj_dk")
    out["dv"], out["dv16"] = tail(7, _ep_bias_bf16copy, (), (F32, BF16), "proj_dv")
    (out["pu"],) = tail(8, _ep_bias, (), (F32,), "proj_pu")
    (out["gates"],) = tail(9, _ep_bias_sigmoid, (), (F32,), "proj_gates")
    return out


ATTN_TQ = 512
TILES_PROMPT = dict(norm=512, proj=(1024, 512), merge=(1024, 256), out=(1024, 512),
                    ffn_up=(1024, 256), ffn_down=(512, 512), pool=512)
TILES_SAMPLE = dict(proj=512, merge=256, out=512, ffn_up=256, ffn_down=256)


def _layer_prompt(x, p, wts, lw, bias_tiles, l, *, batch):
    m = x.shape[0]
    t = TILES_PROMPT
    lam_init = 0.8 - 0.6 * math.exp(-0.3 * l)
    h = _rmsnorm(x, lw["norm_mix"], tm=t["norm"])
    pr = _projections(h, wts, lw, l, tm=t["proj"][0], tn=t["proj"][1])
    hm, c1, n1, m1 = _mlstm_prompt(pr["mqkv"], pr["mif"], pr["mo"], lw["mlstm_norm"], batch=batch)
    hd = _attn_prompt(pr["dq16"], pr["dk16"], pr["dv16"], bias_tiles, lw["lambda_q"], lw["lambda_k"],
                      lw["diff_subln"], batch=batch, lam_init=lam_init)
    hp, pool_new = _pool_prompt(pr["pu"], lw["pool_w"], lw["pool_scale"], batch=batch, tm=t["pool"])
    merged = _merge(hm, hd, hp, wts["w_br_mlstm"], wts["w_br_diff"], wts["w_br_pool"], pr["gates"],
                    layer=l, tm=t["merge"][0], tn=t["merge"][1])
    (x1,) = _matmul(merged, p["w_out"], (x,), _ep_residual, (F32,), layer=l, tm=t["out"][0], tn=t["out"][1],
                    name="out_proj")
    h2 = _rmsnorm(x1, lw["norm_ffn"], tm=t["norm"])
    act, conv_new = _ffn_up_prompt(h2, p["w_up"], lw["conv_w"], lw["conv_b"], layer=l, batch=batch,
                                   tm=t["ffn_up"][0], tn=t["ffn_up"][1])
    (x2,) = _matmul(act, wts["w_down"], (x1,), _ep_residual, (F32,), layer=l, tm=t["ffn_down"][0],
                    tn=t["ffn_down"][1], name="ffn_down")
    seq = m // batch
    kv_shape = (batch, seq, DIFF_HEADS, DIFF_DV)
    return x2, pr["dk"].reshape(kv_shape), pr["dv"].reshape(kv_shape), c1, n1, m1, pool_new, conv_new


def _layer_sample(x, p, wts, lw, bias_tab, l, cache_k, cache_v, page_table, c0, n0, m0, pool_buf, conv_buf):
    b = x.shape[0]
    t = TILES_SAMPLE
    lam_init = 0.8 - 0.6 * math.exp(-0.3 * l)
    h = _rmsnorm(x, lw["norm_mix"], tm=b)
    pr = _projections(h, wts, lw, l, tm=b, tn=t["proj"])
    hm, c1, n1, m1 = _mlstm_sample(pr["mqkv"], pr["mif"], pr["mo"], lw["mlstm_norm"], c0, n0, m0)
    hd = _attn_sample(pr["dq"], pr["dk"], pr["dv"], cache_k, cache_v, l, page_table, bias_tab,
                      lw["lambda_q"], lw["lambda_k"], lw["diff_subln"], lam_init=lam_init)
    hp, pool_new = _pool_sample(pr["pu"], pool_buf, lw["pool_w"], lw["pool_scale"])
    merged = _merge(hm, hd, hp, wts["w_br_mlstm"], wts["w_br_diff"], wts["w_br_pool"], pr["gates"],
                    layer=l, tm=b, tn=t["merge"])
    (x1,) = _matmul(merged, p["w_out"], (x,), _ep_residual, (F32,), layer=l, tm=b, tn=t["out"],
                    name="out_proj_s")
    h2 = _rmsnorm(x1, lw["norm_ffn"], tm=b)
    act, conv_new = _ffn_up_sample(h2, p["w_up"], lw["conv_w"], lw["conv_b"], conv_buf, layer=l, tn=t["ffn_up"])
    (x2,) = _matmul(act, wts["w_down"], (x1,), _ep_residual, (F32,), layer=l, tm=b, tn=t["ffn_down"],
                    name="ffn_down_s")
    kv_shape = (b, 1, DIFF_HEADS, DIFF_DV)
    return x2, pr["dk"].reshape(kv_shape), pr["dv"].reshape(kv_shape), c1, n1, m1, pool_new, conv_new


def kernel(x_prompt, x_sample, cache_k, cache_v, page_table, state_mlstm_c, state_mlstm_n, state_mlstm_m,
           state_pool, state_conv, rel_bias, norm_mix, w_in, b_in, mlstm_norm, q_norm, k_norm, lambda_q,
           lambda_k, diff_subln, pool_w, pool_scale, w_br_mlstm, w_br_diff, w_br_pool, w_out, norm_ffn,
           w_up, conv_w, conv_b, w_down):
    params = dict(norm_mix=norm_mix, w_in=w_in, b_in=b_in, mlstm_norm=mlstm_norm, q_norm=q_norm,
                  k_norm=k_norm, lambda_q=lambda_q, lambda_k=lambda_k, diff_subln=diff_subln,
                  pool_w=pool_w, pool_scale=pool_scale, w_br_mlstm=w_br_mlstm, w_br_diff=w_br_diff,
                  w_br_pool=w_br_pool, w_out=w_out, norm_ffn=norm_ffn, w_up=w_up, conv_w=conv_w,
                  conv_b=conv_b, w_down=w_down)
    depth = w_in.shape[0]
    n_prompt, seq, d = x_prompt.shape
    n_dec = x_sample.shape[0]
    assert x_sample.shape[1] == 1, "sample group decodes one token per sequence"
    bias_tab = _bias_by_distance(rel_bias)
    bias_tiles = _bias_tiles(rel_bias, min(ATTN_TQ, seq))
    ck = cache_k.reshape(cache_k.shape[:2] + (PAGE_ROWS, 2 * DIFF_DQK))
    cv = cache_v.reshape(cache_v.shape[:2] + (PAGE_ROWS, DIFF_DV))
    xp = x_prompt.reshape(n_prompt * seq, d)
    xs = x_sample.reshape(n_dec, d)
    outs_p, outs_s = [], []
    wts = _prep_weights(params)
    for l in range(depth):
        lw = _prep_layer(params, l)
        xp, *rest_p = _layer_prompt(xp, params, wts, lw, bias_tiles, l, batch=n_prompt)
        xs, *rest_s = _layer_sample(xs, params, wts, lw, bias_tab, l, ck, cv, page_table, state_mlstm_c[l],
                                    state_mlstm_n[l], state_mlstm_m[l], state_pool[l], state_conv[l])
        outs_p.append(rest_p)
        outs_s.append(rest_s)
    stack = lambda outs, idx: jnp.stack([o[idx] for o in outs])
    kp, vp, cp, np_, mp, pp, cvp = (stack(outs_p, i) for i in range(7))
    ks, vs, cs_, ns, ms, ps, cvs = (stack(outs_s, i) for i in range(7))
    return (xp.reshape(n_prompt, seq, d), xs.reshape(n_dec, 1, d),
            kp, vp, ks, vs, cp, np_, mp, cs_, ns, ms, pp, ps, cvp, cvs)
```
